```python
import jax, jax.numpy as jnp
from jax import lax
import numpy as np

D_MODEL = 1024
BATCH = 16
SEQ = 2048
DEPTH = 2
DEC_BATCH = 32
DEC_SEQ = 64
PAST_LEN = 4096

CHUNK = 64
N_MIXERS = 2
N_RET_LAYERS = (DEPTH + 1) // 2
N_POOL_LAYERS = DEPTH // 2
RET_HEADS = 4
RET_QK_DIM = 256
RET_V_DIM = 512
RET_QK = RET_HEADS * RET_QK_DIM
RET_V = RET_HEADS * RET_V_DIM
RET_IN = 2 * RET_QK + 2 * RET_V
ROPE_BASE = 10000.0
POOL_WINDOWS = (2, 4, 8, 16)
POOL_GROUPS = 4
POOL_GC = D_MODEL // POOL_GROUPS
POOL_STATE = 16 - 1
N_EXPERTS = 32
TOP_K = 4
D_FF = D_MODEL
SWIGLU_LIMIT = 7.0
SWIGLU_ALPHA = 1.702
MOE_BLOCK = 128
ALPHA = (2 * DEPTH) ** 0.25
BETA = (8 * DEPTH) ** -0.25
LN_EPS = 1e-5
RMS_EPS = 1e-6

kernel_name = "hybrid_retention_pool_moe_stream_step"

F32 = jnp.float32


def layer_norm(x, g, b):
    xf = x.astype(F32)
    mu = jnp.mean(xf, axis=-1, keepdims=True)
    xc = xf - mu
    var = jnp.mean(xc * xc, axis=-1, keepdims=True)
    return (xc * lax.rsqrt(var + LN_EPS) * g.astype(F32) + b.astype(F32)).astype(x.dtype)


def rotary(t, pos):
    half = t.shape[-1] // 2
    inv = 1.0 / (ROPE_BASE ** jnp.linspace(0.0, 1.0, half, dtype=F32))
    ang = pos.astype(F32)[:, None] * inv[None, :]
    cos = jnp.cos(ang)[None, :, None, :]
    sin = jnp.sin(ang)[None, :, None, :]
    t1 = t[..., :half].astype(F32)
    t2 = t[..., half:].astype(F32)
    return jnp.concatenate([t1 * cos - t2 * sin, t1 * sin + t2 * cos], axis=-1)


def log_gamma():
    return jnp.log1p(-jnp.exp2(-5.0 - jnp.arange(RET_HEADS, dtype=F32)))


def ret_project(x, w_in, pos):
    B, L, _ = x.shape
    p = x @ w_in
    q = p[..., :RET_QK].reshape(B, L, RET_HEADS, RET_QK_DIM)
    k = p[..., RET_QK:2 * RET_QK].reshape(B, L, RET_HEADS, RET_QK_DIM)
    v = p[..., 2 * RET_QK:2 * RET_QK + RET_V].reshape(B, L, RET_HEADS, RET_V_DIM)
    g = p[..., 2 * RET_QK + RET_V:]
    q = rotary(q, pos).transpose(0, 2, 1, 3)
    k = (rotary(k, pos) * (RET_QK_DIM ** -0.5)).transpose(0, 2, 1, 3)
    v = v.astype(F32).transpose(0, 2, 1, 3)
    return q, k, v, g


def retention_block(S, q, k, v):
    L = q.shape[2]
    lg = log_gamma()
    idx = jnp.arange(L, dtype=F32)
    intra = jnp.exp(lg[:, None, None] * jnp.abs(idx[:, None] - idx[None, :]))
    q_dec = jnp.exp(lg[:, None] * (idx + 1.0))
    k_dec = jnp.exp(lg[:, None] * (L - 1.0 - idx))
    blk_dec = jnp.exp(lg * L)
    scores = jnp.einsum('bhid,bhjd->bhij', q, k) * intra
    o = (jnp.einsum('bhij,bhjv->bhiv', scores, v)
         + jnp.einsum('bhid,bhdv->bhiv', q * q_dec[..., None], S))
    S_new = S * blk_dec[:, None, None] + jnp.einsum('bhjd,bhjv->bhdv', k * k_dec[..., None], v)
    return S_new, o


def ret_output(o, g, w_out):
    B, H, L, dv = o.shape
    o = o * lax.rsqrt(jnp.mean(o * o, axis=-1, keepdims=True) + RMS_EPS)
    o = o.transpose(0, 2, 1, 3).reshape(B, L, H * dv).astype(g.dtype)
    return (jax.nn.silu(g) * o) @ w_out


def retention_prompt(x, w_in, w_out):
    B, L, _ = x.shape
    q, k, v, g = ret_project(x, w_in, jnp.arange(L))
    nc = L // CHUNK

    def to_chunks(t):
        return t.reshape(B, RET_HEADS, nc, CHUNK, t.shape[-1]).transpose(2, 0, 1, 3, 4)

    S0 = jnp.zeros((B, RET_HEADS, RET_QK_DIM, RET_V_DIM), F32)
    S, o = lax.scan(lambda s, qkv: retention_block(s, *qkv), S0,
                    (to_chunks(q), to_chunks(k), to_chunks(v)))
    o = o.transpose(1, 2, 0, 3, 4).reshape(B, RET_HEADS, L, RET_V_DIM)
    return ret_output(o, g, w_out), S


def retention_sample(x, S, w_in, w_out):
    L = x.shape[1]
    q, k, v, g = ret_project(x, w_in, PAST_LEN + jnp.arange(L))
    S_new, o = retention_block(S.astype(F32), q, k, v)
    return ret_output(o, g, w_out), S_new


def pool_mixer(x, prefix, pos0, w_in, w_grp, scale, w_out):
    B, L, _ = x.shape
    P = POOL_STATE
    u = x @ w_in
    full = jnp.concatenate([prefix.astype(u.dtype), u], axis=1)
    fullf = full.astype(F32)
    csum = jnp.concatenate([jnp.zeros((B, 1, D_MODEL), F32), jnp.cumsum(fullf, axis=1)], axis=1)
    pos = pos0 + jnp.arange(L, dtype=F32)
    groups = []
    for gi, w in enumerate(POOL_WINDOWS):
        cs = csum[:, :, gi * POOL_GC:(gi + 1) * POOL_GC]
        wsum = cs[:, P + 1:P + 1 + L] - cs[:, P + 1 - w:P + 1 - w + L]
        cnt = jnp.minimum(pos + 1.0, float(w))
        groups.append(wsum / cnt[None, :, None])
    pooled = jnp.stack(groups, axis=2) - fullf[:, P:].reshape(B, L, POOL_GROUPS, POOL_GC)
    mixed = jnp.einsum('blgc,gcd->blgd', pooled.astype(x.dtype), w_grp).reshape(B, L, D_MODEL) * scale
    return mixed @ w_out, full[:, -P:]


def moe_ffn(x2d, router_w, router_b, w_up, b_up, w_down, b_down):
    T = x2d.shape[0]
    logits = jnp.dot(x2d.astype(F32), router_w.astype(F32)) + router_b.astype(F32)
    top_v, top_e = lax.top_k(logits, TOP_K)
    gates = jax.nn.softmax(top_v, axis=-1)
    n = T * TOP_K
    flat_e = top_e.reshape(n)
    order = jnp.argsort(flat_e)
    sorted_e = flat_e[order]
    counts = jnp.bincount(flat_e, length=N_EXPERTS)
    padded = (counts + MOE_BLOCK - 1) // MOE_BLOCK * MOE_BLOCK
    pad_end = jnp.cumsum(padded)
    pad_start = pad_end - padded
    start = jnp.cumsum(counts) - counts
    dest = pad_start[sorted_e] + jnp.arange(n) - start[sorted_e]
    n_blocks = -(-n // MOE_BLOCK) + N_EXPERTS
    n_rows = n_blocks * MOE_BLOCK
    row_tok = jnp.full((n_rows,), T, jnp.int32).at[dest].set((order // TOP_K).astype(jnp.int32))
    row_gate = jnp.zeros((n_rows,), F32).at[dest].set(gates.reshape(n)[order])
    block_e = jnp.minimum(jnp.searchsorted(pad_end, jnp.arange(n_blocks) * MOE_BLOCK, side='right'),
                          N_EXPERTS - 1)
    x_pad = jnp.concatenate([x2d, jnp.zeros((1, D_MODEL), x2d.dtype)], axis=0)
    xb = x_pad[row_tok].reshape(n_blocks, MOE_BLOCK, D_MODEL)

    def expert(args):
        xe, e = args
        h = xe @ w_up[e] + b_up[e]
        glu = jnp.minimum(h[:, 0::2], SWIGLU_LIMIT)
        lin = jnp.clip(h[:, 1::2], -SWIGLU_LIMIT, SWIGLU_LIMIT)
        a = glu * jax.nn.sigmoid(SWIGLU_ALPHA * glu) * (lin + 1.0)
        return a @ w_down[e] + b_down[e]

    yb = lax.map(expert, (xb, block_e)).reshape(n_rows, D_MODEL)
    y = jnp.zeros((T + 1, D_MODEL), F32).at[row_tok].add(yb.astype(F32) * row_gate[:, None])
    return y[:T].astype(x2d.dtype)


def setup_inputs(seed: int = 0) -> dict:
    key = jax.random.key(seed)
    ks = jax.random.split(key, 20)

    def nrm(k, shape, scale):
        return jax.random.normal(k, shape, F32) * scale

    return {
        "x_prompt": nrm(ks[0], (BATCH, SEQ, D_MODEL), 1.0),
        "x_sample": nrm(ks[1], (DEC_BATCH, DEC_SEQ, D_MODEL), 1.0),
        "state_ret": nrm(ks[2], (N_RET_LAYERS, DEC_BATCH, RET_HEADS, RET_QK_DIM, RET_V_DIM), 0.1),
        "cache_pool": nrm(ks[3], (N_POOL_LAYERS, DEC_BATCH, POOL_STATE, D_MODEL), 1.0),
        "ret_w_in": nrm(ks[4], (N_RET_LAYERS, D_MODEL, RET_IN), D_MODEL ** -0.5),
        "ret_w_out": nrm(ks[5], (N_RET_LAYERS, RET_V, D_MODEL), BETA * RET_V ** -0.5),
        "pool_w_in": nrm(ks[6], (N_POOL_LAYERS, D_MODEL, D_MODEL), D_MODEL ** -0.5),
        "pool_w_grp": nrm(ks[7], (N_POOL_LAYERS, POOL_GROUPS, POOL_GC, POOL_GC), POOL_GC ** -0.5),
        "pool_scale": 1.0 + nrm(ks[8], (N_POOL_LAYERS, D_MODEL), 0.02),
        "pool_w_out": nrm(ks[9], (N_POOL_LAYERS, D_MODEL, D_MODEL), BETA * D_MODEL ** -0.5),
        "ln1_g": 1.0 + nrm(ks[10], (DEPTH, D_MODEL), 0.02),
        "ln1_b": nrm(ks[11], (DEPTH, D_MODEL), 0.02),
        "ln2_g": 1.0 + nrm(ks[12], (DEPTH, D_MODEL), 0.02),
        "ln2_b": nrm(ks[13], (DEPTH, D_MODEL), 0.02),
        "router_w": nrm(ks[14], (DEPTH, D_MODEL, N_EXPERTS), D_MODEL ** -0.5),
        "router_b": nrm(ks[15], (DEPTH, N_EXPERTS), 0.01),
        "w_up": nrm(ks[16], (DEPTH, N_EXPERTS, D_MODEL, 2 * D_FF), D_MODEL ** -0.5),
        "b_up": nrm(ks[17], (DEPTH, N_EXPERTS, 2 * D_FF), 0.01),
        "w_down": nrm(ks[18], (DEPTH, N_EXPERTS, D_FF, D_MODEL), BETA * D_FF ** -0.5),
        "b_down": nrm(ks[19], (DEPTH, N_EXPERTS, D_MODEL), 0.01),
    }


def reference(x_prompt, x_sample, state_ret, cache_pool, ret_w_in, ret_w_out, pool_w_in, pool_w_grp,
              pool_scale, pool_w_out, ln1_g, ln1_b, ln2_g, ln2_b, router_w, router_b, w_up, b_up,
              w_down, b_down):
    xp, xs = x_prompt, x_sample
    Bp, Lp, _ = xp.shape
    Bs, Ls, _ = xs.shape
    ret_p, ret_s, pool_p, pool_s = [], [], [], []
    for i in range(DEPTH):
        j = i // N_MIXERS
        if i % N_MIXERS == 0:
            hp, sp = retention_prompt(xp, ret_w_in[j], ret_w_out[j])
            hs, ss = retention_sample(xs, state_ret[j], ret_w_in[j], ret_w_out[j])
            ret_p.append(sp)
            ret_s.append(ss)
        else:
            hp, rp = pool_mixer(xp, jnp.zeros((Bp, POOL_STATE, D_MODEL), xp.dtype), 0,
                                pool_w_in[j], pool_w_grp[j], pool_scale[j], pool_w_out[j])
            hs, rs = pool_mixer(xs, cache_pool[j], PAST_LEN,
                                pool_w_in[j], pool_w_grp[j], pool_scale[j], pool_w_out[j])
            pool_p.append(rp)
            pool_s.append(rs)
        xp = layer_norm(ALPHA * xp + hp, ln1_g[i], ln1_b[i])
        xs = layer_norm(ALPHA * xs + hs, ln1_g[i], ln1_b[i])
        tok = jnp.concatenate([xp.reshape(Bp * Lp, D_MODEL), xs.reshape(Bs * Ls, D_MODEL)], axis=0)
        m = moe_ffn(tok, router_w[i], router_b[i], w_up[i], b_up[i], w_down[i], b_down[i])
        mp = m[:Bp * Lp].reshape(Bp, Lp, D_MODEL)
        ms = m[Bp * Lp:].reshape(Bs, Ls, D_MODEL)
        xp = layer_norm(ALPHA * xp + mp, ln2_g[i], ln2_b[i])
        xs = layer_norm(ALPHA * xs + ms, ln2_g[i], ln2_b[i])
    state_ret_prompt = jnp.stack(ret_p)
    state_ret_sample = jnp.stack(ret_s)
    cache_pool_prompt = jnp.stack(pool_p)
    cache_pool_sample = jnp.stack(pool_s)
    return (xp, xs, state_ret_prompt, state_ret_sample, cache_pool_prompt, cache_pool_sample)
```

```python
import functools

import jax
import jax.numpy as jnp
from jax import lax
from jax.experimental import pallas as pl
from jax.experimental.pallas import tpu as pltpu

F32 = jnp.float32
BF16 = jnp.bfloat16
I32 = jnp.int32

D = 1024
BATCH = 16
SEQ = 2048
DEC_BATCH = 32
DEC_SEQ = 64
PAST_LEN = 4096
TP = BATCH * SEQ
TS = DEC_BATCH * DEC_SEQ
T = TP + TS

HEADS = 4
DK = 256
DV = 512
RET_QK = HEADS * DK
RET_V = HEADS * DV
RET_IN = 2 * RET_QK + 2 * RET_V
ROPE_BASE = 10000.0
RMS_EPS = 1e-6
LN_EPS = 1e-5
ALPHA = 4.0 ** 0.25

POOL_WINDOWS = (2, 4, 8, 16)
POOL_GC = D // 4
POOL_STATE = 15
CARRY = 16

N_EXPERTS = 32
TOP_K = 4
SWIGLU_LIMIT = 7.0
SWIGLU_ALPHA = 1.702

LANES = 128
SUBLANES = 8
SUPER = 256
CHUNK = 64

TM_PROJ = 1024
TM = 512
TM_COMB = 256
BM = 512
NB = -(-(T * TOP_K) // BM) + N_EXPERTS
NB_PAD = -(-NB // 8) * 8
R_ROWS = NB * BM

VMEM_LIMIT = 56 * 1024 * 1024


def _cparams(sem, vmem=VMEM_LIMIT):
    return pltpu.CompilerParams(dimension_semantics=sem, vmem_limit_bytes=vmem)


def _proj_kernel(x_ref, w_ref, cos_ref, sin_ref, o_ref):
    n = pl.program_id(1)
    xb = x_ref[...].astype(BF16)

    @pl.when(n < 2)
    def _():
        scale = jnp.where(n == 1, DK ** -0.5, 1.0).astype(F32)
        cos = cos_ref[...] * scale
        sin = sin_ref[...] * scale
        half = DK // 2
        for h in range(HEADS):
            p = jnp.dot(xb, w_ref[:, h * DK:(h + 1) * DK], preferred_element_type=F32)
            t1 = p[:, :half]
            t2 = p[:, half:]
            o_ref[:, h * DK:h * DK + half] = (t1 * cos - t2 * sin).astype(BF16)
            o_ref[:, h * DK + half:(h + 1) * DK] = (t1 * sin + t2 * cos).astype(BF16)

    @pl.when(n >= 2)
    def _():
        for h in range(4):
            p = jnp.dot(xb, w_ref[:, h * 256:(h + 1) * 256], preferred_element_type=F32)
            o_ref[:, h * 256:(h + 1) * 256] = p.astype(BF16)


def _ret_project(x, w_bf16, cos_t, sin_t):
    n_prompt_tiles = TP // TM_PROJ
    tiles_per_seq = SEQ // TM_PROJ

    def tab_map(i, n):
        return (jnp.where(i < n_prompt_tiles, i % tiles_per_seq, tiles_per_seq), 0)

    return pl.pallas_call(
        _proj_kernel,
        grid=(T // TM_PROJ, RET_IN // 1024),
        in_specs=[
            pl.BlockSpec((TM_PROJ, D), lambda i, n: (i, 0)),
            pl.BlockSpec((D, 1024), lambda i, n: (0, n)),
            pl.BlockSpec((TM_PROJ, LANES), tab_map),
            pl.BlockSpec((TM_PROJ, LANES), tab_map),
        ],
        out_specs=pl.BlockSpec((TM_PROJ, 1024), lambda i, n: (i, n)),
        out_shape=jax.ShapeDtypeStruct((T, RET_IN), BF16),
        compiler_params=_cparams(("arbitrary", "arbitrary")),
        name="ret_project",
    )(x, w_bf16, cos_t, sin_t)


def _ret_core_kernel(has_init, n_steps, *refs):
    if has_init:
        (q_ref, k_ref, v_ref, mask_ref, qd_ref, kd_ref, bd_ref, s0_ref,
         o_ref, sout_ref, s_ref) = refs
    else:
        (q_ref, k_ref, v_ref, mask_ref, qd_ref, kd_ref, bd_ref,
         o_ref, sout_ref, s_ref) = refs
    c = pl.program_id(2)

    @pl.when(c == 0)
    def _():
        if has_init:
            s_ref[...] = s0_ref[0].astype(F32)
        else:
            s_ref[...] = jnp.zeros_like(s_ref)

    q = q_ref[...]
    k = k_ref[...]
    v = v_ref[...]
    s_prev = s_ref[...]
    scores = lax.dot_general(q, k, (((1,), (1,)), ((), ())), preferred_element_type=F32)
    scores = scores * mask_ref[0]
    qd = (q.astype(F32) * qd_ref[0]).astype(BF16)
    o = (jnp.dot(scores.astype(BF16), v, preferred_element_type=F32)
         + jnp.dot(qd, s_prev.astype(BF16), preferred_element_type=F32))
    kd = (k.astype(F32) * kd_ref[0]).astype(BF16)
    s_new = s_prev * bd_ref[0] + lax.dot_general(kd, v, (((0,), (0,)), ((), ())),
                                                 preferred_element_type=F32)
    s_ref[...] = s_new
    o = o * lax.rsqrt(jnp.mean(o * o, axis=-1, keepdims=True) + RMS_EPS)
    o_ref[...] = o.astype(BF16)

    @pl.when(c == n_steps - 1)
    def _():
        sout_ref[0] = s_new


def _decay_tables(rows):
    lg = jnp.log1p(-jnp.exp2(-5.0 - jnp.arange(HEADS, dtype=F32)))
    idx = jnp.arange(rows, dtype=F32)
    ch = jnp.arange(rows) // CHUNK
    diff = idx[:, None] - idx[None, :]
    same = ch[:, None] == ch[None, :]
    earlier = ch[None, :] < ch[:, None]
    expo = jnp.where(same, jnp.abs(diff), diff)
    w = jnp.exp(lg[:, None, None] * expo[None])
    mask = jnp.where((same | earlier)[None], w, 0.0).astype(F32)
    q_dec = jnp.exp(lg[:, None] * (idx + 1.0))[:, :, None]
    k_dec = jnp.exp(lg[:, None] * (rows - 1.0 - idx))[:, :, None]
    blk = jnp.broadcast_to(jnp.exp(lg * rows)[:, None, None], (HEADS, 1, DV))
    return mask, q_dec, k_dec, blk.astype(F32)


def _ret_core(p_all, n_seq, rows, n_steps, row_block0, s0=None):
    mask, q_dec, k_dec, blk = _decay_tables(rows)
    has_init = s0 is not None

    def rb(b, c):
        return row_block0 + b * n_steps + c

    in_specs = [
        pl.BlockSpec((rows, DK), lambda b, h, c: (rb(b, c), h)),
        pl.BlockSpec((rows, DK), lambda b, h, c: (rb(b, c), HEADS + h)),
        pl.BlockSpec((rows, DV), lambda b, h, c: (rb(b, c), HEADS + h)),
        pl.BlockSpec((1, rows, rows), lambda b, h, c: (h, 0, 0)),
        pl.BlockSpec((1, rows, 1), lambda b, h, c: (h, 0, 0)),
        pl.BlockSpec((1, rows, 1), lambda b, h, c: (h, 0, 0)),
        pl.BlockSpec((1, 1, DV), lambda b, h, c: (h, 0, 0)),
    ]
    args = [p_all, p_all, p_all, mask, q_dec, k_dec, blk]
    if has_init:
        in_specs.append(pl.BlockSpec((1, DK, DV), lambda b, h, c: (b * HEADS + h, 0, 0)))
        args.append(s0)
    return pl.pallas_call(
        functools.partial(_ret_core_kernel, has_init, n_steps),
        grid=(n_seq, HEADS, n_steps),
        in_specs=in_specs,
        out_specs=[
            pl.BlockSpec((rows, DV), lambda b, h, c: (b * n_steps + c, h)),
            pl.BlockSpec((1, DK, DV), lambda b, h, c: (b * HEADS + h, 0, 0)),
        ],
        out_shape=[
            jax.ShapeDtypeStruct((n_seq * n_steps * rows, RET_V), BF16),
            jax.ShapeDtypeStruct((n_seq * HEADS, DK, DV), F32),
        ],
        scratch_shapes=[pltpu.VMEM((DK, DV), F32)],
        compiler_params=_cparams(("arbitrary", "arbitrary", "arbitrary")),
        name="ret_core_sample" if has_init else "ret_core_prompt",
    )(*args)


def _layer_norm(y, g, b):
    mu = jnp.mean(y, axis=-1, keepdims=True)
    yc = y - mu
    var = jnp.mean(yc * yc, axis=-1, keepdims=True)
    return yc * lax.rsqrt(var + LN_EPS) * g + b


def _split_bf16(a):
    hi = a.astype(BF16)
    lo = (a - hi.astype(F32)).astype(BF16)
    return hi, lo


def _route(x1, rw_ref, rb_ref, te_ref, gt_ref):
    rows = x1.shape[0]
    xh, xl = _split_bf16(x1)
    wh, wl = _split_bf16(rw_ref[...])
    logits = (jnp.dot(xh, wh, preferred_element_type=F32)
              + jnp.dot(xl, wh, preferred_element_type=F32)
              + jnp.dot(xh, wl, preferred_element_type=F32)) + rb_ref[...]
    lane = lax.broadcasted_iota(I32, (rows, N_EXPERTS), 1)
    lane_k = lax.broadcasted_iota(I32, (rows, TOP_K), 1)
    te = jnp.zeros((rows, TOP_K), I32)
    tv = jnp.zeros((rows, TOP_K), F32)
    cur = logits
    for j in range(TOP_K):
        m = jnp.max(cur, axis=-1, keepdims=True)
        idx = jnp.min(jnp.where(cur == m, lane, N_EXPERTS), axis=-1, keepdims=True)
        te = jnp.where(lane_k == j, idx, te)
        tv = jnp.where(lane_k == j, m, tv)
        cur = jnp.where(lane == idx, -jnp.inf, cur)
    ex = jnp.exp(tv - jnp.max(tv, axis=-1, keepdims=True))
    gt_ref[...] = ex / jnp.sum(ex, axis=-1, keepdims=True)
    te_ref[...] = te


def _residual_norm_route(x, h, lg_ref, lb_ref, rw_ref, rb_ref, x1_ref, te_ref, gt_ref):
    x1 = _layer_norm(ALPHA * x + h, lg_ref[...], lb_ref[...])
    x1_ref[...] = x1
    _route(x1, rw_ref, rb_ref, te_ref, gt_ref)


def _ret_out_kernel(op_ref, os_ref, g_ref, x_ref, w_ref, lg_ref, lb_ref, rw_ref, rb_ref,
                    x1_ref, te_ref, gt_ref):
    g = g_ref[...].astype(F32)
    o = jnp.where(pl.program_id(0) < TP // TM, op_ref[...], os_ref[...])
    a = (g * jax.nn.sigmoid(g) * o.astype(F32)).astype(BF16)
    h = jnp.dot(a, w_ref[...], preferred_element_type=F32)
    _residual_norm_route(x_ref[...], h, lg_ref, lb_ref, rw_ref, rb_ref, x1_ref, te_ref, gt_ref)


def _row_spec(cols, col_block=0):
    return pl.BlockSpec((TM, cols), lambda i: (i, col_block))


def _const_spec(shape):
    nd = len(shape)
    return pl.BlockSpec(shape, lambda i: (0,) * nd)


_EPILOGUE_OUT_SPECS = [_row_spec(D), _row_spec(TOP_K), _row_spec(TOP_K)]
_EPILOGUE_OUT_SHAPE = [
    jax.ShapeDtypeStruct((T, D), F32),
    jax.ShapeDtypeStruct((T, TOP_K), I32),
    jax.ShapeDtypeStruct((T, TOP_K), F32),
]


def _ret_out(o_prompt, o_sample, p_all, x, w_out_bf16, ln_g, ln_b, router_w, router_b):
    n_prompt_tiles = TP // TM
    return pl.pallas_call(
        _ret_out_kernel,
        grid=(T // TM,),
        in_specs=[
            pl.BlockSpec((TM, RET_V), lambda i: (jnp.minimum(i, n_prompt_tiles - 1), 0)),
            pl.BlockSpec((TM, RET_V), lambda i: (jnp.maximum(i - n_prompt_tiles, 0), 0)),
            _row_spec(RET_V, col_block=2),
            _row_spec(D),
            _const_spec((RET_V, D)),
            _const_spec((1, D)), _const_spec((1, D)),
            _const_spec((D, N_EXPERTS)), _const_spec((1, N_EXPERTS)),
        ],
        out_specs=_EPILOGUE_OUT_SPECS,
        out_shape=_EPILOGUE_OUT_SHAPE,
        compiler_params=_cparams(("arbitrary",)),
        name="ret_out_norm_route",
    )(o_prompt, o_sample, p_all, x, w_out_bf16, ln_g, ln_b, router_w, router_b)


SEQS_PER_TILE = TM // DEC_SEQ


def _group_cols(gi):
    return slice(gi * POOL_GC, (gi + 1) * POOL_GC)


def _window_pool(src_ref, rows, pos, dst_ref, row0):
    for gi, w in enumerate(POOL_WINDOWS):
        cols = _group_cols(gi)
        cur = src_ref[CARRY:CARRY + rows, cols]
        acc = cur
        for back in range(1, w):
            acc = acc + src_ref[CARRY - back:CARRY - back + rows, cols]
        cnt = jnp.minimum(pos + 1.0, float(w))
        dst_ref[row0:row0 + rows, cols] = acc / cnt - cur


def _pool_kernel(x_ref, pre_ref, win_ref, wg_ref, sc_ref, wout_ref, lg_ref, lb_ref, rw_ref, rb_ref,
                 x1_ref, te_ref, gt_ref, cachep_ref, caches_ref, full_ref, seq_ref, pooled_ref):
    i = pl.program_id(0)
    n_prompt_tiles = TP // TM
    tiles_per_seq = SEQ // TM
    x = x_ref[...]
    u = jnp.dot(x.astype(BF16), win_ref[...], preferred_element_type=F32)

    @pl.when(i < n_prompt_tiles)
    def _():
        j = i % tiles_per_seq

        @pl.when(j == 0)
        def _():
            full_ref[0:CARRY, :] = jnp.zeros((CARRY, D), F32)

        full_ref[CARRY:CARRY + TM, :] = u
        pos = (j * TM + lax.broadcasted_iota(I32, (TM, 1), 0)).astype(F32)
        _window_pool(full_ref, TM, pos, pooled_ref, 0)

        @pl.when(j == tiles_per_seq - 1)
        def _():
            cachep_ref[0] = full_ref[TM:CARRY + TM, :]

        full_ref[0:CARRY, :] = full_ref[TM:CARRY + TM, :]

    @pl.when(i >= n_prompt_tiles)
    def _():
        pos = (PAST_LEN + lax.broadcasted_iota(I32, (DEC_SEQ, 1), 0)).astype(F32)
        seq_ref[0:1, :] = jnp.zeros((1, D), F32)
        for r in range(SEQS_PER_TILE):
            seq_ref[1:CARRY, :] = pre_ref[r]
            seq_ref[CARRY:CARRY + DEC_SEQ, :] = u[r * DEC_SEQ:(r + 1) * DEC_SEQ, :]
            _window_pool(seq_ref, DEC_SEQ, pos, pooled_ref, r * DEC_SEQ)
            caches_ref[r] = seq_ref[DEC_SEQ:CARRY + DEC_SEQ, :]

    mixed = [jnp.dot(pooled_ref[:, _group_cols(gi)].astype(BF16), wg_ref[gi],
                     preferred_element_type=F32) for gi in range(len(POOL_WINDOWS))]
    mixed = jnp.concatenate(mixed, axis=-1) * sc_ref[...]
    h = jnp.dot(mixed.astype(BF16), wout_ref[...], preferred_element_type=F32)
    _residual_norm_route(x, h, lg_ref, lb_ref, rw_ref, rb_ref, x1_ref, te_ref, gt_ref)


def _pool_mixer(x, prefix, win, wgrp, scale, wout, ln_g, ln_b, router_w, router_b):
    n_prompt_tiles = TP // TM
    tiles_per_seq = SEQ // TM
    return pl.pallas_call(
        _pool_kernel,
        grid=(T // TM,),
        in_specs=[
            _row_spec(D),
            pl.BlockSpec((SEQS_PER_TILE, POOL_STATE, D),
                         lambda i: (jnp.maximum(i - n_prompt_tiles, 0), 0, 0)),
            _const_spec((D, D)), _const_spec((len(POOL_WINDOWS), POOL_GC, POOL_GC)),
            _const_spec((1, D)), _const_spec((D, D)),
            _const_spec((1, D)), _const_spec((1, D)),
            _const_spec((D, N_EXPERTS)), _const_spec((1, N_EXPERTS)),
        ],
        out_specs=_EPILOGUE_OUT_SPECS + [
            pl.BlockSpec((1, CARRY, D),
                         lambda i: (jnp.minimum(i, n_prompt_tiles - 1) // tiles_per_seq, 0, 0)),
            pl.BlockSpec((SEQS_PER_TILE, CARRY, D),
                         lambda i: (jnp.maximum(i - n_prompt_tiles, 0), 0, 0)),
        ],
        out_shape=_EPILOGUE_OUT_SHAPE + [
            jax.ShapeDtypeStruct((BATCH, CARRY, D), F32),
            jax.ShapeDtypeStruct((DEC_BATCH, CARRY, D), F32),
        ],
        scratch_shapes=[
            pltpu.VMEM((CARRY + TM, D), F32),
            pltpu.VMEM((CARRY + DEC_SEQ, D), F32),
            pltpu.VMEM((TM, D), F32),
        ],
        compiler_params=_cparams(("arbitrary",)),
        name="pool_norm_route",
    )(x, prefix, win, wgrp, scale, wout, ln_g, ln_b, router_w, router_b)


def _lane_cumsum(v):
    lane = lax.broadcasted_iota(I32, v.shape, 1)
    shift = 1
    while shift < LANES:
        v = v + jnp.where(lane >= shift, pltpu.roll(v, shift, axis=1), 0.0)
        shift *= 2
    return v


def _positions_kernel(te_ref, dest_ref, blk_ref, stat_ref, cnt_ref, base_ref):
    phase = pl.program_id(0)
    i = pl.program_id(1)
    te = te_ref[...]
    lane = lax.broadcasted_iota(I32, (TM, LANES), 1)
    hits = [lane == te[:, j:j + 1] for j in range(TOP_K)]
    onehot = sum(h.astype(F32) for h in hits)
    tile_cnt = jnp.sum(onehot, axis=0, keepdims=True)

    @pl.when((phase == 0) & (i == 0))
    def _():
        cnt_ref[...] = jnp.zeros_like(cnt_ref)

    @pl.when(phase == 0)
    def _():
        cnt_ref[...] = cnt_ref[...] + tile_cnt

    @pl.when((phase == 1) & (i == 0))
    def _():
        cnt = cnt_ref[...].astype(I32)
        padded = ((cnt + (BM - 1)) & ~(BM - 1)).astype(F32)
        pad_end = _lane_cumsum(padded)
        base_ref[...] = pad_end - padded
        stat_ref[...] = jnp.concatenate([cnt_ref[...], pad_end - padded, pad_end], axis=0)
        lane8 = lax.broadcasted_iota(I32, (NB_PAD, LANES), 1)
        start = (lax.broadcasted_iota(I32, (NB_PAD, LANES), 0) * BM).astype(F32)
        done = jnp.where((lane8 < N_EXPERTS) & (pad_end[0:1, :] <= start), 1.0, 0.0)
        blk = jnp.minimum(jnp.sum(done, axis=-1, keepdims=True), N_EXPERTS - 1.0)
        blk_ref[...] = blk.astype(I32)

    @pl.when(phase == 1)
    def _():
        r = lax.broadcasted_iota(I32, (TM, TM), 0)
        c = lax.broadcasted_iota(I32, (TM, TM), 1)
        tri = jnp.where(r > c, 1.0, 0.0).astype(BF16)
        before = jnp.dot(tri, onehot.astype(BF16), preferred_element_type=F32)
        slot = base_ref[0:1, :] + before
        lane_k = lax.broadcasted_iota(I32, (TM, TOP_K), 1)
        dest = jnp.zeros((TM, TOP_K), F32)
        for j in range(TOP_K):
            dj = jnp.sum(jnp.where(hits[j], slot, 0.0), axis=-1, keepdims=True)
            dest = jnp.where(lane_k == j, dj, dest)
        dest_ref[...] = dest.astype(I32)
        base_ref[...] = base_ref[...] + tile_cnt


def _positions(top_e):
    return pl.pallas_call(
        _positions_kernel,
        grid=(2, T // TM),
        in_specs=[pl.BlockSpec((TM, TOP_K), lambda p, i: (i, 0))],
        out_specs=[
            pl.BlockSpec((TM, TOP_K), lambda p, i: (i * p, 0)),
            pl.BlockSpec((NB_PAD, 1), lambda p, i: (0, 0)),
            pl.BlockSpec((24, LANES), lambda p, i: (0, 0)),
        ],
        out_shape=[
            jax.ShapeDtypeStruct((T, TOP_K), I32),
            jax.ShapeDtypeStruct((NB_PAD, 1), I32),
            jax.ShapeDtypeStruct((24, LANES), F32),
        ],
        scratch_shapes=[pltpu.VMEM((8, LANES), F32), pltpu.VMEM((8, LANES), F32)],
        compiler_params=_cparams(("arbitrary", "arbitrary")),
        name="moe_positions",
    )(top_e)


ZROWS = 256


def _dispatch_kernel(cnt_ref, start_ref, x_ref, dest_hbm, xs_hbm, idx_ref, zero_ref, sem_idx, sem_row,
                     sem_zero):
    i = pl.program_id(0)
    n_idx = TM * TOP_K
    idx_copy = pltpu.make_async_copy(dest_hbm.at[pl.ds(i * n_idx, n_idx)], idx_ref, sem_idx)
    idx_copy.start()

    @pl.when(i == 0)
    def _():
        zero_ref[...] = jnp.zeros_like(zero_ref)

        def zero_copy(row, size):
            return pltpu.make_async_copy(zero_ref.at[pl.ds(0, size)], xs_hbm.at[pl.ds(row, size)],
                                         sem_zero)

        def fill_expert(e, carry):
            row0 = start_ref[e] + cnt_ref[e]
            n_pad = (-cnt_ref[e]) & (BM - 1)
            n_head = (-row0) & (SUBLANES - 1)
            for r in range(SUBLANES - 1):
                @pl.when(r < n_head)
                def _(r=r):
                    cp = zero_copy(row0 + r, 1)
                    cp.start()
                    cp.wait()

            row = row0 + n_head
            n_rest = n_pad - n_head
            size = ZROWS
            while size >= SUBLANES:
                take = (n_rest & size) != 0

                @pl.when(take)
                def _(row=row, size=size):
                    cp = zero_copy(pl.multiple_of(row, SUBLANES), size)
                    cp.start()
                    cp.wait()

                row = row + jnp.where(take, size, 0)
                size //= 2
            return carry

        lax.fori_loop(0, N_EXPERTS, fill_expert, 0)

        def fill_unused(c, carry):
            cp = zero_copy(pl.multiple_of(c * ZROWS, ZROWS), ZROWS)
            cp.start()
            cp.wait()
            return carry

        last = N_EXPERTS - 1
        used_rows = start_ref[last] + ((cnt_ref[last] + (BM - 1)) & ~(BM - 1))
        lax.fori_loop(used_rows // ZROWS, R_ROWS // ZROWS, fill_unused, 0)

    idx_copy.wait()

    def body(t, carry):
        for j in range(TOP_K):
            d = idx_ref[t * TOP_K + j]
            pltpu.make_async_copy(x_ref.at[pl.ds(t, 1)], xs_hbm.at[pl.ds(d, 1)], sem_row).start()
        return carry

    lax.fori_loop(0, TM, body, 0, unroll=8)
    for j in range(TOP_K):
        pltpu.make_async_copy(x_ref, xs_hbm.at[pl.ds(0, TM)], sem_row).wait()


def _dispatch(x1, dest_flat, counts, starts):
    return pl.pallas_call(
        _dispatch_kernel,
        grid_spec=pltpu.PrefetchScalarGridSpec(
            num_scalar_prefetch=2,
            grid=(T // TM,),
            in_specs=[
                pl.BlockSpec((TM, D), lambda i, c, s: (i, 0)),
                pl.BlockSpec(memory_space=pl.ANY),
            ],
            out_specs=pl.BlockSpec(memory_space=pl.ANY),
            scratch_shapes=[
                pltpu.SMEM((TM * TOP_K,), I32),
                pltpu.VMEM((ZROWS, D), F32),
                pltpu.SemaphoreType.DMA(()),
                pltpu.SemaphoreType.DMA(()),
                pltpu.SemaphoreType.DMA(()),
            ],
        ),
        out_shape=jax.ShapeDtypeStruct((R_ROWS, D), F32),
        compiler_params=_cparams(("arbitrary",)),
        name="moe_dispatch",
    )(counts, starts, x1, dest_flat)


def _expert_kernel(blk_ref, used_ref, xs_ref, wg_ref, wl_ref, wd_ref, bg_ref, bl_ref, bd_ref, ys_ref):
    i = pl.program_id(0)

    @pl.when(i < used_ref[0])
    def _():
        xb = xs_ref[...].astype(BF16)
        glu = jnp.dot(xb, wg_ref[0], preferred_element_type=F32) + bg_ref[0]
        lin = jnp.dot(xb, wl_ref[0], preferred_element_type=F32) + bl_ref[0]
        glu = jnp.minimum(glu, SWIGLU_LIMIT)
        lin = jnp.clip(lin, -SWIGLU_LIMIT, SWIGLU_LIMIT)
        a = glu * jax.nn.sigmoid(SWIGLU_ALPHA * glu) * (lin + 1.0)
        ys_ref[...] = jnp.dot(a.astype(BF16), wd_ref[0], preferred_element_type=F32) + bd_ref[0]

    @pl.when(i >= used_ref[0])
    def _():
        ys_ref[...] = jnp.zeros_like(ys_ref)


def _experts(xs, blk_e, n_used, wg, wl, wd, bg, bl, bd):
    def row_map(i, blk, used):
        return (jnp.minimum(i, used[0] - 1), 0)

    def out_map(i, blk, used):
        return (i, 0)

    def w_map(i, blk, used):
        return (blk[jnp.minimum(i, used[0] - 1)], 0, 0)

    return pl.pallas_call(
        _expert_kernel,
        grid_spec=pltpu.PrefetchScalarGridSpec(
            num_scalar_prefetch=2,
            grid=(NB,),
            in_specs=[
                pl.BlockSpec((BM, D), row_map),
                pl.BlockSpec((1, D, D), w_map),
                pl.BlockSpec((1, D, D), w_map),
                pl.BlockSpec((1, D, D), w_map),
                pl.BlockSpec((1, 1, D), w_map),
                pl.BlockSpec((1, 1, D), w_map),
                pl.BlockSpec((1, 1, D), w_map),
            ],
            out_specs=pl.BlockSpec((BM, D), out_map),
        ),
        out_shape=jax.ShapeDtypeStruct((R_ROWS, D), F32),
        compiler_params=_cparams(("arbitrary",)),
        name="moe_experts",
    )(blk_e, n_used, xs, wg, wl, wd, bg, bl, bd)


def _combine_kernel(x_ref, gt_ref, lg_ref, lb_ref, dest_hbm, ys_hbm, o_ref, idx_ref, buf_ref,
                    sem_idx, sem_row):
    i = pl.program_id(0)
    n_idx = TM_COMB * TOP_K
    idx_copy = pltpu.make_async_copy(dest_hbm.at[pl.ds(i * n_idx, n_idx)], idx_ref, sem_idx)
    idx_copy.start()
    idx_copy.wait()

    def body(t, carry):
        for j in range(TOP_K):
            d = idx_ref[t * TOP_K + j]
            pltpu.make_async_copy(ys_hbm.at[pl.ds(d, 1)],
                                  buf_ref.at[pl.ds(j * TM_COMB + t, 1)], sem_row).start()
        return carry

    lax.fori_loop(0, TM_COMB, body, 0, unroll=8)
    pltpu.make_async_copy(ys_hbm.at[pl.ds(0, n_idx)], buf_ref, sem_row).wait()
    gt = gt_ref[...]
    m = gt[:, 0:1] * buf_ref[0:TM_COMB, :]
    for j in range(1, TOP_K):
        m = m + gt[:, j:j + 1] * buf_ref[j * TM_COMB:(j + 1) * TM_COMB, :]
    o_ref[...] = _layer_norm(ALPHA * x_ref[...] + m, lg_ref[...], lb_ref[...])


def _combine(x1, gates, ln_g, ln_b, dest_flat, ys):
    def row(cols):
        return pl.BlockSpec((TM_COMB, cols), lambda i: (i, 0))

    return pl.pallas_call(
        _combine_kernel,
        grid=(T // TM_COMB,),
        in_specs=[
            row(D), row(TOP_K), _const_spec((1, D)), _const_spec((1, D)),
            pl.BlockSpec(memory_space=pl.ANY),
            pl.BlockSpec(memory_space=pl.ANY),
        ],
        out_specs=row(D),
        out_shape=jax.ShapeDtypeStruct((T, D), F32),
        scratch_shapes=[
            pltpu.SMEM((TM_COMB * TOP_K,), I32),
            pltpu.VMEM((TM_COMB * TOP_K, D), F32),
            pltpu.SemaphoreType.DMA(()),
            pltpu.SemaphoreType.DMA(()),
        ],
        compiler_params=_cparams(("arbitrary",)),
        name="moe_combine_norm",
    )(x1, gates, ln_g, ln_b, dest_flat, ys)


def _moe_layer(x1, top_e, gates, w_up, b_up, w_down, b_down, ln_g, ln_b):
    dest, blk_e, stat = _positions(top_e)
    dest_flat = dest.reshape(T * TOP_K)
    counts = stat[0, :N_EXPERTS].astype(I32)
    starts = stat[8, :N_EXPERTS].astype(I32)
    n_used = (stat[16, N_EXPERTS - 1:N_EXPERTS].astype(I32)) // BM
    xs = _dispatch(x1, dest_flat, counts, starts)
    wg = w_up[:, :, 0::2].astype(BF16)
    wl = w_up[:, :, 1::2].astype(BF16)
    bg = b_up[:, None, 0::2]
    bl = b_up[:, None, 1::2]
    ys = _experts(xs, blk_e.reshape(NB_PAD), n_used, wg, wl, w_down.astype(BF16), bg, bl,
                  b_down[:, None, :])
    return _combine(x1, gates, ln_g, ln_b, dest_flat, ys)


def _rope_tables():
    half = DK // 2
    inv = 1.0 / (ROPE_BASE ** jnp.linspace(0.0, 1.0, half, dtype=F32))
    pos = jnp.concatenate([jnp.arange(SEQ), PAST_LEN + (jnp.arange(TM_PROJ) % DEC_SEQ)]).astype(F32)
    ang = pos[:, None] * inv[None, :]
    return jnp.cos(ang), jnp.sin(ang)


def kernel(x_prompt, x_sample, state_ret, cache_pool, ret_w_in, ret_w_out, pool_w_in, pool_w_grp,
           pool_scale, pool_w_out, ln1_g, ln1_b, ln2_g, ln2_b, router_w, router_b, w_up, b_up,
           w_down, b_down):
    x = jnp.concatenate([x_prompt.reshape(TP, D), x_sample.reshape(TS, D)], axis=0)

    def vec(a):
        return a.reshape(1, -1)

    cos_t, sin_t = _rope_tables()
    p_all = _ret_project(x, ret_w_in[0].astype(BF16), cos_t, sin_t)
    o_prompt, s_prompt = _ret_core(p_all, BATCH, SUPER, SEQ // SUPER, 0)
    o_sample, s_sample = _ret_core(p_all, DEC_BATCH, DEC_SEQ, 1, TP // DEC_SEQ,
                                   s0=state_ret[0].reshape(DEC_BATCH * HEADS, DK, DV))
    x1, top_e, gates = _ret_out(o_prompt, o_sample, p_all, x, ret_w_out[0].astype(BF16),
                                vec(ln1_g[0]), vec(ln1_b[0]), router_w[0], vec(router_b[0]))
    x = _moe_layer(x1, top_e, gates, w_up[0], b_up[0], w_down[0], b_down[0], vec(ln2_g[0]),
                   vec(ln2_b[0]))

    x1, top_e, gates, cache_p, cache_s = _pool_mixer(
        x, cache_pool[0], pool_w_in[0].astype(BF16), pool_w_grp[0].astype(BF16),
        vec(pool_scale[0]), pool_w_out[0].astype(BF16), vec(ln1_g[1]), vec(ln1_b[1]), router_w[1],
        vec(router_b[1]))
    x = _moe_layer(x1, top_e, gates, w_up[1], b_up[1], w_down[1], b_down[1], vec(ln2_g[1]),
                   vec(ln2_b[1]))

    y_prompt = x[:TP].reshape(BATCH, SEQ, D)
    y_sample = x[TP:].reshape(DEC_BATCH, DEC_SEQ, D)
    state_ret_prompt = s_prompt.reshape(1, BATCH, HEADS, DK, DV)
    state_ret_sample = s_sample.reshape(1, DEC_BATCH, HEADS, DK, DV)
    cache_pool_prompt = cache_p[None, :, 1:, :]
    cache_pool_sample = cache_s[None, :, 1:, :]
    return (y_prompt, y_sample, state_ret_prompt, state_ret_sample, cache_pool_prompt,
            cache_pool_sample)
```

```python
import functools

import jax
import jax.numpy as jnp
from jax import lax
from jax.experimental import pallas as pl
from jax.experimental.pallas import tpu as pltpu

F32 = jnp.float32
BF16 = jnp.bfloat16
I32 = jnp.int32

D = 1024
BATCH = 16
SEQ = 2048
DEC_BATCH = 32
DEC_SEQ = 64
PAST_LEN = 4096
TP = BATCH * SEQ
TS = DEC_BATCH * DEC_SEQ
T = TP + TS

HEADS = 4
DK = 256
DV = 512
RET_QK = HEADS * DK
RET_V = HEADS * DV
RET_IN = 2 * RET_QK + 2 * RET_V
ROPE_BASE = 10000.0
RMS_EPS = 1e-6
LN_EPS = 1e-5
ALPHA = 4.0 ** 0.25

POOL_WINDOWS = (2, 4, 8, 16)
POOL_GC = D // 4
POOL_STATE = 15
CARRY = 16

N_EXPERTS = 32
TOP_K = 4
SWIGLU_LIMIT = 7.0
SWIGLU_ALPHA = 1.702

LANES = 128
SUBLANES = 8
SUPER = 256
CHUNK = 64

TM_PROJ = 1024
TM = 512
TM_DISP = 256
TM_COMB = 256
BM = 512
NB = -(-(T * TOP_K) // BM) + N_EXPERTS
NB_PAD = -(-NB // 8) * 8
R_ROWS = NB * BM

VMEM_LIMIT = 56 * 1024 * 1024


def _cparams(sem, vmem=VMEM_LIMIT):
    return pltpu.CompilerParams(dimension_semantics=sem, vmem_limit_bytes=vmem)


def _proj_kernel(x_ref, w_ref, cos_ref, sin_ref, o_ref):
    n = pl.program_id(1)
    xb = x_ref[...].astype(BF16)

    @pl.when(n < 2)
    def _():
        scale = jnp.where(n == 1, DK ** -0.5, 1.0).astype(F32)
        cos = cos_ref[...] * scale
        sin = sin_ref[...] * scale
        half = DK // 2
        for h in range(HEADS):
            p = jnp.dot(xb, w_ref[:, h * DK:(h + 1) * DK], preferred_element_type=F32)
            t1 = p[:, :half]
            t2 = p[:, half:]
            o_ref[:, h * DK:h * DK + half] = (t1 * cos - t2 * sin).astype(BF16)
            o_ref[:, h * DK + half:(h + 1) * DK] = (t1 * sin + t2 * cos).astype(BF16)

    @pl.when(n >= 2)
    def _():
        for h in range(4):
            p = jnp.dot(xb, w_ref[:, h * 256:(h + 1) * 256], preferred_element_type=F32)
            o_ref[:, h * 256:(h + 1) * 256] = p.astype(BF16)


def _ret_project(x, w_bf16, cos_t, sin_t):
    n_prompt_tiles = TP // TM_PROJ
    tiles_per_seq = SEQ // TM_PROJ

    def tab_map(i, n):
        return (jnp.where(i < n_prompt_tiles, i % tiles_per_seq, tiles_per_seq), 0)

    return pl.pallas_call(
        _proj_kernel,
        grid=(T // TM_PROJ, RET_IN // 1024),
        in_specs=[
            pl.BlockSpec((TM_PROJ, D), lambda i, n: (i, 0)),
            pl.BlockSpec((D, 1024), lambda i, n: (0, n)),
            pl.BlockSpec((TM_PROJ, LANES), tab_map),
            pl.BlockSpec((TM_PROJ, LANES), tab_map),
        ],
        out_specs=pl.BlockSpec((TM_PROJ, 1024), lambda i, n: (i, n)),
        out_shape=jax.ShapeDtypeStruct((T, RET_IN), BF16),
        compiler_params=_cparams(("arbitrary", "arbitrary")),
        name="ret_project",
    )(x, w_bf16, cos_t, sin_t)


def _ret_core_kernel(has_init, n_steps, *refs):
    if has_init:
        (q_ref, k_ref, v_ref, mask_ref, qd_ref, kd_ref, bd_ref, s0_ref,
         o_ref, sout_ref, s_ref) = refs
    else:
        (q_ref, k_ref, v_ref, mask_ref, qd_ref, kd_ref, bd_ref,
         o_ref, sout_ref, s_ref) = refs
    c = pl.program_id(2)

    @pl.when(c == 0)
    def _():
        if has_init:
            s_ref[...] = s0_ref[0].astype(F32)
        else:
            s_ref[...] = jnp.zeros_like(s_ref)

    q = q_ref[...]
    k = k_ref[...]
    v = v_ref[...]
    s_prev = s_ref[...]
    scores = lax.dot_general(q, k, (((1,), (1,)), ((), ())), preferred_element_type=F32)
    scores = scores * mask_ref[0]
    qd = (q.astype(F32) * qd_ref[0]).astype(BF16)
    o = (jnp.dot(scores.astype(BF16), v, preferred_element_type=F32)
         + jnp.dot(qd, s_prev.astype(BF16), preferred_element_type=F32))
    kd = (k.astype(F32) * kd_ref[0]).astype(BF16)
    s_new = s_prev * bd_ref[0] + lax.dot_general(kd, v, (((0,), (0,)), ((), ())),
                                                 preferred_element_type=F32)
    s_ref[...] = s_new
    o = o * lax.rsqrt(jnp.mean(o * o, axis=-1, keepdims=True) + RMS_EPS)
    o_ref[...] = o.astype(BF16)

    @pl.when(c == n_steps - 1)
    def _():
        sout_ref[0] = s_new


def _decay_tables(rows):
    lg = jnp.log1p(-jnp.exp2(-5.0 - jnp.arange(HEADS, dtype=F32)))
    idx = jnp.arange(rows, dtype=F32)
    ch = jnp.arange(rows) // CHUNK
    diff = idx[:, None] - idx[None, :]
    same = ch[:, None] == ch[None, :]
    earlier = ch[None, :] < ch[:, None]
    expo = jnp.where(same, jnp.abs(diff), diff)
    w = jnp.exp(lg[:, None, None] * expo[None])
    mask = jnp.where((same | earlier)[None], w, 0.0).astype(F32)
    q_dec = jnp.exp(lg[:, None] * (idx + 1.0))[:, :, None]
    k_dec = jnp.exp(lg[:, None] * (rows - 1.0 - idx))[:, :, None]
    blk = jnp.broadcast_to(jnp.exp(lg * rows)[:, None, None], (HEADS, 1, DV))
    return mask, q_dec, k_dec, blk.astype(F32)


def _ret_core(p_all, n_seq, rows, n_steps, row_block0, s0=None):
    mask, q_dec, k_dec, blk = _decay_tables(rows)
    has_init = s0 is not None

    def rb(b, c):
        return row_block0 + b * n_steps + c

    in_specs = [
        pl.BlockSpec((rows, DK), lambda b, h, c: (rb(b, c), h)),
        pl.BlockSpec((rows, DK), lambda b, h, c: (rb(b, c), HEADS + h)),
        pl.BlockSpec((rows, DV), lambda b, h, c: (rb(b, c), HEADS + h)),
        pl.BlockSpec((1, rows, rows), lambda b, h, c: (h, 0, 0)),
        pl.BlockSpec((1, rows, 1), lambda b, h, c: (h, 0, 0)),
        pl.BlockSpec((1, rows, 1), lambda b, h, c: (h, 0, 0)),
        pl.BlockSpec((1, 1, DV), lambda b, h, c: (h, 0, 0)),
    ]
    args = [p_all, p_all, p_all, mask, q_dec, k_dec, blk]
    if has_init:
        in_specs.append(pl.BlockSpec((1, DK, DV), lambda b, h, c: (b * HEADS + h, 0, 0)))
        args.append(s0)
    return pl.pallas_call(
        functools.partial(_ret_core_kernel, has_init, n_steps),
        grid=(n_seq, HEADS, n_steps),
        in_specs=in_specs,
        out_specs=[
            pl.BlockSpec((rows, DV), lambda b, h, c: (b * n_steps + c, h)),
            pl.BlockSpec((1, DK, DV), lambda b, h, c: (b * HEADS + h, 0, 0)),
        ],
        out_shape=[
            jax.ShapeDtypeStruct((n_seq * n_steps * rows, RET_V), BF16),
            jax.ShapeDtypeStruct((n_seq * HEADS, DK, DV), F32),
        ],
        scratch_shapes=[pltpu.VMEM((DK, DV), F32)],
        compiler_params=_cparams(("arbitrary", "arbitrary", "arbitrary")),
        name="ret_core_sample" if has_init else "ret_core_prompt",
    )(*args)


def _layer_norm(y, g, b):
    mu = jnp.mean(y, axis=-1, keepdims=True)
    yc = y - mu
    var = jnp.mean(yc * yc, axis=-1, keepdims=True)
    return yc * lax.rsqrt(var + LN_EPS) * g + b


def _split_bf16(a):
    hi = a.astype(BF16)
    lo = (a - hi.astype(F32)).astype(BF16)
    return hi, lo


def _route(x1, rw_ref, rb_ref, te_ref, gt_ref):
    rows = x1.shape[0]
    xh, xl = _split_bf16(x1)
    wh, wl = _split_bf16(rw_ref[...])
    logits = (jnp.dot(xh, wh, preferred_element_type=F32)
              + jnp.dot(xl, wh, preferred_element_type=F32)
              + jnp.dot(xh, wl, preferred_element_type=F32)) + rb_ref[...]
    lane = lax.broadcasted_iota(I32, (rows, N_EXPERTS), 1)
    lane_k = lax.broadcasted_iota(I32, (rows, TOP_K), 1)
    te = jnp.zeros((rows, TOP_K), I32)
    tv = jnp.zeros((rows, TOP_K), F32)
    cur = logits
    for j in range(TOP_K):
        m = jnp.max(cur, axis=-1, keepdims=True)
        idx = jnp.min(jnp.where(cur == m, lane, N_EXPERTS), axis=-1, keepdims=True)
        te = jnp.where(lane_k == j, idx, te)
        tv = jnp.where(lane_k == j, m, tv)
        cur = jnp.where(lane == idx, -jnp.inf, cur)
    ex = jnp.exp(tv - jnp.max(tv, axis=-1, keepdims=True))
    gt_ref[...] = ex / jnp.sum(ex, axis=-1, keepdims=True)
    te_ref[...] = te


def _residual_norm_route(x, h, lg_ref, lb_ref, rw_ref, rb_ref, x1_ref, te_ref, gt_ref):
    x1 = _layer_norm(ALPHA * x + h, lg_ref[...], lb_ref[...])
    x1_ref[...] = x1
    _route(x1, rw_ref, rb_ref, te_ref, gt_ref)


def _ret_out_kernel(op_ref, os_ref, g_ref, x_ref, w_ref, lg_ref, lb_ref, rw_ref, rb_ref,
                    x1_ref, te_ref, gt_ref):
    g = g_ref[...].astype(F32)
    o = jnp.where(pl.program_id(0) < TP // TM, op_ref[...], os_ref[...])
    a = (g * jax.nn.sigmoid(g) * o.astype(F32)).astype(BF16)
    h = jnp.dot(a, w_ref[...], preferred_element_type=F32)
    _residual_norm_route(x_ref[...], h, lg_ref, lb_ref, rw_ref, rb_ref, x1_ref, te_ref, gt_ref)


def _row_spec(cols, col_block=0):
    return pl.BlockSpec((TM, cols), lambda i: (i, col_block))


def _const_spec(shape):
    nd = len(shape)
    return pl.BlockSpec(shape, lambda i: (0,) * nd)


_EPILOGUE_OUT_SPECS = [_row_spec(D), _row_spec(TOP_K), _row_spec(TOP_K)]
_EPILOGUE_OUT_SHAPE = [
    jax.ShapeDtypeStruct((T, D), F32),
    jax.ShapeDtypeStruct((T, TOP_K), I32),
    jax.ShapeDtypeStruct((T, TOP_K), F32),
]


def _ret_out(o_prompt, o_sample, p_all, x, w_out_bf16, ln_g, ln_b, router_w, router_b):
    n_prompt_tiles = TP // TM
    return pl.pallas_call(
        _ret_out_kernel,
        grid=(T // TM,),
        in_specs=[
            pl.BlockSpec((TM, RET_V), lambda i: (jnp.minimum(i, n_prompt_tiles - 1), 0)),
            pl.BlockSpec((TM, RET_V), lambda i: (jnp.maximum(i - n_prompt_tiles, 0), 0)),
            _row_spec(RET_V, col_block=2),
            _row_spec(D),
            _const_spec((RET_V, D)),
            _const_spec((1, D)), _const_spec((1, D)),
            _const_spec((D, N_EXPERTS)), _const_spec((1, N_EXPERTS)),
        ],
        out_specs=_EPILOGUE_OUT_SPECS,
        out_shape=_EPILOGUE_OUT_SHAPE,
        compiler_params=_cparams(("arbitrary",)),
        name="ret_out_norm_route",
    )(o_prompt, o_sample, p_all, x, w_out_bf16, ln_g, ln_b, router_w, router_b)


SEQS_PER_TILE = TM // DEC_SEQ


def _group_cols(gi):
    return slice(gi * POOL_GC, (gi + 1) * POOL_GC)


def _window_pool(src_ref, rows, pos, dst_ref, row0):
    for gi, w in enumerate(POOL_WINDOWS):
        cols = _group_cols(gi)
        cur = src_ref[CARRY:CARRY + rows, cols]
        acc = cur
        for back in range(1, w):
            acc = acc + src_ref[CARRY - back:CARRY - back + rows, cols]
        cnt = jnp.minimum(pos + 1.0, float(w))
        dst_ref[row0:row0 + rows, cols] = acc / cnt - cur


def _pool_kernel(x_ref, pre_ref, win_ref, wg_ref, sc_ref, wout_ref, lg_ref, lb_ref, rw_ref, rb_ref,
                 x1_ref, te_ref, gt_ref, cachep_ref, caches_ref, full_ref, seq_ref, pooled_ref):
    i = pl.program_id(0)
    n_prompt_tiles = TP // TM
    tiles_per_seq = SEQ // TM
    x = x_ref[...]
    u = jnp.dot(x.astype(BF16), win_ref[...], preferred_element_type=F32)

    @pl.when(i < n_prompt_tiles)
    def _():
        j = i % tiles_per_seq

        @pl.when(j == 0)
        def _():
            full_ref[0:CARRY, :] = jnp.zeros((CARRY, D), F32)

        full_ref[CARRY:CARRY + TM, :] = u
        pos = (j * TM + lax.broadcasted_iota(I32, (TM, 1), 0)).astype(F32)
        _window_pool(full_ref, TM, pos, pooled_ref, 0)

        @pl.when(j == tiles_per_seq - 1)
        def _():
            cachep_ref[0] = full_ref[TM:CARRY + TM, :]

        full_ref[0:CARRY, :] = full_ref[TM:CARRY + TM, :]

    @pl.when(i >= n_prompt_tiles)
    def _():
        pos = (PAST_LEN + lax.broadcasted_iota(I32, (DEC_SEQ, 1), 0)).astype(F32)
        seq_ref[0:1, :] = jnp.zeros((1, D), F32)
        for r in range(SEQS_PER_TILE):
            seq_ref[1:CARRY, :] = pre_ref[r]
            seq_ref[CARRY:CARRY + DEC_SEQ, :] = u[r * DEC_SEQ:(r + 1) * DEC_SEQ, :]
            _window_pool(seq_ref, DEC_SEQ, pos, pooled_ref, r * DEC_SEQ)
            caches_ref[r] = seq_ref[DEC_SEQ:CARRY + DEC_SEQ, :]

    mixed = [jnp.dot(pooled_ref[:, _group_cols(gi)].astype(BF16), wg_ref[gi],
                     preferred_element_type=F32) for gi in range(len(POOL_WINDOWS))]
    mixed = jnp.concatenate(mixed, axis=-1) * sc_ref[...]
    h = jnp.dot(mixed.astype(BF16), wout_ref[...], preferred_element_type=F32)
    _residual_norm_route(x, h, lg_ref, lb_ref, rw_ref, rb_ref, x1_ref, te_ref, gt_ref)


def _pool_mixer(x, prefix, win, wgrp, scale, wout, ln_g, ln_b, router_w, router_b):
    n_prompt_tiles = TP // TM
    tiles_per_seq = SEQ // TM
    return pl.pallas_call(
        _pool_kernel,
        grid=(T // TM,),
        in_specs=[
            _row_spec(D),
            pl.BlockSpec((SEQS_PER_TILE, POOL_STATE, D),
                         lambda i: (jnp.maximum(i - n_prompt_tiles, 0), 0, 0)),
            _const_spec((D, D)), _const_spec((len(POOL_WINDOWS), POOL_GC, POOL_GC)),
            _const_spec((1, D)), _const_spec((D, D)),
            _const_spec((1, D)), _const_spec((1, D)),
            _const_spec((D, N_EXPERTS)), _const_spec((1, N_EXPERTS)),
        ],
        out_specs=_EPILOGUE_OUT_SPECS + [
            pl.BlockSpec((1, CARRY, D),
                         lambda i: (jnp.minimum(i, n_prompt_tiles - 1) // tiles_per_seq, 0, 0)),
            pl.BlockSpec((SEQS_PER_TILE, CARRY, D),
                         lambda i: (jnp.maximum(i - n_prompt_tiles, 0), 0, 0)),
        ],
        out_shape=_EPILOGUE_OUT_SHAPE + [
            jax.ShapeDtypeStruct((BATCH, CARRY, D), F32),
            jax.ShapeDtypeStruct((DEC_BATCH, CARRY, D), F32),
        ],
        scratch_shapes=[
            pltpu.VMEM((CARRY + TM, D), F32),
            pltpu.VMEM((CARRY + DEC_SEQ, D), F32),
            pltpu.VMEM((TM, D), F32),
        ],
        compiler_params=_cparams(("arbitrary",)),
        name="pool_norm_route",
    )(x, prefix, win, wgrp, scale, wout, ln_g, ln_b, router_w, router_b)


def _lane_cumsum(v):
    lane = lax.broadcasted_iota(I32, v.shape, 1)
    shift = 1
    while shift < LANES:
        v = v + jnp.where(lane >= shift, pltpu.roll(v, shift, axis=1), 0.0)
        shift *= 2
    return v


def _positions_kernel(te_ref, dest_ref, blk_ref, stat_ref, cnt_ref, base_ref):
    phase = pl.program_id(0)
    i = pl.program_id(1)
    te = te_ref[...]
    lane = lax.broadcasted_iota(I32, (TM, LANES), 1)
    hits = [lane == te[:, j:j + 1] for j in range(TOP_K)]
    onehot = sum(h.astype(F32) for h in hits)
    tile_cnt = jnp.sum(onehot, axis=0, keepdims=True)

    @pl.when((phase == 0) & (i == 0))
    def _():
        cnt_ref[...] = jnp.zeros_like(cnt_ref)

    @pl.when(phase == 0)
    def _():
        cnt_ref[...] = cnt_ref[...] + tile_cnt

    @pl.when((phase == 1) & (i == 0))
    def _():
        cnt = cnt_ref[...].astype(I32)
        padded = ((cnt + (BM - 1)) & ~(BM - 1)).astype(F32)
        pad_end = _lane_cumsum(padded)
        base_ref[...] = pad_end - padded
        stat_ref[...] = jnp.concatenate([cnt_ref[...], pad_end - padded, pad_end], axis=0)
        lane8 = lax.broadcasted_iota(I32, (NB_PAD, LANES), 1)
        start = (lax.broadcasted_iota(I32, (NB_PAD, LANES), 0) * BM).astype(F32)
        done = jnp.where((lane8 < N_EXPERTS) & (pad_end[0:1, :] <= start), 1.0, 0.0)
        blk = jnp.minimum(jnp.sum(done, axis=-1, keepdims=True), N_EXPERTS - 1.0)
        blk_ref[...] = blk.astype(I32)

    @pl.when(phase == 1)
    def _():
        r = lax.broadcasted_iota(I32, (TM, TM), 0)
        c = lax.broadcasted_iota(I32, (TM, TM), 1)
        tri = jnp.where(r > c, 1.0, 0.0).astype(BF16)
        before = jnp.dot(tri, onehot.astype(BF16), preferred_element_type=F32)
        slot = base_ref[0:1, :] + before
        lane_k = lax.broadcasted_iota(I32, (TM, TOP_K), 1)
        dest = jnp.zeros((TM, TOP_K), F32)
        for j in range(TOP_K):
            dj = jnp.sum(jnp.where(hits[j], slot, 0.0), axis=-1, keepdims=True)
            dest = jnp.where(lane_k == j, dj, dest)
        dest_ref[...] = dest.astype(I32)
        base_ref[...] = base_ref[...] + tile_cnt


def _positions(top_e):
    return pl.pallas_call(
        _positions_kernel,
        grid=(2, T // TM),
        in_specs=[pl.BlockSpec((TM, TOP_K), lambda p, i: (i, 0))],
        out_specs=[
            pl.BlockSpec((TM, TOP_K), lambda p, i: (i * p, 0)),
            pl.BlockSpec((NB_PAD, 1), lambda p, i: (0, 0)),
            pl.BlockSpec((24, LANES), lambda p, i: (0, 0)),
        ],
        out_shape=[
            jax.ShapeDtypeStruct((T, TOP_K), I32),
            jax.ShapeDtypeStruct((NB_PAD, 1), I32),
            jax.ShapeDtypeStruct((24, LANES), F32),
        ],
        scratch_shapes=[pltpu.VMEM((8, LANES), F32), pltpu.VMEM((8, LANES), F32)],
        compiler_params=_cparams(("arbitrary", "arbitrary")),
        name="moe_positions",
    )(top_e)


ZROWS = 256


def _dispatch_kernel(cnt_ref, start_ref, x_ref, dest_hbm, xs_hbm, idx_ref, zero_ref, sem_idx, sem_row,
                     sem_zero):
    i = pl.program_id(0)
    n_idx = TM_DISP * TOP_K
    idx_copy = pltpu.make_async_copy(dest_hbm.at[pl.ds(i * n_idx, n_idx)], idx_ref, sem_idx)
    idx_copy.start()

    @pl.when(i == 0)
    def _():
        zero_ref[...] = jnp.zeros_like(zero_ref)

        def zero_copy(row, size):
            return pltpu.make_async_copy(zero_ref.at[pl.ds(0, size)], xs_hbm.at[pl.ds(row, size)],
                                         sem_zero)

        def fill_expert(e, carry):
            row0 = start_ref[e] + cnt_ref[e]
            n_pad = (-cnt_ref[e]) & (BM - 1)
            n_head = (-row0) & (SUBLANES - 1)
            for r in range(SUBLANES - 1):
                @pl.when(r < n_head)
                def _(r=r):
                    cp = zero_copy(row0 + r, 1)
                    cp.start()
                    cp.wait()

            row = row0 + n_head
            n_rest = n_pad - n_head
            size = ZROWS
            while size >= SUBLANES:
                take = (n_rest & size) != 0

                @pl.when(take)
                def _(row=row, size=size):
                    cp = zero_copy(pl.multiple_of(row, SUBLANES), size)
                    cp.start()
                    cp.wait()

                row = row + jnp.where(take, size, 0)
                size //= 2
            return carry

        lax.fori_loop(0, N_EXPERTS, fill_expert, 0)

        def fill_unused(c, carry):
            cp = zero_copy(pl.multiple_of(c * ZROWS, ZROWS), ZROWS)
            cp.start()
            cp.wait()
            return carry

        last = N_EXPERTS - 1
        used_rows = start_ref[last] + ((cnt_ref[last] + (BM - 1)) & ~(BM - 1))
        lax.fori_loop(used_rows // ZROWS, R_ROWS // ZROWS, fill_unused, 0)

    idx_copy.wait()

    for t in range(TM_DISP):
        for j in range(TOP_K):
            d = idx_ref[t * TOP_K + j]
            pltpu.make_async_copy(x_ref.at[pl.ds(t, 1)], xs_hbm.at[pl.ds(d, 1)],
                                  sem_row).start(priority=j % 2)
    for j in range(TOP_K):
        pltpu.make_async_copy(x_ref, xs_hbm.at[pl.ds(0, TM_DISP)], sem_row).wait()


def _dispatch(x1, dest_flat, counts, starts):
    return pl.pallas_call(
        _dispatch_kernel,
        grid_spec=pltpu.PrefetchScalarGridSpec(
            num_scalar_prefetch=2,
            grid=(T // TM_DISP,),
            in_specs=[
                pl.BlockSpec((TM_DISP, D), lambda i, c, s: (i, 0)),
                pl.BlockSpec(memory_space=pl.ANY),
            ],
            out_specs=pl.BlockSpec(memory_space=pl.ANY),
            scratch_shapes=[
                pltpu.SMEM((TM_DISP * TOP_K,), I32),
                pltpu.VMEM((ZROWS, D), F32),
                pltpu.SemaphoreType.DMA(()),
                pltpu.SemaphoreType.DMA(()),
                pltpu.SemaphoreType.DMA(()),
            ],
        ),
        out_shape=jax.ShapeDtypeStruct((R_ROWS, D), F32),
        compiler_params=_cparams(("arbitrary",)),
        name="moe_dispatch",
    )(counts, starts, x1, dest_flat)


W_CHUNK = 64


def _split_even_odd(x):
    rows, cols = x.shape
    lane = lax.broadcasted_iota(I32, (rows, LANES), 1)
    idx_even = (2 * lane) % LANES
    idx_odd = (2 * lane + 1) % LANES
    low = lane < LANES // 2
    even, odd = [], []
    for c in range(cols // (2 * LANES)):
        a = x[:, (2 * c) * LANES:(2 * c + 1) * LANES]
        b = x[:, (2 * c + 1) * LANES:(2 * c + 2) * LANES]
        even.append(jnp.where(low, jnp.take_along_axis(a, idx_even, axis=1),
                              jnp.take_along_axis(b, idx_even, axis=1)))
        odd.append(jnp.where(low, jnp.take_along_axis(a, idx_odd, axis=1),
                             jnp.take_along_axis(b, idx_odd, axis=1)))
    return jnp.concatenate(even, axis=1), jnp.concatenate(odd, axis=1)


def _expert_kernel(blk_ref, used_ref, xs_ref, wu_ref, wd_ref, bu_ref, bd_ref, ys_ref,
                   wg_s, wl_s, wd_s, bg_s, bl_s):
    i = pl.program_id(0)
    active = i < used_ref[0]
    new_expert = (i == 0) | (blk_ref[i] != blk_ref[jnp.maximum(i - 1, 0)])

    @pl.when(active & new_expert)
    def _():
        def convert(r, carry):
            rows = pl.ds(pl.multiple_of(r * W_CHUNK, W_CHUNK), W_CHUNK)
            even, odd = _split_even_odd(wu_ref[0, rows, :])
            wg_s[rows, :] = even.astype(BF16)
            wl_s[rows, :] = odd.astype(BF16)
            wd_s[rows, :] = wd_ref[0, rows, :].astype(BF16)
            return carry

        lax.fori_loop(0, D // W_CHUNK, convert, 0)
        even, odd = _split_even_odd(jnp.broadcast_to(bu_ref[0], (SUBLANES, 2 * D)))
        bg_s[...] = even
        bl_s[...] = odd

    @pl.when(active)
    def _():
        xb = xs_ref[...].astype(BF16)
        glu = jnp.dot(xb, wg_s[...], preferred_element_type=F32) + bg_s[0:1, :]
        lin = jnp.dot(xb, wl_s[...], preferred_element_type=F32) + bl_s[0:1, :]
        glu = jnp.minimum(glu, SWIGLU_LIMIT)
        lin = jnp.clip(lin, -SWIGLU_LIMIT, SWIGLU_LIMIT)
        a = glu * jax.nn.sigmoid(SWIGLU_ALPHA * glu) * (lin + 1.0)
        ys_ref[...] = jnp.dot(a.astype(BF16), wd_s[...], preferred_element_type=F32) + bd_ref[0]

    @pl.when(jnp.logical_not(active))
    def _():
        ys_ref[...] = jnp.zeros_like(ys_ref)


def _experts(xs, blk_e, n_used, w_up, w_down, b_up, b_down):
    def row_map(i, blk, used):
        return (jnp.minimum(i, used[0] - 1), 0)

    def out_map(i, blk, used):
        return (i, 0)

    def w_map(i, blk, used):
        return (blk[jnp.minimum(i, used[0] - 1)], 0, 0)

    return pl.pallas_call(
        _expert_kernel,
        grid_spec=pltpu.PrefetchScalarGridSpec(
            num_scalar_prefetch=2,
            grid=(NB,),
            in_specs=[
                pl.BlockSpec((BM, D), row_map),
                pl.BlockSpec((1, D, 2 * D), w_map),
                pl.BlockSpec((1, D, D), w_map),
                pl.BlockSpec((1, 1, 2 * D), w_map),
                pl.BlockSpec((1, 1, D), w_map),
            ],
            out_specs=pl.BlockSpec((BM, D), out_map),
            scratch_shapes=[
                pltpu.VMEM((D, D), BF16), pltpu.VMEM((D, D), BF16), pltpu.VMEM((D, D), BF16),
                pltpu.VMEM((SUBLANES, D), F32), pltpu.VMEM((SUBLANES, D), F32),
            ],
        ),
        out_shape=jax.ShapeDtypeStruct((R_ROWS, D), F32),
        compiler_params=_cparams(("arbitrary",)),
        name="moe_experts",
    )(blk_e, n_used, xs, w_up, w_down, b_up, b_down)


N_COMB = T // TM_COMB


def _combine_kernel(x_ref, gt_ref, lg_ref, lb_ref, dest_hbm, ys_hbm, o_ref, idx_ref, buf_ref,
                    sem_idx, sem_row):
    s = pl.program_id(0)
    n_idx = TM_COMB * TOP_K

    def start_gathers(slot):
        base = slot * n_idx
        idx_copy = pltpu.make_async_copy(dest_hbm.at[pl.ds(pl.multiple_of(s * n_idx, n_idx), n_idx)],
                                         idx_ref.at[pl.ds(base, n_idx)], sem_idx)
        idx_copy.start()
        idx_copy.wait()
        for t in range(TM_COMB):
            for j in range(TOP_K):
                d = idx_ref[base + t * TOP_K + j]
                pltpu.make_async_copy(ys_hbm.at[pl.ds(d, 1)],
                                      buf_ref.at[pl.ds(base + j * TM_COMB + t, 1)],
                                      sem_row.at[slot]).start(priority=j % 2)

    def finish(slot):
        base = slot * n_idx
        pltpu.make_async_copy(ys_hbm.at[pl.ds(0, n_idx)], buf_ref.at[pl.ds(base, n_idx)],
                              sem_row.at[slot]).wait()
        gt = gt_ref[...]
        m = gt[:, 0:1] * buf_ref[base:base + TM_COMB, :]
        for j in range(1, TOP_K):
            m = m + gt[:, j:j + 1] * buf_ref[base + j * TM_COMB:base + (j + 1) * TM_COMB, :]
        o_ref[...] = _layer_norm(ALPHA * x_ref[...] + m, lg_ref[...], lb_ref[...])

    for slot in range(2):
        @pl.when((s < N_COMB) & (s % 2 == slot))
        def _(slot=slot):
            start_gathers(slot)

    for slot in range(2):
        @pl.when((s > 0) & ((s - 1) % 2 == slot))
        def _(slot=slot):
            finish(slot)


def _combine(x1, gates, ln_g, ln_b, dest_flat, ys):
    def row(cols):
        return pl.BlockSpec((TM_COMB, cols), lambda s: (jnp.maximum(s - 1, 0), 0))

    return pl.pallas_call(
        _combine_kernel,
        grid=(N_COMB + 1,),
        in_specs=[
            row(D), row(TOP_K), _const_spec((1, D)), _const_spec((1, D)),
            pl.BlockSpec(memory_space=pl.ANY),
            pl.BlockSpec(memory_space=pl.ANY),
        ],
        out_specs=row(D),
        out_shape=jax.ShapeDtypeStruct((T, D), F32),
        scratch_shapes=[
            pltpu.SMEM((2 * TM_COMB * TOP_K,), I32),
            pltpu.VMEM((2 * TM_COMB * TOP_K, D), F32),
            pltpu.SemaphoreType.DMA(()),
            pltpu.SemaphoreType.DMA((2,)),
        ],
        compiler_params=_cparams(("arbitrary",)),
        name="moe_combine_norm",
    )(x1, gates, ln_g, ln_b, dest_flat, ys)


def _moe_layer(x1, top_e, gates, w_up, b_up, w_down, b_down, ln_g, ln_b):
    dest, blk_e, stat = _positions(top_e)
    dest_flat = dest.reshape(T * TOP_K)
    counts = stat[0, :N_EXPERTS].astype(I32)
    starts = stat[8, :N_EXPERTS].astype(I32)
    n_used = (stat[16, N_EXPERTS - 1:N_EXPERTS].astype(I32)) // BM
    xs = _dispatch(x1, dest_flat, counts, starts)
    ys = _experts(xs, blk_e.reshape(NB_PAD), n_used, w_up, w_down, b_up[:, None, :],
                  b_down[:, None, :])
    return _combine(x1, gates, ln_g, ln_b, dest_flat, ys)


def _rope_tables():
    half = DK // 2
    inv = 1.0 / (ROPE_BASE ** jnp.linspace(0.0, 1.0, half, dtype=F32))
    pos = jnp.concatenate([jnp.arange(SEQ), PAST_LEN + (jnp.arange(TM_PROJ) % DEC_SEQ)]).astype(F32)
    ang = pos[:, None] * inv[None, :]
    return jnp.cos(ang), jnp.sin(ang)


def kernel(x_prompt, x_sample, state_ret, cache_pool, ret_w_in, ret_w_out, pool_w_in, pool_w_grp,
           pool_scale, pool_w_out, ln1_g, ln1_b, ln2_g, ln2_b, router_w, router_b, w_up, b_up,
           w_down, b_down):
    x = jnp.concatenate([x_prompt.reshape(TP, D), x_sample.reshape(TS, D)], axis=0)

    def vec(a):
        return a.reshape(1, -1)

    cos_t, sin_t = _rope_tables()
    p_all = _ret_project(x, ret_w_in[0].astype(BF16), cos_t, sin_t)
    o_prompt, s_prompt = _ret_core(p_all, BATCH, SUPER, SEQ // SUPER, 0)
    o_sample, s_sample = _ret_core(p_all, DEC_BATCH, DEC_SEQ, 1, TP // DEC_SEQ,
                                   s0=state_ret[0].reshape(DEC_BATCH * HEADS, DK, DV))
    x1, top_e, gates = _ret_out(o_prompt, o_sample, p_all, x, ret_w_out[0].astype(BF16),
                                vec(ln1_g[0]), vec(ln1_b[0]), router_w[0], vec(router_b[0]))
    x = _moe_layer(x1, top_e, gates, w_up[0], b_up[0], w_down[0], b_down[0], vec(ln2_g[0]),
                   vec(ln2_b[0]))

    x1, top_e, gates, cache_p, cache_s = _pool_mixer(
        x, cache_pool[0], pool_w_in[0].astype(BF16), pool_w_grp[0].astype(BF16),
        vec(pool_scale[0]), pool_w_out[0].astype(BF16), vec(ln1_g[1]), vec(ln1_b[1]), router_w[1],
        vec(router_b[1]))
    x = _moe_layer(x1, top_e, gates, w_up[1], b_up[1], w_down[1], b_down[1], vec(ln2_g[1]),
                   vec(ln2_b[1]))

    y_prompt = x[:TP].reshape(BATCH, SEQ, D)
    y_sample = x[TP:].reshape(DEC_BATCH, DEC_SEQ, D)
    state_ret_prompt = s_prompt.reshape(1, BATCH, HEADS, DK, DV)
    state_ret_sample = s_sample.reshape(1, DEC_BATCH, HEADS, DK, DV)
    cache_pool_prompt = cache_p[None, :, 1:, :]
    cache_pool_sample = cache_s[None, :, 1:, :]
    return (y_prompt, y_sample, state_ret_prompt, state_ret_sample, cache_pool_prompt,
            cache_pool_sample)
```

```python
import functools

import jax
import jax.numpy as jnp
from jax import lax
from jax.experimental import pallas as pl
from jax.experimental.pallas import tpu as pltpu

F32 = jnp.float32
BF16 = jnp.bfloat16
I32 = jnp.int32

D = 1024
BATCH = 16
SEQ = 2048
DEC_BATCH = 32
DEC_SEQ = 64
PAST_LEN = 4096
TP = BATCH * SEQ
TS = DEC_BATCH * DEC_SEQ
T = TP + TS

HEADS = 4
DK = 256
DV = 512
RET_QK = HEADS * DK
RET_V = HEADS * DV
RET_IN = 2 * RET_QK + 2 * RET_V
ROPE_BASE = 10000.0
RMS_EPS = 1e-6
LN_EPS = 1e-5
ALPHA = 4.0 ** 0.25

POOL_WINDOWS = (2, 4, 8, 16)
POOL_GC = D // 4
POOL_STATE = 15
CARRY = 16

N_EXPERTS = 32
TOP_K = 4
SWIGLU_LIMIT = 7.0
SWIGLU_ALPHA = 1.702

LANES = 128
SUBLANES = 8
assert D == SUBLANES * LANES
SUPER = 256
CHUNK = 64

TM_PROJ = 1024
TM = 512
TM_DISP = 256
TM_COMB = 256
BM = 512
NB = -(-(T * TOP_K) // BM) + N_EXPERTS
NB_PAD = -(-NB // 8) * 8
R_ROWS = NB * BM

VMEM_LIMIT = 56 * 1024 * 1024


def _cparams(sem, vmem=VMEM_LIMIT):
    return pltpu.CompilerParams(dimension_semantics=sem, vmem_limit_bytes=vmem)


def _proj_kernel(xp_ref, xs_ref, w_ref, cos_ref, sin_ref, o_ref):
    n = pl.program_id(1)
    x = jnp.where(pl.program_id(0) < TP // TM_PROJ, xp_ref[...], xs_ref[...])
    xb = x.astype(BF16)

    @pl.when(n < 2)
    def _():
        scale = jnp.where(n == 1, DK ** -0.5, 1.0).astype(F32)
        cos = cos_ref[...] * scale
        sin = sin_ref[...] * scale
        half = DK // 2
        for h in range(HEADS):
            p = jnp.dot(xb, w_ref[:, h * DK:(h + 1) * DK], preferred_element_type=F32)
            t1 = p[:, :half]
            t2 = p[:, half:]
            o_ref[:, h * DK:h * DK + half] = (t1 * cos - t2 * sin).astype(BF16)
            o_ref[:, h * DK + half:(h + 1) * DK] = (t1 * sin + t2 * cos).astype(BF16)

    @pl.when(n >= 2)
    def _():
        for h in range(4):
            p = jnp.dot(xb, w_ref[:, h * 256:(h + 1) * 256], preferred_element_type=F32)
            o_ref[:, h * 256:(h + 1) * 256] = p.astype(BF16)


def _ret_project(x_prompt, x_sample, w_bf16, cos_t, sin_t):
    n_prompt_tiles = TP // TM_PROJ
    tiles_per_seq = SEQ // TM_PROJ

    def tab_map(i, n):
        return (jnp.where(i < n_prompt_tiles, i % tiles_per_seq, tiles_per_seq), 0)

    return pl.pallas_call(
        _proj_kernel,
        grid=(T // TM_PROJ, RET_IN // 1024),
        in_specs=[
            pl.BlockSpec((TM_PROJ, D), lambda i, n: (jnp.minimum(i, n_prompt_tiles - 1), 0)),
            pl.BlockSpec((TM_PROJ, D), lambda i, n: (jnp.maximum(i - n_prompt_tiles, 0), 0)),
            pl.BlockSpec((D, 1024), lambda i, n: (0, n)),
            pl.BlockSpec((TM_PROJ, LANES), tab_map),
            pl.BlockSpec((TM_PROJ, LANES), tab_map),
        ],
        out_specs=pl.BlockSpec((TM_PROJ, 1024), lambda i, n: (i, n)),
        out_shape=jax.ShapeDtypeStruct((T, RET_IN), BF16),
        compiler_params=_cparams(("arbitrary", "arbitrary")),
        name="ret_project",
    )(x_prompt, x_sample, w_bf16, cos_t, sin_t)


def _ret_core_kernel(has_init, n_steps, *refs):
    if has_init:
        (q_ref, k_ref, v_ref, mask_ref, qd_ref, kd_ref, bd_ref, s0_ref,
         o_ref, sout_ref, s_ref) = refs
    else:
        (q_ref, k_ref, v_ref, mask_ref, qd_ref, kd_ref, bd_ref,
         o_ref, sout_ref, s_ref) = refs
    c = pl.program_id(2)

    @pl.when(c == 0)
    def _():
        if has_init:
            s_ref[...] = s0_ref[0].astype(F32)
        else:
            s_ref[...] = jnp.zeros_like(s_ref)

    q = q_ref[...]
    k = k_ref[...]
    v = v_ref[...]
    s_prev = s_ref[...]
    scores = lax.dot_general(q, k, (((1,), (1,)), ((), ())), preferred_element_type=F32)
    scores = scores * mask_ref[0]
    qd = (q.astype(F32) * qd_ref[0]).astype(BF16)
    o = (jnp.dot(scores.astype(BF16), v, preferred_element_type=F32)
         + jnp.dot(qd, s_prev.astype(BF16), preferred_element_type=F32))
    kd = (k.astype(F32) * kd_ref[0]).astype(BF16)
    s_new = s_prev * bd_ref[0] + lax.dot_general(kd, v, (((0,), (0,)), ((), ())),
                                                 preferred_element_type=F32)
    s_ref[...] = s_new
    o = o * lax.rsqrt(jnp.mean(o * o, axis=-1, keepdims=True) + RMS_EPS)
    o_ref[...] = o.astype(BF16)

    @pl.when(c == n_steps - 1)
    def _():
        sout_ref[0] = s_new


def _decay_tables(rows):
    lg = jnp.log1p(-jnp.exp2(-5.0 - jnp.arange(HEADS, dtype=F32)))
    idx = jnp.arange(rows, dtype=F32)
    ch = jnp.arange(rows) // CHUNK
    diff = idx[:, None] - idx[None, :]
    same = ch[:, None] == ch[None, :]
    earlier = ch[None, :] < ch[:, None]
    expo = jnp.where(same, jnp.abs(diff), diff)
    w = jnp.exp(lg[:, None, None] * expo[None])
    mask = jnp.where((same | earlier)[None], w, 0.0).astype(F32)
    q_dec = jnp.exp(lg[:, None] * (idx + 1.0))[:, :, None]
    k_dec = jnp.exp(lg[:, None] * (rows - 1.0 - idx))[:, :, None]
    blk = jnp.broadcast_to(jnp.exp(lg * rows)[:, None, None], (HEADS, 1, DV))
    return mask, q_dec, k_dec, blk.astype(F32)


def _ret_core(p_all, n_seq, rows, n_steps, row_block0, s0=None):
    mask, q_dec, k_dec, blk = _decay_tables(rows)
    has_init = s0 is not None

    def rb(b, c):
        return row_block0 + b * n_steps + c

    in_specs = [
        pl.BlockSpec((rows, DK), lambda b, h, c: (rb(b, c), h)),
        pl.BlockSpec((rows, DK), lambda b, h, c: (rb(b, c), HEADS + h)),
        pl.BlockSpec((rows, DV), lambda b, h, c: (rb(b, c), HEADS + h)),
        pl.BlockSpec((1, rows, rows), lambda b, h, c: (h, 0, 0)),
        pl.BlockSpec((1, rows, 1), lambda b, h, c: (h, 0, 0)),
        pl.BlockSpec((1, rows, 1), lambda b, h, c: (h, 0, 0)),
        pl.BlockSpec((1, 1, DV), lambda b, h, c: (h, 0, 0)),
    ]
    args = [p_all, p_all, p_all, mask, q_dec, k_dec, blk]
    if has_init:
        in_specs.append(pl.BlockSpec((1, DK, DV), lambda b, h, c: (b * HEADS + h, 0, 0)))
        args.append(s0)
    return pl.pallas_call(
        functools.partial(_ret_core_kernel, has_init, n_steps),
        grid=(n_seq, HEADS, n_steps),
        in_specs=in_specs,
        out_specs=[
            pl.BlockSpec((rows, DV), lambda b, h, c: (b * n_steps + c, h)),
            pl.BlockSpec((1, DK, DV), lambda b, h, c: (b * HEADS + h, 0, 0)),
        ],
        out_shape=[
            jax.ShapeDtypeStruct((n_seq * n_steps * rows, RET_V), BF16),
            jax.ShapeDtypeStruct((n_seq * HEADS, DK, DV), F32),
        ],
        scratch_shapes=[pltpu.VMEM((DK, DV), F32)],
        compiler_params=_cparams(("arbitrary", "arbitrary", "arbitrary")),
        name="ret_core_sample" if has_init else "ret_core_prompt",
    )(*args)


def _layer_norm(y, g, b):
    mu = jnp.mean(y, axis=-1, keepdims=True)
    yc = y - mu
    var = jnp.mean(yc * yc, axis=-1, keepdims=True)
    return yc * lax.rsqrt(var + LN_EPS) * g + b


def _split_bf16(a):
    hi = a.astype(BF16)
    lo = (a - hi.astype(F32)).astype(BF16)
    return hi, lo


def _route(x1, rw_ref, rb_ref, te_ref, gt_ref):
    rows = x1.shape[0]
    xh, xl = _split_bf16(x1)
    wh, wl = _split_bf16(rw_ref[...])
    logits = (jnp.dot(xh, wh, preferred_element_type=F32)
              + jnp.dot(xl, wh, preferred_element_type=F32)
              + jnp.dot(xh, wl, preferred_element_type=F32)) + rb_ref[...]
    lane = lax.broadcasted_iota(I32, (rows, N_EXPERTS), 1)
    lane_k = lax.broadcasted_iota(I32, (rows, TOP_K), 1)
    te = jnp.zeros((rows, TOP_K), I32)
    tv = jnp.zeros((rows, TOP_K), F32)
    cur = logits
    for j in range(TOP_K):
        m = jnp.max(cur, axis=-1, keepdims=True)
        idx = jnp.min(jnp.where(cur == m, lane, N_EXPERTS), axis=-1, keepdims=True)
        te = jnp.where(lane_k == j, idx, te)
        tv = jnp.where(lane_k == j, m, tv)
        cur = jnp.where(lane == idx, -jnp.inf, cur)
    ex = jnp.exp(tv - jnp.max(tv, axis=-1, keepdims=True))
    gt_ref[...] = ex / jnp.sum(ex, axis=-1, keepdims=True)
    te_ref[...] = te


def _load_row_tiles(ref, row0, rows):
    return jnp.concatenate(
        [ref[pl.ds(row0 * SUBLANES + s, rows, stride=SUBLANES), :] for s in range(SUBLANES)], axis=1)


def _store_row_tiles(ref, val):
    rows = val.shape[0]
    for s in range(SUBLANES):
        ref[pl.ds(s, rows, stride=SUBLANES), :] = val[:, s * LANES:(s + 1) * LANES]


def _residual_norm_route(x, h, lg_ref, lb_ref, rw_ref, rb_ref, x1_ref, te_ref, gt_ref):
    x1 = _layer_norm(ALPHA * x + h, lg_ref[...], lb_ref[...])
    _store_row_tiles(x1_ref, x1)
    _route(x1, rw_ref, rb_ref, te_ref, gt_ref)


def _ret_out_kernel(op_ref, os_ref, g_ref, xp_ref, xs_ref, w_ref, lg_ref, lb_ref, rw_ref, rb_ref,
                    x1_ref, te_ref, gt_ref):
    is_prompt = pl.program_id(0) < TP // TM
    g = g_ref[...].astype(F32)
    o = jnp.where(is_prompt, op_ref[...], os_ref[...])
    a = (g * jax.nn.sigmoid(g) * o.astype(F32)).astype(BF16)
    h = jnp.dot(a, w_ref[...], preferred_element_type=F32)
    x = jnp.where(is_prompt, xp_ref[...], xs_ref[...])
    _residual_norm_route(x, h, lg_ref, lb_ref, rw_ref, rb_ref, x1_ref, te_ref, gt_ref)


def _row_spec(cols, col_block=0):
    return pl.BlockSpec((TM, cols), lambda i: (i, col_block))


def _const_spec(shape):
    nd = len(shape)
    return pl.BlockSpec(shape, lambda i: (0,) * nd)


def _tile_rows_spec(rows, index_map):
    return pl.BlockSpec((rows * SUBLANES, LANES), index_map)


_EPILOGUE_OUT_SPECS = [_tile_rows_spec(TM, lambda i: (i, 0)), _row_spec(TOP_K), _row_spec(TOP_K)]
_EPILOGUE_OUT_SHAPE = [
    jax.ShapeDtypeStruct((T * SUBLANES, LANES), F32),
    jax.ShapeDtypeStruct((T, TOP_K), I32),
    jax.ShapeDtypeStruct((T, TOP_K), F32),
]


def _ret_out(o_prompt, o_sample, p_all, x_prompt, x_sample, w_out_bf16, ln_g, ln_b, router_w,
             router_b):
    n_prompt_tiles = TP // TM

    def prompt_spec(cols):
        return pl.BlockSpec((TM, cols), lambda i: (jnp.minimum(i, n_prompt_tiles - 1), 0))

    def sample_spec(cols):
        return pl.BlockSpec((TM, cols), lambda i: (jnp.maximum(i - n_prompt_tiles, 0), 0))

    return pl.pallas_call(
        _ret_out_kernel,
        grid=(T // TM,),
        in_specs=[
            prompt_spec(RET_V), sample_spec(RET_V),
            _row_spec(RET_V, col_block=2),
            prompt_spec(D), sample_spec(D),
            _const_spec((RET_V, D)),
            _const_spec((1, D)), _const_spec((1, D)),
            _const_spec((D, N_EXPERTS)), _const_spec((1, N_EXPERTS)),
        ],
        out_specs=_EPILOGUE_OUT_SPECS,
        out_shape=_EPILOGUE_OUT_SHAPE,
        compiler_params=_cparams(("arbitrary",)),
        name="ret_out_norm_route",
    )(o_prompt, o_sample, p_all, x_prompt, x_sample, w_out_bf16, ln_g, ln_b, router_w, router_b)


SEQS_PER_TILE = TM // DEC_SEQ


def _group_cols(gi):
    return slice(gi * POOL_GC, (gi + 1) * POOL_GC)


def _window_pool(src_ref, rows, pos, dst_ref, row0):
    for gi, w in enumerate(POOL_WINDOWS):
        cols = _group_cols(gi)
        cur = src_ref[CARRY:CARRY + rows, cols]
        acc = cur
        for back in range(1, w):
            acc = acc + src_ref[CARRY - back:CARRY - back + rows, cols]
        cnt = jnp.minimum(pos + 1.0, float(w))
        dst_ref[row0:row0 + rows, cols] = acc / cnt - cur


def _pool_kernel(x_ref, pre_ref, win_ref, wg_ref, sc_ref, wout_ref, lg_ref, lb_ref, rw_ref, rb_ref,
                 x1_ref, te_ref, gt_ref, cachep_ref, caches_ref, full_ref, seq_ref, pooled_ref):
    i = pl.program_id(0)
    n_prompt_tiles = TP // TM
    tiles_per_seq = SEQ // TM
    x = x_ref[...]
    u = jnp.dot(x.astype(BF16), win_ref[...], preferred_element_type=F32)

    @pl.when(i < n_prompt_tiles)
    def _():
        j = i % tiles_per_seq

        @pl.when(j == 0)
        def _():
            full_ref[0:CARRY, :] = jnp.zeros((CARRY, D), F32)

        full_ref[CARRY:CARRY + TM, :] = u
        pos = (j * TM + lax.broadcasted_iota(I32, (TM, 1), 0)).astype(F32)
        _window_pool(full_ref, TM, pos, pooled_ref, 0)

        @pl.when(j == tiles_per_seq - 1)
        def _():
            cachep_ref[0] = full_ref[TM:CARRY + TM, :]

        full_ref[0:CARRY, :] = full_ref[TM:CARRY + TM, :]

    @pl.when(i >= n_prompt_tiles)
    def _():
        pos = (PAST_LEN + lax.broadcasted_iota(I32, (DEC_SEQ, 1), 0)).astype(F32)
        seq_ref[0:1, :] = jnp.zeros((1, D), F32)
        for r in range(SEQS_PER_TILE):
            seq_ref[1:CARRY, :] = pre_ref[r]
            seq_ref[CARRY:CARRY + DEC_SEQ, :] = u[r * DEC_SEQ:(r + 1) * DEC_SEQ, :]
            _window_pool(seq_ref, DEC_SEQ, pos, pooled_ref, r * DEC_SEQ)
            caches_ref[r] = seq_ref[DEC_SEQ:CARRY + DEC_SEQ, :]

    mixed = [jnp.dot(pooled_ref[:, _group_cols(gi)].astype(BF16), wg_ref[gi],
                     preferred_element_type=F32) for gi in range(len(POOL_WINDOWS))]
    mixed = jnp.concatenate(mixed, axis=-1) * sc_ref[...]
    h = jnp.dot(mixed.astype(BF16), wout_ref[...], preferred_element_type=F32)
    _residual_norm_route(x, h, lg_ref, lb_ref, rw_ref, rb_ref, x1_ref, te_ref, gt_ref)


def _pool_mixer(x, prefix, win, wgrp, scale, wout, ln_g, ln_b, router_w, router_b):
    n_prompt_tiles = TP // TM
    tiles_per_seq = SEQ // TM
    return pl.pallas_call(
        _pool_kernel,
        grid=(T // TM,),
        in_specs=[
            _row_spec(D),
            pl.BlockSpec((SEQS_PER_TILE, POOL_STATE, D),
                         lambda i: (jnp.maximum(i - n_prompt_tiles, 0), 0, 0)),
            _const_spec((D, D)), _const_spec((len(POOL_WINDOWS), POOL_GC, POOL_GC)),
            _const_spec((1, D)), _const_spec((D, D)),
            _const_spec((1, D)), _const_spec((1, D)),
            _const_spec((D, N_EXPERTS)), _const_spec((1, N_EXPERTS)),
        ],
        out_specs=_EPILOGUE_OUT_SPECS + [
            pl.BlockSpec((1, CARRY, D),
                         lambda i: (jnp.minimum(i, n_prompt_tiles - 1) // tiles_per_seq, 0, 0)),
            pl.BlockSpec((SEQS_PER_TILE, CARRY, D),
                         lambda i: (jnp.maximum(i - n_prompt_tiles, 0), 0, 0)),
        ],
        out_shape=_EPILOGUE_OUT_SHAPE + [
            jax.ShapeDtypeStruct((BATCH, CARRY, D), F32),
            jax.ShapeDtypeStruct((DEC_BATCH, CARRY, D), F32),
        ],
        scratch_shapes=[
            pltpu.VMEM((CARRY + TM, D), F32),
            pltpu.VMEM((CARRY + DEC_SEQ, D), F32),
            pltpu.VMEM((TM, D), F32),
        ],
        compiler_params=_cparams(("arbitrary",)),
        name="pool_norm_route",
    )(x, prefix, win, wgrp, scale, wout, ln_g, ln_b, router_w, router_b)


def _lane_cumsum(v):
    lane = lax.broadcasted_iota(I32, v.shape, 1)
    shift = 1
    while shift < LANES:
        v = v + jnp.where(lane >= shift, pltpu.roll(v, shift, axis=1), 0.0)
        shift *= 2
    return v


def _positions_kernel(te_ref, dest_ref, blk_ref, stat_ref, cnt_ref, base_ref):
    phase = pl.program_id(0)
    i = pl.program_id(1)
    te = te_ref[...]
    lane = lax.broadcasted_iota(I32, (TM, LANES), 1)
    hits = [lane == te[:, j:j + 1] for j in range(TOP_K)]
    onehot = sum(h.astype(F32) for h in hits)
    tile_cnt = jnp.sum(onehot, axis=0, keepdims=True)

    @pl.when((phase == 0) & (i == 0))
    def _():
        cnt_ref[...] = jnp.zeros_like(cnt_ref)

    @pl.when(phase == 0)
    def _():
        cnt_ref[...] = cnt_ref[...] + tile_cnt

    @pl.when((phase == 1) & (i == 0))
    def _():
        cnt = cnt_ref[...].astype(I32)
        padded = ((cnt + (BM - 1)) & ~(BM - 1)).astype(F32)
        pad_end = _lane_cumsum(padded)
        base_ref[...] = pad_end - padded
        stat_ref[...] = jnp.concatenate([cnt_ref[...], pad_end - padded, pad_end], axis=0)
        lane8 = lax.broadcasted_iota(I32, (NB_PAD, LANES), 1)
        start = (lax.broadcasted_iota(I32, (NB_PAD, LANES), 0) * BM).astype(F32)
        done = jnp.where((lane8 < N_EXPERTS) & (pad_end[0:1, :] <= start), 1.0, 0.0)
        blk = jnp.minimum(jnp.sum(done, axis=-1, keepdims=True), N_EXPERTS - 1.0)
        blk_ref[...] = blk.astype(I32)

    @pl.when(phase == 1)
    def _():
        r = lax.broadcasted_iota(I32, (TM, TM), 0)
        c = lax.broadcasted_iota(I32, (TM, TM), 1)
        tri = jnp.where(r > c, 1.0, 0.0).astype(BF16)
        before = jnp.dot(tri, onehot.astype(BF16), preferred_element_type=F32)
        slot = base_ref[0:1, :] + before
        lane_k = lax.broadcasted_iota(I32, (TM, TOP_K), 1)
        dest = jnp.zeros((TM, TOP_K), F32)
        for j in range(TOP_K):
            dj = jnp.sum(jnp.where(hits[j], slot, 0.0), axis=-1, keepdims=True)
            dest = jnp.where(lane_k == j, dj, dest)
        dest_ref[...] = dest.astype(I32)
        base_ref[...] = base_ref[...] + tile_cnt


def _positions(top_e):
    return pl.pallas_call(
        _positions_kernel,
        grid=(2, T // TM),
        in_specs=[pl.BlockSpec((TM, TOP_K), lambda p, i: (i, 0))],
        out_specs=[
            pl.BlockSpec((TM, TOP_K), lambda p, i: (i * p, 0)),
            pl.BlockSpec((NB_PAD, 1), lambda p, i: (0, 0)),
            pl.BlockSpec((24, LANES), lambda p, i: (0, 0)),
        ],
        out_shape=[
            jax.ShapeDtypeStruct((T, TOP_K), I32),
            jax.ShapeDtypeStruct((NB_PAD, 1), I32),
            jax.ShapeDtypeStruct((24, LANES), F32),
        ],
        scratch_shapes=[pltpu.VMEM((8, LANES), F32), pltpu.VMEM((8, LANES), F32)],
        compiler_params=_cparams(("arbitrary", "arbitrary")),
        name="moe_positions",
    )(top_e)


ZROWS = 256


def _tok_rows(ref, tok, n_tok):
    start = tok * SUBLANES
    if not isinstance(start, int):
        start = pl.multiple_of(start, SUBLANES)
    return ref.at[pl.ds(start, n_tok * SUBLANES)]


def _dispatch_kernel(cnt_ref, start_ref, x_ref, dest_hbm, xs_hbm, idx_ref, zero_ref, sem_idx, sem_row,
                     sem_zero):
    i = pl.program_id(0)
    n_idx = TM_DISP * TOP_K
    idx_copy = pltpu.make_async_copy(dest_hbm.at[pl.ds(i * n_idx, n_idx)], idx_ref, sem_idx)
    idx_copy.start()

    @pl.when(i == 0)
    def _():
        zero_ref[...] = jnp.zeros_like(zero_ref)

        def zero_copy(tok, n_tok):
            return pltpu.make_async_copy(zero_ref.at[pl.ds(0, n_tok * SUBLANES)],
                                         _tok_rows(xs_hbm, tok, n_tok), sem_zero)

        def fill_expert(e, carry):
            tok = start_ref[e] + cnt_ref[e]
            n_pad = (-cnt_ref[e]) & (BM - 1)
            size = ZROWS
            while size >= 1:
                take = (n_pad & size) != 0

                @pl.when(take)
                def _(tok=tok, size=size):
                    cp = zero_copy(tok, size)
                    cp.start()
                    cp.wait()

                tok = tok + jnp.where(take, size, 0)
                size //= 2
            return carry

        lax.fori_loop(0, N_EXPERTS, fill_expert, 0)

        def fill_unused(c, carry):
            cp = zero_copy(c * ZROWS, ZROWS)
            cp.start()
            cp.wait()
            return carry

        last = N_EXPERTS - 1
        used_rows = start_ref[last] + ((cnt_ref[last] + (BM - 1)) & ~(BM - 1))
        lax.fori_loop(used_rows // ZROWS, R_ROWS // ZROWS, fill_unused, 0)

    idx_copy.wait()

    for t in range(TM_DISP):
        for j in range(TOP_K):
            d = idx_ref[t * TOP_K + j]
            pltpu.make_async_copy(_tok_rows(x_ref, t, 1), _tok_rows(xs_hbm, d, 1),
                                  sem_row).start(priority=j % 2)
    for j in range(TOP_K):
        pltpu.make_async_copy(x_ref, _tok_rows(xs_hbm, 0, TM_DISP), sem_row).wait()


def _dispatch(x1, dest_flat, counts, starts):
    return pl.pallas_call(
        _dispatch_kernel,
        grid_spec=pltpu.PrefetchScalarGridSpec(
            num_scalar_prefetch=2,
            grid=(T // TM_DISP,),
            in_specs=[
                _tile_rows_spec(TM_DISP, lambda i, c, s: (i, 0)),
                pl.BlockSpec(memory_space=pl.ANY),
            ],
            out_specs=pl.BlockSpec(memory_space=pl.ANY),
            scratch_shapes=[
                pltpu.SMEM((TM_DISP * TOP_K,), I32),
                pltpu.VMEM((ZROWS * SUBLANES, LANES), F32),
                pltpu.SemaphoreType.DMA(()),
                pltpu.SemaphoreType.DMA(()),
                pltpu.SemaphoreType.DMA(()),
            ],
        ),
        out_shape=jax.ShapeDtypeStruct((R_ROWS * SUBLANES, LANES), F32),
        compiler_params=_cparams(("arbitrary",)),
        name="moe_dispatch",
    )(counts, starts, x1, dest_flat)


W_CHUNK = 64


def _split_even_odd(x):
    rows, cols = x.shape
    lane = lax.broadcasted_iota(I32, (rows, LANES), 1)
    idx_even = (2 * lane) % LANES
    idx_odd = (2 * lane + 1) % LANES
    low = lane < LANES // 2
    even, odd = [], []
    for c in range(cols // (2 * LANES)):
        a = x[:, (2 * c) * LANES:(2 * c + 1) * LANES]
        b = x[:, (2 * c + 1) * LANES:(2 * c + 2) * LANES]
        even.append(jnp.where(low, jnp.take_along_axis(a, idx_even, axis=1),
                              jnp.take_along_axis(b, idx_even, axis=1)))
        odd.append(jnp.where(low, jnp.take_along_axis(a, idx_odd, axis=1),
                             jnp.take_along_axis(b, idx_odd, axis=1)))
    return jnp.concatenate(even, axis=1), jnp.concatenate(odd, axis=1)


def _expert_kernel(blk_ref, used_ref, xs_ref, wu_ref, wd_ref, bu_ref, bd_ref, ys_ref,
                   wg_s, wl_s, wd_s, bg_s, bl_s):
    i = pl.program_id(0)
    active = i < used_ref[0]
    new_expert = (i == 0) | (blk_ref[i] != blk_ref[jnp.maximum(i - 1, 0)])

    @pl.when(active & new_expert)
    def _():
        def convert(r, carry):
            rows = pl.ds(pl.multiple_of(r * W_CHUNK, W_CHUNK), W_CHUNK)
            even, odd = _split_even_odd(wu_ref[0, 0, rows, :])
            wg_s[rows, :] = even.astype(BF16)
            wl_s[rows, :] = odd.astype(BF16)
            wd_s[rows, :] = wd_ref[0, 0, rows, :].astype(BF16)
            return carry

        lax.fori_loop(0, D // W_CHUNK, convert, 0)
        even, odd = _split_even_odd(jnp.broadcast_to(bu_ref[0, 0], (SUBLANES, 2 * D)))
        bg_s[...] = even
        bl_s[...] = odd

    @pl.when(active)
    def _():
        xb = _load_row_tiles(xs_ref, 0, BM).astype(BF16)
        glu = jnp.dot(xb, wg_s[...], preferred_element_type=F32) + bg_s[0:1, :]
        lin = jnp.dot(xb, wl_s[...], preferred_element_type=F32) + bl_s[0:1, :]
        glu = jnp.minimum(glu, SWIGLU_LIMIT)
        lin = jnp.clip(lin, -SWIGLU_LIMIT, SWIGLU_LIMIT)
        a = glu * jax.nn.sigmoid(SWIGLU_ALPHA * glu) * (lin + 1.0)
        y = jnp.dot(a.astype(BF16), wd_s[...], preferred_element_type=F32) + bd_ref[0, 0]
        _store_row_tiles(ys_ref, y)

    @pl.when(jnp.logical_not(active))
    def _():
        ys_ref[...] = jnp.zeros_like(ys_ref)


def _experts(layer, xs, blk_e, n_used, w_up, w_down, b_up, b_down):
    def row_map(i, blk, used):
        return (jnp.minimum(i, used[0] - 1), 0)

    def out_map(i, blk, used):
        return (i, 0)

    def w_map(i, blk, used):
        return (layer, blk[jnp.minimum(i, used[0] - 1)], 0, 0)

    return pl.pallas_call(
        _expert_kernel,
        grid_spec=pltpu.PrefetchScalarGridSpec(
            num_scalar_prefetch=2,
            grid=(NB,),
            in_specs=[
                _tile_rows_spec(BM, row_map),
                pl.BlockSpec((1, 1, D, 2 * D), w_map),
                pl.BlockSpec((1, 1, D, D), w_map),
                pl.BlockSpec((1, 1, 1, 2 * D), w_map),
                pl.BlockSpec((1, 1, 1, D), w_map),
            ],
            out_specs=_tile_rows_spec(BM, out_map),
            scratch_shapes=[
                pltpu.VMEM((D, D), BF16), pltpu.VMEM((D, D), BF16), pltpu.VMEM((D, D), BF16),
                pltpu.VMEM((SUBLANES, D), F32), pltpu.VMEM((SUBLANES, D), F32),
            ],
        ),
        out_shape=jax.ShapeDtypeStruct((R_ROWS * SUBLANES, LANES), F32),
        compiler_params=_cparams(("arbitrary",)),
        name="moe_experts",
    )(blk_e, n_used, xs, w_up, w_down, b_up, b_down)


N_COMB = T // TM_COMB


N_COMB_PROMPT = TP // TM_COMB


def _combine_kernel(split_out, x_ref, gt_ref, lg_ref, lb_ref, dest_hbm, ys_hbm, *refs):
    if split_out:
        op_ref, os_ref, idx_ref, buf_ref, sem_idx, sem_row = refs
    else:
        o_ref, idx_ref, buf_ref, sem_idx, sem_row = refs
    s = pl.program_id(0)
    n_idx = TM_COMB * TOP_K

    def start_gathers(slot):
        base = slot * n_idx
        idx_copy = pltpu.make_async_copy(dest_hbm.at[pl.ds(pl.multiple_of(s * n_idx, n_idx), n_idx)],
                                         idx_ref.at[pl.ds(base, n_idx)], sem_idx)
        idx_copy.start()
        idx_copy.wait()
        for t in range(TM_COMB):
            for j in range(TOP_K):
                d = idx_ref[base + t * TOP_K + j]
                pltpu.make_async_copy(_tok_rows(ys_hbm, d, 1),
                                      _tok_rows(buf_ref, base + j * TM_COMB + t, 1),
                                      sem_row.at[slot]).start(priority=j % 2)

    def finish(slot):
        base = slot * n_idx
        pltpu.make_async_copy(_tok_rows(ys_hbm, 0, n_idx), _tok_rows(buf_ref, base, n_idx),
                              sem_row.at[slot]).wait()
        gt = gt_ref[...]
        m = gt[:, 0:1] * _load_row_tiles(buf_ref, base, TM_COMB)
        for j in range(1, TOP_K):
            m = m + gt[:, j:j + 1] * _load_row_tiles(buf_ref, base + j * TM_COMB, TM_COMB)
        x1 = _load_row_tiles(x_ref, 0, TM_COMB)
        y = _layer_norm(ALPHA * x1 + m, lg_ref[...], lb_ref[...])
        if split_out:
            @pl.when(s - 1 < N_COMB_PROMPT)
            def _():
                op_ref[...] = y

            @pl.when(s - 1 >= N_COMB_PROMPT)
            def _():
                os_ref[...] = y
        else:
            o_ref[...] = y

    for slot in range(2):
        @pl.when((s < N_COMB) & (s % 2 == slot))
        def _(slot=slot):
            start_gathers(slot)

    for slot in range(2):
        @pl.when((s > 0) & ((s - 1) % 2 == slot))
        def _(slot=slot):
            finish(slot)


def _combine(x1, gates, ln_g, ln_b, dest_flat, ys, split_out):
    def tile_map(s):
        return (jnp.maximum(s - 1, 0), 0)

    if split_out:
        out_specs = [
            pl.BlockSpec((TM_COMB, D), lambda s: (jnp.clip(s - 1, 0, N_COMB_PROMPT - 1), 0)),
            pl.BlockSpec((TM_COMB, D), lambda s: (jnp.maximum(s - 1 - N_COMB_PROMPT, 0), 0)),
        ]
        out_shape = [jax.ShapeDtypeStruct((TP, D), F32), jax.ShapeDtypeStruct((TS, D), F32)]
    else:
        out_specs = pl.BlockSpec((TM_COMB, D), tile_map)
        out_shape = jax.ShapeDtypeStruct((T, D), F32)
    return pl.pallas_call(
        functools.partial(_combine_kernel, split_out),
        grid=(N_COMB + 1,),
        in_specs=[
            _tile_rows_spec(TM_COMB, tile_map),
            pl.BlockSpec((TM_COMB, TOP_K), tile_map),
            _const_spec((1, D)), _const_spec((1, D)),
            pl.BlockSpec(memory_space=pl.ANY),
            pl.BlockSpec(memory_space=pl.ANY),
        ],
        out_specs=out_specs,
        out_shape=out_shape,
        scratch_shapes=[
            pltpu.SMEM((2 * TM_COMB * TOP_K,), I32),
            pltpu.VMEM((2 * TM_COMB * TOP_K * SUBLANES, LANES), F32),
            pltpu.SemaphoreType.DMA(()),
            pltpu.SemaphoreType.DMA((2,)),
        ],
        compiler_params=_cparams(("arbitrary",)),
        name="moe_combine_norm",
    )(x1, gates, ln_g, ln_b, dest_flat, ys)


def _moe_layer(layer, x1, top_e, gates, w_up, b_up, w_down, b_down, ln_g, ln_b, split_out):
    dest, blk_e, stat = _positions(top_e)
    dest_flat = dest.reshape(T * TOP_K)
    counts = stat[0, :N_EXPERTS].astype(I32)
    starts = stat[8, :N_EXPERTS].astype(I32)
    n_used = (stat[16, N_EXPERTS - 1:N_EXPERTS].astype(I32)) // BM
    xs = _dispatch(x1, dest_flat, counts, starts)
    ys = _experts(layer, xs, blk_e.reshape(NB_PAD), n_used, w_up, w_down, b_up[:, :, None, :],
                  b_down[:, :, None, :])
    return _combine(x1, gates, ln_g, ln_b, dest_flat, ys, split_out)


def _rope_tables():
    half = DK // 2
    inv = 1.0 / (ROPE_BASE ** jnp.linspace(0.0, 1.0, half, dtype=F32))
    pos = jnp.concatenate([jnp.arange(SEQ), PAST_LEN + (jnp.arange(TM_PROJ) % DEC_SEQ)]).astype(F32)
    ang = pos[:, None] * inv[None, :]
    return jnp.cos(ang), jnp.sin(ang)


def kernel(x_prompt, x_sample, state_ret, cache_pool, ret_w_in, ret_w_out, pool_w_in, pool_w_grp,
           pool_scale, pool_w_out, ln1_g, ln1_b, ln2_g, ln2_b, router_w, router_b, w_up, b_up,
           w_down, b_down):
    xp = x_prompt.reshape(TP, D)
    xs = x_sample.reshape(TS, D)

    def vec(a):
        return a.reshape(1, -1)

    cos_t, sin_t = _rope_tables()
    p_all = _ret_project(xp, xs, ret_w_in[0].astype(BF16), cos_t, sin_t)
    o_prompt, s_prompt = _ret_core(p_all, BATCH, SUPER, SEQ // SUPER, 0)
    o_sample, s_sample = _ret_core(p_all, DEC_BATCH, DEC_SEQ, 1, TP // DEC_SEQ,
                                   s0=state_ret[0].reshape(DEC_BATCH * HEADS, DK, DV))
    x1, top_e, gates = _ret_out(o_prompt, o_sample, p_all, xp, xs, ret_w_out[0].astype(BF16),
                                vec(ln1_g[0]), vec(ln1_b[0]), router_w[0], vec(router_b[0]))
    x = _moe_layer(0, x1, top_e, gates, w_up, b_up, w_down, b_down, vec(ln2_g[0]), vec(ln2_b[0]),
                   split_out=False)

    x1, top_e, gates, cache_p, cache_s = _pool_mixer(
        x, cache_pool[0], pool_w_in[0].astype(BF16), pool_w_grp[0].astype(BF16),
        vec(pool_scale[0]), pool_w_out[0].astype(BF16), vec(ln1_g[1]), vec(ln1_b[1]), router_w[1],
        vec(router_b[1]))
    yp, ys = _moe_layer(1, x1, top_e, gates, w_up, b_up, w_down, b_down, vec(ln2_g[1]),
                        vec(ln2_b[1]), split_out=True)

    y_prompt = yp.reshape(BATCH, SEQ, D)
    y_sample = ys.reshape(DEC_BATCH, DEC_SEQ, D)
    state_ret_prompt = s_prompt.reshape(1, BATCH, HEADS, DK, DV)
    state_ret_sample = s_sample.reshape(1, DEC_BATCH, HEADS, DK, DV)
    cache_pool_prompt = cache_p[None, :, 1:, :]
    cache_pool_sample = cache_s[None, :, 1:, :]
    return (y_prompt, y_sample, state_ret_prompt, state_ret_sample, cache_pool_prompt,
            cache_pool_sample)
```

```python
import functools

import jax
import jax.numpy as jnp
from jax import lax
from jax.experimental import pallas as pl
from jax.experimental.pallas import tpu as pltpu
from jax.experimental.pallas import tpu_sc as plsc

F32 = jnp.float32
BF16 = jnp.bfloat16
I32 = jnp.int32

D = 1024
BATCH = 16
SEQ = 2048
DEC_BATCH = 32
DEC_SEQ = 64
PAST_LEN = 4096
TP = BATCH * SEQ
TS = DEC_BATCH * DEC_SEQ
T = TP + TS

HEADS = 4
DK = 256
DV = 512
RET_QK = HEADS * DK
RET_V = HEADS * DV
RET_IN = 2 * RET_QK + 2 * RET_V
ROPE_BASE = 10000.0
RMS_EPS = 1e-6
LN_EPS = 1e-5
ALPHA = 4.0 ** 0.25

POOL_WINDOWS = (2, 4, 8, 16)
POOL_GC = D // 4
POOL_STATE = 15
CARRY = 16

N_EXPERTS = 32
TOP_K = 4
SWIGLU_LIMIT = 7.0
SWIGLU_ALPHA = 1.702

LANES = 128
SUBLANES = 8
assert D == SUBLANES * LANES
SUPER = 256
CHUNK = 64

TM_PROJ = 1024
TM = 512
TM_COMB = 256
BM = 512
NB = -(-(T * TOP_K) // BM) + N_EXPERTS
NB_PAD = -(-NB // 8) * 8
R_ROWS = NB * BM

VMEM_LIMIT = 56 * 1024 * 1024


def _cparams(sem, vmem=VMEM_LIMIT):
    return pltpu.CompilerParams(dimension_semantics=sem, vmem_limit_bytes=vmem)


def _proj_kernel(xp_ref, xs_ref, w_ref, cos_ref, sin_ref, o_ref):
    n = pl.program_id(1)
    x = jnp.where(pl.program_id(0) < TP // TM_PROJ, xp_ref[...], xs_ref[...])
    xb = x.astype(BF16)

    @pl.when(n < 2)
    def _():
        scale = jnp.where(n == 1, DK ** -0.5, 1.0).astype(F32)
        cos = cos_ref[...] * scale
        sin = sin_ref[...] * scale
        half = DK // 2
        for h in range(HEADS):
            p = jnp.dot(xb, w_ref[:, h * DK:(h + 1) * DK], preferred_element_type=F32)
            t1 = p[:, :half]
            t2 = p[:, half:]
            o_ref[:, h * DK:h * DK + half] = (t1 * cos - t2 * sin).astype(BF16)
            o_ref[:, h * DK + half:(h + 1) * DK] = (t1 * sin + t2 * cos).astype(BF16)

    @pl.when(n >= 2)
    def _():
        for h in range(4):
            p = jnp.dot(xb, w_ref[:, h * 256:(h + 1) * 256], preferred_element_type=F32)
            o_ref[:, h * 256:(h + 1) * 256] = p.astype(BF16)


def _ret_project(x_prompt, x_sample, w_bf16, cos_t, sin_t):
    n_prompt_tiles = TP // TM_PROJ
    tiles_per_seq = SEQ // TM_PROJ

    def tab_map(i, n):
        return (jnp.where(i < n_prompt_tiles, i % tiles_per_seq, tiles_per_seq), 0)

    return pl.pallas_call(
        _proj_kernel,
        grid=(T // TM_PROJ, RET_IN // 1024),
        in_specs=[
            pl.BlockSpec((TM_PROJ, D), lambda i, n: (jnp.minimum(i, n_prompt_tiles - 1), 0)),
            pl.BlockSpec((TM_PROJ, D), lambda i, n: (jnp.maximum(i - n_prompt_tiles, 0), 0)),
            pl.BlockSpec((D, 1024), lambda i, n: (0, n)),
            pl.BlockSpec((TM_PROJ, LANES), tab_map),
            pl.BlockSpec((TM_PROJ, LANES), tab_map),
        ],
        out_specs=pl.BlockSpec((TM_PROJ, 1024), lambda i, n: (i, n)),
        out_shape=jax.ShapeDtypeStruct((T, RET_IN), BF16),
        compiler_params=_cparams(("arbitrary", "arbitrary")),
        name="ret_project",
    )(x_prompt, x_sample, w_bf16, cos_t, sin_t)


def _ret_core_kernel(has_init, n_steps, *refs):
    if has_init:
        (q_ref, k_ref, v_ref, mask_ref, qd_ref, kd_ref, bd_ref, s0_ref,
         o_ref, sout_ref, s_ref) = refs
    else:
        (q_ref, k_ref, v_ref, mask_ref, qd_ref, kd_ref, bd_ref,
         o_ref, sout_ref, s_ref) = refs
    c = pl.program_id(2)

    @pl.when(c == 0)
    def _():
        if has_init:
            s_ref[...] = s0_ref[0].astype(F32)
        else:
            s_ref[...] = jnp.zeros_like(s_ref)

    q = q_ref[...]
    k = k_ref[...]
    v = v_ref[...]
    s_prev = s_ref[...]
    scores = lax.dot_general(q, k, (((1,), (1,)), ((), ())), preferred_element_type=F32)
    scores = scores * mask_ref[0]
    qd = (q.astype(F32) * qd_ref[0]).astype(BF16)
    o = (jnp.dot(scores.astype(BF16), v, preferred_element_type=F32)
         + jnp.dot(qd, s_prev.astype(BF16), preferred_element_type=F32))
    kd = (k.astype(F32) * kd_ref[0]).astype(BF16)
    s_new = s_prev * bd_ref[0] + lax.dot_general(kd, v, (((0,), (0,)), ((), ())),
                                                 preferred_element_type=F32)
    s_ref[...] = s_new
    o = o * lax.rsqrt(jnp.mean(o * o, axis=-1, keepdims=True) + RMS_EPS)
    o_ref[...] = o.astype(BF16)

    @pl.when(c == n_steps - 1)
    def _():
        sout_ref[0] = s_new


def _decay_tables(rows):
    lg = jnp.log1p(-jnp.exp2(-5.0 - jnp.arange(HEADS, dtype=F32)))
    idx = jnp.arange(rows, dtype=F32)
    ch = jnp.arange(rows) // CHUNK
    diff = idx[:, None] - idx[None, :]
    same = ch[:, None] == ch[None, :]
    earlier = ch[None, :] < ch[:, None]
    expo = jnp.where(same, jnp.abs(diff), diff)
    w = jnp.exp(lg[:, None, None] * expo[None])
    mask = jnp.where((same | earlier)[None], w, 0.0).astype(F32)
    q_dec = jnp.exp(lg[:, None] * (idx + 1.0))[:, :, None]
    k_dec = jnp.exp(lg[:, None] * (rows - 1.0 - idx))[:, :, None]
    blk = jnp.broadcast_to(jnp.exp(lg * rows)[:, None, None], (HEADS, 1, DV))
    return mask, q_dec, k_dec, blk.astype(F32)


def _ret_core(p_all, n_seq, rows, n_steps, row_block0, s0=None):
    mask, q_dec, k_dec, blk = _decay_tables(rows)
    has_init = s0 is not None

    def rb(b, c):
        return row_block0 + b * n_steps + c

    in_specs = [
        pl.BlockSpec((rows, DK), lambda b, h, c: (rb(b, c), h)),
        pl.BlockSpec((rows, DK), lambda b, h, c: (rb(b, c), HEADS + h)),
        pl.BlockSpec((rows, DV), lambda b, h, c: (rb(b, c), HEADS + h)),
        pl.BlockSpec((1, rows, rows), lambda b, h, c: (h, 0, 0)),
        pl.BlockSpec((1, rows, 1), lambda b, h, c: (h, 0, 0)),
        pl.BlockSpec((1, rows, 1), lambda b, h, c: (h, 0, 0)),
        pl.BlockSpec((1, 1, DV), lambda b, h, c: (h, 0, 0)),
    ]
    args = [p_all, p_all, p_all, mask, q_dec, k_dec, blk]
    if has_init:
        in_specs.append(pl.BlockSpec((1, DK, DV), lambda b, h, c: (b * HEADS + h, 0, 0)))
        args.append(s0)
    return pl.pallas_call(
        functools.partial(_ret_core_kernel, has_init, n_steps),
        grid=(n_seq, HEADS, n_steps),
        in_specs=in_specs,
        out_specs=[
            pl.BlockSpec((rows, DV), lambda b, h, c: (b * n_steps + c, h)),
            pl.BlockSpec((1, DK, DV), lambda b, h, c: (b * HEADS + h, 0, 0)),
        ],
        out_shape=[
            jax.ShapeDtypeStruct((n_seq * n_steps * rows, RET_V), BF16),
            jax.ShapeDtypeStruct((n_seq * HEADS, DK, DV), F32),
        ],
        scratch_shapes=[pltpu.VMEM((DK, DV), F32)],
        compiler_params=_cparams(("arbitrary", "arbitrary", "arbitrary")),
        name="ret_core_sample" if has_init else "ret_core_prompt",
    )(*args)


def _layer_norm(y, g, b):
    mu = jnp.mean(y, axis=-1, keepdims=True)
    yc = y - mu
    var = jnp.mean(yc * yc, axis=-1, keepdims=True)
    return yc * lax.rsqrt(var + LN_EPS) * g + b


def _split_bf16(a):
    hi = a.astype(BF16)
    lo = (a - hi.astype(F32)).astype(BF16)
    return hi, lo


def _route(x1, rw_ref, rb_ref, te_ref, gt_ref):
    rows = x1.shape[0]
    xh, xl = _split_bf16(x1)
    wh, wl = _split_bf16(rw_ref[...])
    logits = (jnp.dot(xh, wh, preferred_element_type=F32)
              + jnp.dot(xl, wh, preferred_element_type=F32)
              + jnp.dot(xh, wl, preferred_element_type=F32)) + rb_ref[...]
    lane = lax.broadcasted_iota(I32, (rows, N_EXPERTS), 1)
    lane_k = lax.broadcasted_iota(I32, (rows, TOP_K), 1)
    te = jnp.zeros((rows, TOP_K), I32)
    tv = jnp.zeros((rows, TOP_K), F32)
    cur = logits
    for j in range(TOP_K):
        m = jnp.max(cur, axis=-1, keepdims=True)
        idx = jnp.min(jnp.where(cur == m, lane, N_EXPERTS), axis=-1, keepdims=True)
        te = jnp.where(lane_k == j, idx, te)
        tv = jnp.where(lane_k == j, m, tv)
        cur = jnp.where(lane == idx, -jnp.inf, cur)
    ex = jnp.exp(tv - jnp.max(tv, axis=-1, keepdims=True))
    gt_ref[...] = ex / jnp.sum(ex, axis=-1, keepdims=True)
    te_ref[...] = te


def _load_row_tiles(ref, row0, rows, tok_stride=1):
    return jnp.concatenate(
        [ref[pl.ds(row0 * SUBLANES + s, rows, stride=SUBLANES * tok_stride), :]
         for s in range(SUBLANES)], axis=1)


def _store_row_tiles(ref, val, row0=0):
    rows = val.shape[0]
    for s in range(SUBLANES):
        ref[pl.ds(row0 * SUBLANES + s, rows, stride=SUBLANES), :] = val[:, s * LANES:(s + 1) * LANES]


def _residual_norm_route(x, h, lg_ref, lb_ref, rw_ref, rb_ref, x1_ref, te_ref, gt_ref):
    x1 = _layer_norm(ALPHA * x + h, lg_ref[...], lb_ref[...])
    _store_row_tiles(x1_ref, x1)
    _route(x1, rw_ref, rb_ref, te_ref, gt_ref)


def _ret_out_kernel(op_ref, os_ref, g_ref, xp_ref, xs_ref, w_ref, lg_ref, lb_ref, rw_ref, rb_ref,
                    x1_ref, te_ref, gt_ref):
    is_prompt = pl.program_id(0) < TP // TM
    g = g_ref[...].astype(F32)
    o = jnp.where(is_prompt, op_ref[...], os_ref[...])
    a = (g * jax.nn.sigmoid(g) * o.astype(F32)).astype(BF16)
    h = jnp.dot(a, w_ref[...], preferred_element_type=F32)
    x = jnp.where(is_prompt, xp_ref[...], xs_ref[...])
    _residual_norm_route(x, h, lg_ref, lb_ref, rw_ref, rb_ref, x1_ref, te_ref, gt_ref)


def _row_spec(cols, col_block=0):
    return pl.BlockSpec((TM, cols), lambda i: (i, col_block))


def _const_spec(shape):
    nd = len(shape)
    return pl.BlockSpec(shape, lambda i: (0,) * nd)


def _tile_rows_spec(rows, index_map):
    return pl.BlockSpec((rows * SUBLANES, LANES), index_map)


_EPILOGUE_OUT_SPECS = [_tile_rows_spec(TM, lambda i: (i, 0)), _row_spec(TOP_K), _row_spec(TOP_K)]
_EPILOGUE_OUT_SHAPE = [
    jax.ShapeDtypeStruct((T * SUBLANES, LANES), F32),
    jax.ShapeDtypeStruct((T, TOP_K), I32),
    jax.ShapeDtypeStruct((T, TOP_K), F32),
]


def _ret_out(o_prompt, o_sample, p_all, x_prompt, x_sample, w_out_bf16, ln_g, ln_b, router_w,
             router_b):
    n_prompt_tiles = TP // TM

    def prompt_spec(cols):
        return pl.BlockSpec((TM, cols), lambda i: (jnp.minimum(i, n_prompt_tiles - 1), 0))

    def sample_spec(cols):
        return pl.BlockSpec((TM, cols), lambda i: (jnp.maximum(i - n_prompt_tiles, 0), 0))

    return pl.pallas_call(
        _ret_out_kernel,
        grid=(T // TM,),
        in_specs=[
            prompt_spec(RET_V), sample_spec(RET_V),
            _row_spec(RET_V, col_block=2),
            prompt_spec(D), sample_spec(D),
            _const_spec((RET_V, D)),
            _const_spec((1, D)), _const_spec((1, D)),
            _const_spec((D, N_EXPERTS)), _const_spec((1, N_EXPERTS)),
        ],
        out_specs=_EPILOGUE_OUT_SPECS,
        out_shape=_EPILOGUE_OUT_SHAPE,
        compiler_params=_cparams(("arbitrary",)),
        name="ret_out_norm_route",
    )(o_prompt, o_sample, p_all, x_prompt, x_sample, w_out_bf16, ln_g, ln_b, router_w, router_b)


SEQS_PER_TILE = TM // DEC_SEQ


def _group_cols(gi):
    return slice(gi * POOL_GC, (gi + 1) * POOL_GC)


def _window_pool(src_ref, rows, pos, dst_ref, row0):
    for gi, w in enumerate(POOL_WINDOWS):
        cols = _group_cols(gi)
        cur = src_ref[CARRY:CARRY + rows, cols]
        acc = cur
        for back in range(1, w):
            acc = acc + src_ref[CARRY - back:CARRY - back + rows, cols]
        cnt = jnp.minimum(pos + 1.0, float(w))
        dst_ref[row0:row0 + rows, cols] = acc / cnt - cur


def _pool_kernel(x_ref, pre_ref, win_ref, wg_ref, sc_ref, wout_ref, lg_ref, lb_ref, rw_ref, rb_ref,
                 x1_ref, te_ref, gt_ref, cachep_ref, caches_ref, full_ref, seq_ref, pooled_ref):
    i = pl.program_id(0)
    n_prompt_tiles = TP // TM
    tiles_per_seq = SEQ // TM
    x = x_ref[...]
    u = jnp.dot(x.astype(BF16), win_ref[...], preferred_element_type=F32)

    @pl.when(i < n_prompt_tiles)
    def _():
        j = i % tiles_per_seq

        @pl.when(j == 0)
        def _():
            full_ref[0:CARRY, :] = jnp.zeros((CARRY, D), F32)

        full_ref[CARRY:CARRY + TM, :] = u
        pos = (j * TM + lax.broadcasted_iota(I32, (TM, 1), 0)).astype(F32)
        _window_pool(full_ref, TM, pos, pooled_ref, 0)

        @pl.when(j == tiles_per_seq - 1)
        def _():
            cachep_ref[0] = full_ref[TM:CARRY + TM, :]

        full_ref[0:CARRY, :] = full_ref[TM:CARRY + TM, :]

    @pl.when(i >= n_prompt_tiles)
    def _():
        pos = (PAST_LEN + lax.broadcasted_iota(I32, (DEC_SEQ, 1), 0)).astype(F32)
        seq_ref[0:1, :] = jnp.zeros((1, D), F32)
        for r in range(SEQS_PER_TILE):
            seq_ref[1:CARRY, :] = pre_ref[r]
            seq_ref[CARRY:CARRY + DEC_SEQ, :] = u[r * DEC_SEQ:(r + 1) * DEC_SEQ, :]
            _window_pool(seq_ref, DEC_SEQ, pos, pooled_ref, r * DEC_SEQ)
            caches_ref[r] = seq_ref[DEC_SEQ:CARRY + DEC_SEQ, :]

    mixed = [jnp.dot(pooled_ref[:, _group_cols(gi)].astype(BF16), wg_ref[gi],
                     preferred_element_type=F32) for gi in range(len(POOL_WINDOWS))]
    mixed = jnp.concatenate(mixed, axis=-1) * sc_ref[...]
    h = jnp.dot(mixed.astype(BF16), wout_ref[...], preferred_element_type=F32)
    _residual_norm_route(x, h, lg_ref, lb_ref, rw_ref, rb_ref, x1_ref, te_ref, gt_ref)


def _pool_mixer(x, prefix, win, wgrp, scale, wout, ln_g, ln_b, router_w, router_b):
    n_prompt_tiles = TP // TM
    tiles_per_seq = SEQ // TM
    return pl.pallas_call(
        _pool_kernel,
        grid=(T // TM,),
        in_specs=[
            _row_spec(D),
            pl.BlockSpec((SEQS_PER_TILE, POOL_STATE, D),
                         lambda i: (jnp.maximum(i - n_prompt_tiles, 0), 0, 0)),
            _const_spec((D, D)), _const_spec((len(POOL_WINDOWS), POOL_GC, POOL_GC)),
            _const_spec((1, D)), _const_spec((D, D)),
            _const_spec((1, D)), _const_spec((1, D)),
            _const_spec((D, N_EXPERTS)), _const_spec((1, N_EXPERTS)),
        ],
        out_specs=_EPILOGUE_OUT_SPECS + [
            pl.BlockSpec((1, CARRY, D),
                         lambda i: (jnp.minimum(i, n_prompt_tiles - 1) // tiles_per_seq, 0, 0)),
            pl.BlockSpec((SEQS_PER_TILE, CARRY, D),
                         lambda i: (jnp.maximum(i - n_prompt_tiles, 0), 0, 0)),
        ],
        out_shape=_EPILOGUE_OUT_SHAPE + [
            jax.ShapeDtypeStruct((BATCH, CARRY, D), F32),
            jax.ShapeDtypeStruct((DEC_BATCH, CARRY, D), F32),
        ],
        scratch_shapes=[
            pltpu.VMEM((CARRY + TM, D), F32),
            pltpu.VMEM((CARRY + DEC_SEQ, D), F32),
            pltpu.VMEM((TM, D), F32),
        ],
        compiler_params=_cparams(("arbitrary",)),
        name="pool_norm_route",
    )(x, prefix, win, wgrp, scale, wout, ln_g, ln_b, router_w, router_b)


def _lane_cumsum(v):
    lane = lax.broadcasted_iota(I32, v.shape, 1)
    shift = 1
    while shift < LANES:
        v = v + jnp.where(lane >= shift, pltpu.roll(v, shift, axis=1), 0.0)
        shift *= 2
    return v


def _positions_kernel(te_ref, dest_ref, blk_ref, stat_ref, cnt_ref, base_ref):
    phase = pl.program_id(0)
    i = pl.program_id(1)
    te = te_ref[...]
    lane = lax.broadcasted_iota(I32, (TM, LANES), 1)
    hits = [lane == te[:, j:j + 1] for j in range(TOP_K)]
    onehot = sum(h.astype(F32) for h in hits)
    tile_cnt = jnp.sum(onehot, axis=0, keepdims=True)

    @pl.when((phase == 0) & (i == 0))
    def _():
        cnt_ref[...] = jnp.zeros_like(cnt_ref)

    @pl.when(phase == 0)
    def _():
        cnt_ref[...] = cnt_ref[...] + tile_cnt

    @pl.when((phase == 1) & (i == 0))
    def _():
        cnt = cnt_ref[...].astype(I32)
        padded = ((cnt + (BM - 1)) & ~(BM - 1)).astype(F32)
        pad_end = _lane_cumsum(padded)
        base_ref[...] = pad_end - padded
        stat_ref[...] = jnp.concatenate([cnt_ref[...], pad_end - padded, pad_end], axis=0)
        lane8 = lax.broadcasted_iota(I32, (NB_PAD, LANES), 1)
        start = (lax.broadcasted_iota(I32, (NB_PAD, LANES), 0) * BM).astype(F32)
        done = jnp.where((lane8 < N_EXPERTS) & (pad_end[0:1, :] <= start), 1.0, 0.0)
        blk = jnp.minimum(jnp.sum(done, axis=-1, keepdims=True), N_EXPERTS - 1.0)
        blk_ref[...] = blk.astype(I32)

    @pl.when(phase == 1)
    def _():
        r = lax.broadcasted_iota(I32, (TM, TM), 0)
        c = lax.broadcasted_iota(I32, (TM, TM), 1)
        tri = jnp.where(r > c, 1.0, 0.0).astype(BF16)
        before = jnp.dot(tri, onehot.astype(BF16), preferred_element_type=F32)
        slot = base_ref[0:1, :] + before
        lane_k = lax.broadcasted_iota(I32, (TM, TOP_K), 1)
        dest = jnp.zeros((TM, TOP_K), F32)
        for j in range(TOP_K):
            dj = jnp.sum(jnp.where(hits[j], slot, 0.0), axis=-1, keepdims=True)
            dest = jnp.where(lane_k == j, dj, dest)
        dest_ref[...] = dest.astype(I32)
        base_ref[...] = base_ref[...] + tile_cnt


def _positions(top_e):
    return pl.pallas_call(
        _positions_kernel,
        grid=(2, T // TM),
        in_specs=[pl.BlockSpec((TM, TOP_K), lambda p, i: (i, 0))],
        out_specs=[
            pl.BlockSpec((TM, TOP_K), lambda p, i: (i * p, 0)),
            pl.BlockSpec((NB_PAD, 1), lambda p, i: (0, 0)),
            pl.BlockSpec((24, LANES), lambda p, i: (0, 0)),
        ],
        out_shape=[
            jax.ShapeDtypeStruct((T, TOP_K), I32),
            jax.ShapeDtypeStruct((NB_PAD, 1), I32),
            jax.ShapeDtypeStruct((24, LANES), F32),
        ],
        scratch_shapes=[pltpu.VMEM((8, LANES), F32), pltpu.VMEM((8, LANES), F32)],
        compiler_params=_cparams(("arbitrary", "arbitrary")),
        name="moe_positions",
    )(top_e)


N_SLOTS = T * TOP_K
N_SPARE = 2 * BM
R_INV = (NB + 2) * BM
SC_LANES = 16
SC_WORKERS = 32
INV_PER_WORKER = R_INV // SC_WORKERS
DEST_CHUNK = N_SLOTS // 8
assert R_INV % (SC_WORKERS * SC_LANES) == 0 and DEST_CHUNK % SC_LANES == 0


def _inverse_permutation(dest_flat):
    mesh = plsc.VectorSubcoreMesh(core_axis_name="c", subcore_axis_name="s")

    @functools.partial(
        pl.kernel, mesh=mesh,
        out_type=jax.ShapeDtypeStruct((R_INV,), I32),
        scratch_types=[pltpu.VMEM((INV_PER_WORKER,), I32), pltpu.VMEM((DEST_CHUNK,), I32)],
        compiler_params=pltpu.CompilerParams(needs_layout_passes=False),
        name="moe_inverse_permutation",
    )
    def body(dest_hbm, inv_hbm, local, staged):
        worker = lax.axis_index("s") * 2 + lax.axis_index("c")
        base = worker * INV_PER_WORKER
        lane = lax.iota(I32, SC_LANES)

        @pl.loop(0, INV_PER_WORKER // SC_LANES)
        def _(v):
            row = base + v * SC_LANES + lane
            local[pl.ds(v * SC_LANES, SC_LANES)] = N_SLOTS + (row & (N_SPARE - 1))

        @pl.loop(0, N_SLOTS // DEST_CHUNK)
        def _(c):
            pltpu.sync_copy(dest_hbm.at[pl.ds(c * DEST_CHUNK, DEST_CHUNK)], staged)

            @pl.loop(0, DEST_CHUNK // SC_LANES)
            def _(j):
                rel = staged[pl.ds(j * SC_LANES, SC_LANES)] - base
                mine = (rel >= 0) & (rel < INV_PER_WORKER)
                ids = c * DEST_CHUNK + j * SC_LANES + lane
                plsc.store_scatter(local, [jnp.where(mine, rel, 0)], ids, mask=mine)

        pltpu.sync_copy(local, inv_hbm.at[pl.ds(base, INV_PER_WORKER)])

    return body(dest_flat)


def _tok_rows(ref, tok, n_tok):
    start = tok * SUBLANES
    if not isinstance(start, int):
        start = pl.multiple_of(start, SUBLANES)
    return ref.at[pl.ds(start, n_tok * SUBLANES)]


W_CHUNK = 64


def _split_even_odd(x):
    rows, cols = x.shape
    lane = lax.broadcasted_iota(I32, (rows, LANES), 1)
    idx_even = (2 * lane) % LANES
    idx_odd = (2 * lane + 1) % LANES
    low = lane < LANES // 2
    even, odd = [], []
    for c in range(cols // (2 * LANES)):
        a = x[:, (2 * c) * LANES:(2 * c + 1) * LANES]
        b = x[:, (2 * c + 1) * LANES:(2 * c + 2) * LANES]
        even.append(jnp.where(low, jnp.take_along_axis(a, idx_even, axis=1),
                              jnp.take_along_axis(b, idx_even, axis=1)))
        odd.append(jnp.where(low, jnp.take_along_axis(a, idx_odd, axis=1),
                             jnp.take_along_axis(b, idx_odd, axis=1)))
    return jnp.concatenate(even, axis=1), jnp.concatenate(odd, axis=1)


IDX_CHUNK = 2 * BM
IDX_RING = 4
GUARD_CHUNK = R_INV // IDX_CHUNK - 1
assert IDX_CHUNK == 1024 and R_INV % IDX_CHUNK == 0 and NB % 2 == 0


def _expert_kernel(blk_ref, used_ref, inv_hbm, x1_hbm, wu_ref, wd_ref, bu_ref, bd_ref, y4_hbm,
                   wg_s, wl_s, wd_s, bg_s, bl_s, xbuf0, xbuf1, ybuf0, ybuf1, idx_ref, sem_idx, sem_g,
                   sem_s):
    s = pl.program_id(0)
    n_used = used_ref[0]
    new_expert = (s == 0) | (blk_ref[s] != blk_ref[jnp.maximum(s - 1, 0)])
    xbuf = (xbuf0, xbuf1)
    ybuf = (ybuf0, ybuf1)

    def idx_base(block):
        b = jnp.maximum(block, 0)
        ring = ((b >> 1) & (IDX_RING - 1)) * IDX_CHUNK + (b & 1) * BM
        return jnp.where(block < 0, IDX_RING * IDX_CHUNK, ring)

    def fetch_chunk(chunk, slot):
        cp = pltpu.make_async_copy(
            inv_hbm.at[pl.ds(pl.multiple_of(chunk * IDX_CHUNK, IDX_CHUNK), IDX_CHUNK)],
            idx_ref.at[pl.ds(pl.multiple_of(slot * IDX_CHUNK, IDX_CHUNK), IDX_CHUNK)], sem_idx)
        cp.start()
        cp.wait()

    def start_gather(block, slot):
        base = idx_base(block)
        for r in range(BM):
            tok = jnp.minimum(lax.shift_right_logical(idx_ref[base + r], 2), T - 1)
            pltpu.make_async_copy(_tok_rows(x1_hbm, tok, 1), _tok_rows(xbuf[slot], r, 1),
                                  sem_g.at[slot]).start()

    def wait_gather(slot):
        pltpu.make_async_copy(_tok_rows(x1_hbm, 0, BM), xbuf[slot], sem_g.at[slot]).wait()

    def start_scatter(block, slot):
        base = idx_base(block)
        for r in range(BM):
            pltpu.make_async_copy(_tok_rows(ybuf[slot], r, 1),
                                  _tok_rows(y4_hbm, idx_ref[base + r], 1),
                                  sem_s.at[slot]).start(priority=1)

    def wait_scatter(slot):
        pltpu.make_async_copy(ybuf[slot], _tok_rows(y4_hbm, 0, BM), sem_s.at[slot]).wait()

    def run_block(cur):
        other = 1 - cur
        wait_gather(cur)

        @pl.when(s >= 1)
        def _():
            wait_scatter(cur)

        start_gather(s + 1, other)
        start_scatter(s - 1, other)
        xb = _load_row_tiles(xbuf[cur], 0, BM).astype(BF16)
        glu = jnp.dot(xb, wg_s[...], preferred_element_type=F32) + bg_s[0:1, :]
        lin = jnp.dot(xb, wl_s[...], preferred_element_type=F32) + bl_s[0:1, :]
        glu = jnp.minimum(glu, SWIGLU_LIMIT)
        lin = jnp.clip(lin, -SWIGLU_LIMIT, SWIGLU_LIMIT)
        a = glu * jax.nn.sigmoid(SWIGLU_ALPHA * glu) * (lin + 1.0)
        y = jnp.dot(a.astype(BF16), wd_s[...], preferred_element_type=F32) + bd_ref[0, 0]
        _store_row_tiles(ybuf[cur], y)

        @pl.when(s == n_used - 1)
        def _():
            start_scatter(s, cur)
            wait_scatter(other)
            wait_scatter(cur)
            wait_gather(other)

    @pl.when(s < n_used)
    def _():
        @pl.when(s == 0)
        def _():
            ybuf0[...] = jnp.zeros_like(ybuf0)
            ybuf1[...] = jnp.zeros_like(ybuf1)
            fetch_chunk(0, 0)
            fetch_chunk(GUARD_CHUNK, IDX_RING)
            for half in range(2):
                cp = pltpu.make_async_copy(ybuf[half], _tok_rows(y4_hbm, N_SLOTS + half * BM, BM),
                                           sem_idx)
                cp.start()
                cp.wait()
            start_gather(0, 0)

        @pl.when((s & 1) == 1)
        def _():
            chunk = lax.shift_right_logical(s + 1, 1)
            fetch_chunk(chunk, chunk & (IDX_RING - 1))

        @pl.when(new_expert)
        def _():
            def convert(r, carry):
                rows = pl.ds(pl.multiple_of(r * W_CHUNK, W_CHUNK), W_CHUNK)
                even, odd = _split_even_odd(wu_ref[0, 0, rows, :])
                wg_s[rows, :] = even.astype(BF16)
                wl_s[rows, :] = odd.astype(BF16)
                wd_s[rows, :] = wd_ref[0, 0, rows, :].astype(BF16)
                return carry

            lax.fori_loop(0, D // W_CHUNK, convert, 0)
            even, odd = _split_even_odd(jnp.broadcast_to(bu_ref[0, 0], (SUBLANES, 2 * D)))
            bg_s[...] = even
            bl_s[...] = odd

        for parity in range(2):
            @pl.when((s & 1) == parity)
            def _(parity=parity):
                run_block(parity)


def _experts(layer, x1, inv, blk_e, n_used, w_up, w_down, b_up, b_down):
    def w_map(i, blk, used):
        return (layer, blk[jnp.minimum(i, used[0] - 1)], 0, 0)

    return pl.pallas_call(
        _expert_kernel,
        grid_spec=pltpu.PrefetchScalarGridSpec(
            num_scalar_prefetch=2,
            grid=(NB,),
            in_specs=[
                pl.BlockSpec(memory_space=pl.ANY),
                pl.BlockSpec(memory_space=pl.ANY),
                pl.BlockSpec((1, 1, D, 2 * D), w_map),
                pl.BlockSpec((1, 1, D, D), w_map),
                pl.BlockSpec((1, 1, 1, 2 * D), w_map),
                pl.BlockSpec((1, 1, 1, D), w_map),
            ],
            out_specs=pl.BlockSpec(memory_space=pl.ANY),
            scratch_shapes=[
                pltpu.VMEM((D, D), BF16), pltpu.VMEM((D, D), BF16), pltpu.VMEM((D, D), BF16),
                pltpu.VMEM((SUBLANES, D), F32), pltpu.VMEM((SUBLANES, D), F32),
                pltpu.VMEM((BM * SUBLANES, LANES), F32), pltpu.VMEM((BM * SUBLANES, LANES), F32),
                pltpu.VMEM((BM * SUBLANES, LANES), F32), pltpu.VMEM((BM * SUBLANES, LANES), F32),
                pltpu.SMEM(((IDX_RING + 1) * IDX_CHUNK,), I32),
                pltpu.SemaphoreType.DMA(()),
                pltpu.SemaphoreType.DMA((2,)),
                pltpu.SemaphoreType.DMA((2,)),
            ],
        ),
        out_shape=jax.ShapeDtypeStruct(((N_SLOTS + N_SPARE) * SUBLANES, LANES), F32),
        compiler_params=_cparams(("arbitrary",)),
        name="moe_experts",
    )(blk_e, n_used, inv, x1, w_up, w_down, b_up, b_down)


N_COMB = T // TM_COMB
N_COMB_PROMPT = TP // TM_COMB


def _combine_kernel(split_out, x_ref, y4_ref, gt_ref, lg_ref, lb_ref, *out_refs):
    gt = gt_ref[...]
    m = gt[:, 0:1] * _load_row_tiles(y4_ref, 0, TM_COMB, tok_stride=TOP_K)
    for j in range(1, TOP_K):
        m = m + gt[:, j:j + 1] * _load_row_tiles(y4_ref, j, TM_COMB, tok_stride=TOP_K)
    x1 = _load_row_tiles(x_ref, 0, TM_COMB)
    y = _layer_norm(ALPHA * x1 + m, lg_ref[...], lb_ref[...])
    if split_out:
        op_ref, os_ref = out_refs
        i = pl.program_id(0)

        @pl.when(i < N_COMB_PROMPT)
        def _():
            op_ref[...] = y

        @pl.when(i >= N_COMB_PROMPT)
        def _():
            os_ref[...] = y
    else:
        out_refs[0][...] = y


def _combine(x1, y4, gates, ln_g, ln_b, split_out):
    if split_out:
        out_specs = [
            pl.BlockSpec((TM_COMB, D), lambda i: (jnp.minimum(i, N_COMB_PROMPT - 1), 0)),
            pl.BlockSpec((TM_COMB, D), lambda i: (jnp.maximum(i - N_COMB_PROMPT, 0), 0)),
        ]
        out_shape = [jax.ShapeDtypeStruct((TP, D), F32), jax.ShapeDtypeStruct((TS, D), F32)]
    else:
        out_specs = pl.BlockSpec((TM_COMB, D), lambda i: (i, 0))
        out_shape = jax.ShapeDtypeStruct((T, D), F32)
    return pl.pallas_call(
        functools.partial(_combine_kernel, split_out),
        grid=(N_COMB,),
        in_specs=[
            _tile_rows_spec(TM_COMB, lambda i: (i, 0)),
            _tile_rows_spec(TM_COMB * TOP_K, lambda i: (i, 0)),
            pl.BlockSpec((TM_COMB, TOP_K), lambda i: (i, 0)),
            _const_spec((1, D)), _const_spec((1, D)),
        ],
        out_specs=out_specs,
        out_shape=out_shape,
        compiler_params=_cparams(("arbitrary",)),
        name="moe_combine_norm",
    )(x1, y4, gates, ln_g, ln_b)


def _moe_layer(layer, x1, top_e, gates, w_up, b_up, w_down, b_down, ln_g, ln_b, split_out):
    dest, blk_e, stat = _positions(top_e)
    n_used = (stat[16, N_EXPERTS - 1:N_EXPERTS].astype(I32)) // BM
    inv = _inverse_permutation(dest.reshape(N_SLOTS))
    y4 = _experts(layer, x1, inv, blk_e.reshape(NB_PAD), n_used, w_up, w_down, b_up[:, :, None, :],
                  b_down[:, :, None, :])
    return _combine(x1, y4, gates, ln_g, ln_b, split_out)


def _rope_tables():
    half = DK // 2
    inv = 1.0 / (ROPE_BASE ** jnp.linspace(0.0, 1.0, half, dtype=F32))
    pos = jnp.concatenate([jnp.arange(SEQ), PAST_LEN + (jnp.arange(TM_PROJ) % DEC_SEQ)]).astype(F32)
    ang = pos[:, None] * inv[None, :]
    return jnp.cos(ang), jnp.sin(ang)


def kernel(x_prompt, x_sample, state_ret, cache_pool, ret_w_in, ret_w_out, pool_w_in, pool_w_grp,
           pool_scale, pool_w_out, ln1_g, ln1_b, ln2_g, ln2_b, router_w, router_b, w_up, b_up,
           w_down, b_down):
    xp = x_prompt.reshape(TP, D)
    xs = x_sample.reshape(TS, D)

    def vec(a):
        return a.reshape(1, -1)

    cos_t, sin_t = _rope_tables()
    p_all = _ret_project(xp, xs, ret_w_in[0].astype(BF16), cos_t, sin_t)
    o_prompt, s_prompt = _ret_core(p_all, BATCH, SUPER, SEQ // SUPER, 0)
    o_sample, s_sample = _ret_core(p_all, DEC_BATCH, DEC_SEQ, 1, TP // DEC_SEQ,
                                   s0=state_ret[0].reshape(DEC_BATCH * HEADS, DK, DV))
    x1, top_e, gates = _ret_out(o_prompt, o_sample, p_all, xp, xs, ret_w_out[0].astype(BF16),
                                vec(ln1_g[0]), vec(ln1_b[0]), router_w[0], vec(router_b[0]))
    x = _moe_layer(0, x1, top_e, gates, w_up, b_up, w_down, b_down, vec(ln2_g[0]), vec(ln2_b[0]),
                   split_out=False)

    x1, top_e, gates, cache_p, cache_s = _pool_mixer(
        x, cache_pool[0], pool_w_in[0].astype(BF16), pool_w_grp[0].astype(BF16),
        vec(pool_scale[0]), pool_w_out[0].astype(BF16), vec(ln1_g[1]), vec(ln1_b[1]), router_w[1],
        vec(router_b[1]))
    yp, ys = _moe_layer(1, x1, top_e, gates, w_up, b_up, w_down, b_down, vec(ln2_g[1]),
                        vec(ln2_b[1]), split_out=True)

    y_prompt = yp.reshape(BATCH, SEQ, D)
    y_sample = ys.reshape(DEC_BATCH, DEC_SEQ, D)
    state_ret_prompt = s_prompt.reshape(1, BATCH, HEADS, DK, DV)
    state_ret_sample = s_sample.reshape(1, DEC_BATCH, HEADS, DK, DV)
    cache_pool_prompt = cache_p[None, :, 1:, :]
    cache_pool_sample = cache_s[None, :, 1:, :]
    return (y_prompt, y_sample, state_ret_prompt, state_ret_sample, cache_pool_prompt,
            cache_pool_sample)
```

```python
import functools

import jax
import jax.numpy as jnp
from jax import lax
from jax.experimental import pallas as pl
from jax.experimental.pallas import tpu as pltpu
from jax.experimental.pallas import tpu_sc as plsc

F32 = jnp.float32
BF16 = jnp.bfloat16
I32 = jnp.int32

D = 1024
BATCH = 16
SEQ = 2048
DEC_BATCH = 32
DEC_SEQ = 64
PAST_LEN = 4096
TP = BATCH * SEQ
TS = DEC_BATCH * DEC_SEQ
T = TP + TS

HEADS = 4
DK = 256
DV = 512
RET_QK = HEADS * DK
RET_V = HEADS * DV
RET_IN = 2 * RET_QK + 2 * RET_V
ROPE_BASE = 10000.0
RMS_EPS = 1e-6
LN_EPS = 1e-5
ALPHA = 4.0 ** 0.25

POOL_WINDOWS = (2, 4, 8, 16)
POOL_GC = D // 4
POOL_STATE = 15
CARRY = 16

N_EXPERTS = 32
TOP_K = 4
SWIGLU_LIMIT = 7.0
SWIGLU_ALPHA = 1.702

LANES = 128
SUBLANES = 8
assert D == SUBLANES * LANES
SUPER = 256
CHUNK = 64

TM_PROJ = 1024
TM = 512
TM_COMB = 256
BM = 512
NB = -(-(T * TOP_K) // BM) + N_EXPERTS
NB_PAD = -(-NB // 8) * 8
R_ROWS = NB * BM

VMEM_LIMIT = 56 * 1024 * 1024


def _cparams(sem, vmem=VMEM_LIMIT):
    return pltpu.CompilerParams(dimension_semantics=sem, vmem_limit_bytes=vmem)


def _proj_kernel(xp_ref, xs_ref, w_ref, cos_ref, sin_ref, o_ref, xb_ref):
    n = pl.program_id(1)

    @pl.when(n == 0)
    def _():
        x = jnp.where(pl.program_id(0) < TP // TM_PROJ, xp_ref[...], xs_ref[...])
        xb_ref[...] = x.astype(BF16)

    xb = xb_ref[...]

    @pl.when(n < 2)
    def _():
        scale = jnp.where(n == 1, DK ** -0.5, 1.0).astype(F32)
        cos = cos_ref[...] * scale
        sin = sin_ref[...] * scale
        half = DK // 2
        for h in range(HEADS):
            p = jnp.dot(xb, w_ref[:, h * DK:(h + 1) * DK], preferred_element_type=F32)
            t1 = p[:, :half]
            t2 = p[:, half:]
            o_ref[:, h * DK:h * DK + half] = (t1 * cos - t2 * sin).astype(BF16)
            o_ref[:, h * DK + half:(h + 1) * DK] = (t1 * sin + t2 * cos).astype(BF16)

    @pl.when(n >= 2)
    def _():
        for h in range(4):
            p = jnp.dot(xb, w_ref[:, h * 256:(h + 1) * 256], preferred_element_type=F32)
            o_ref[:, h * 256:(h + 1) * 256] = p.astype(BF16)


def _ret_project(x_prompt, x_sample, w_bf16, cos_t, sin_t):
    n_prompt_tiles = TP // TM_PROJ
    tiles_per_seq = SEQ // TM_PROJ

    def tab_map(i, n):
        return (jnp.where(i < n_prompt_tiles, i % tiles_per_seq, tiles_per_seq), 0)

    return pl.pallas_call(
        _proj_kernel,
        grid=(T // TM_PROJ, RET_IN // 1024),
        in_specs=[
            pl.BlockSpec((TM_PROJ, D), lambda i, n: (jnp.minimum(i, n_prompt_tiles - 1), 0)),
            pl.BlockSpec((TM_PROJ, D), lambda i, n: (jnp.maximum(i - n_prompt_tiles, 0), 0)),
            pl.BlockSpec((D, 1024), lambda i, n: (0, n)),
            pl.BlockSpec((TM_PROJ, LANES), tab_map),
            pl.BlockSpec((TM_PROJ, LANES), tab_map),
        ],
        out_specs=pl.BlockSpec((TM_PROJ, 1024), lambda i, n: (i, n)),
        out_shape=jax.ShapeDtypeStruct((T, RET_IN), BF16),
        scratch_shapes=[pltpu.VMEM((TM_PROJ, D), BF16)],
        compiler_params=_cparams(("arbitrary", "arbitrary")),
        name="ret_project",
    )(x_prompt, x_sample, w_bf16, cos_t, sin_t)


def _ret_core_kernel(has_init, n_steps, *refs):
    if has_init:
        (q_ref, k_ref, v_ref, mask_ref, qd_ref, kd_ref, bd_ref, s0_ref,
         o_ref, sout_ref, s_ref) = refs
    else:
        (q_ref, k_ref, v_ref, mask_ref, qd_ref, kd_ref, bd_ref,
         o_ref, sout_ref, s_ref) = refs
    c = pl.program_id(2)

    @pl.when(c == 0)
    def _():
        if has_init:
            s_ref[...] = s0_ref[0].astype(F32)
        else:
            s_ref[...] = jnp.zeros_like(s_ref)

    q = q_ref[...]
    k = k_ref[...]
    v = v_ref[...]
    s_prev = s_ref[...]
    scores = lax.dot_general(q, k, (((1,), (1,)), ((), ())), preferred_element_type=F32)
    scores = scores * mask_ref[0]
    qd = (q.astype(F32) * qd_ref[0]).astype(BF16)
    o = (jnp.dot(scores.astype(BF16), v, preferred_element_type=F32)
         + jnp.dot(qd, s_prev.astype(BF16), preferred_element_type=F32))
    kd = (k.astype(F32) * kd_ref[0]).astype(BF16)
    s_new = s_prev * bd_ref[0] + lax.dot_general(kd, v, (((0,), (0,)), ((), ())),
                                                 preferred_element_type=F32)
    s_ref[...] = s_new
    o = o * lax.rsqrt(jnp.mean(o * o, axis=-1, keepdims=True) + RMS_EPS)
    o_ref[...] = o.astype(BF16)

    @pl.when(c == n_steps - 1)
    def _():
        sout_ref[0] = s_new


def _decay_tables(rows):
    lg = jnp.log1p(-jnp.exp2(-5.0 - jnp.arange(HEADS, dtype=F32)))
    idx = jnp.arange(rows, dtype=F32)
    ch = jnp.arange(rows) // CHUNK
    diff = idx[:, None] - idx[None, :]
    same = ch[:, None] == ch[None, :]
    earlier = ch[None, :] < ch[:, None]
    expo = jnp.where(same, jnp.abs(diff), diff)
    w = jnp.exp(lg[:, None, None] * expo[None])
    mask = jnp.where((same | earlier)[None], w, 0.0).astype(F32)
    q_dec = jnp.exp(lg[:, None] * (idx + 1.0))[:, :, None]
    k_dec = jnp.exp(lg[:, None] * (rows - 1.0 - idx))[:, :, None]
    blk = jnp.broadcast_to(jnp.exp(lg * rows)[:, None, None], (HEADS, 1, DV))
    return mask, q_dec, k_dec, blk.astype(F32)


def _ret_core(p_all, n_seq, rows, n_steps, row_block0, s0=None):
    mask, q_dec, k_dec, blk = _decay_tables(rows)
    has_init = s0 is not None

    def rb(b, c):
        return row_block0 + b * n_steps + c

    in_specs = [
        pl.BlockSpec((rows, DK), lambda b, h, c: (rb(b, c), h)),
        pl.BlockSpec((rows, DK), lambda b, h, c: (rb(b, c), HEADS + h)),
        pl.BlockSpec((rows, DV), lambda b, h, c: (rb(b, c), HEADS + h)),
        pl.BlockSpec((1, rows, rows), lambda b, h, c: (h, 0, 0)),
        pl.BlockSpec((1, rows, 1), lambda b, h, c: (h, 0, 0)),
        pl.BlockSpec((1, rows, 1), lambda b, h, c: (h, 0, 0)),
        pl.BlockSpec((1, 1, DV), lambda b, h, c: (h, 0, 0)),
    ]
    args = [p_all, p_all, p_all, mask, q_dec, k_dec, blk]
    if has_init:
        in_specs.append(pl.BlockSpec((1, DK, DV), lambda b, h, c: (b * HEADS + h, 0, 0)))
        args.append(s0)
    return pl.pallas_call(
        functools.partial(_ret_core_kernel, has_init, n_steps),
        grid=(n_seq, HEADS, n_steps),
        in_specs=in_specs,
        out_specs=[
            pl.BlockSpec((rows, DV), lambda b, h, c: (b * n_steps + c, h)),
            pl.BlockSpec((1, DK, DV), lambda b, h, c: (b * HEADS + h, 0, 0)),
        ],
        out_shape=[
            jax.ShapeDtypeStruct((n_seq * n_steps * rows, RET_V), BF16),
            jax.ShapeDtypeStruct((n_seq * HEADS, DK, DV), F32),
        ],
        scratch_shapes=[pltpu.VMEM((DK, DV), F32)],
        compiler_params=_cparams(("arbitrary", "arbitrary", "arbitrary")),
        name="ret_core_sample" if has_init else "ret_core_prompt",
    )(*args)


def _layer_norm(y, g, b):
    mu = jnp.mean(y, axis=-1, keepdims=True)
    yc = y - mu
    var = jnp.mean(yc * yc, axis=-1, keepdims=True)
    return yc * lax.rsqrt(var + LN_EPS) * g + b


def _split_bf16(a):
    hi = a.astype(BF16)
    lo = (a - hi.astype(F32)).astype(BF16)
    return hi, lo


def _route(x1, rw_ref, rb_ref, te_ref, gt_ref):
    rows = x1.shape[0]
    xh, xl = _split_bf16(x1)
    wh, wl = _split_bf16(rw_ref[...])
    logits = (jnp.dot(xh, wh, preferred_element_type=F32)
              + jnp.dot(xl, wh, preferred_element_type=F32)
              + jnp.dot(xh, wl, preferred_element_type=F32)) + rb_ref[...]
    lane = lax.broadcasted_iota(I32, (rows, N_EXPERTS), 1)
    lane_k = lax.broadcasted_iota(I32, (rows, TOP_K), 1)
    te = jnp.zeros((rows, TOP_K), I32)
    tv = jnp.zeros((rows, TOP_K), F32)
    cur = logits
    for j in range(TOP_K):
        m = jnp.max(cur, axis=-1, keepdims=True)
        idx = jnp.min(jnp.where(cur == m, lane, N_EXPERTS), axis=-1, keepdims=True)
        te = jnp.where(lane_k == j, idx, te)
        tv = jnp.where(lane_k == j, m, tv)
        cur = jnp.where(lane == idx, -jnp.inf, cur)
    ex = jnp.exp(tv - jnp.max(tv, axis=-1, keepdims=True))
    gt_ref[...] = ex / jnp.sum(ex, axis=-1, keepdims=True)
    te_ref[...] = te


def _load_row_tiles(ref, row0, rows, tok_stride=1):
    return jnp.concatenate(
        [ref[pl.ds(row0 * SUBLANES + s, rows, stride=SUBLANES * tok_stride), :]
         for s in range(SUBLANES)], axis=1)


def _store_row_tiles(ref, val, row0=0):
    rows = val.shape[0]
    for s in range(SUBLANES):
        ref[pl.ds(row0 * SUBLANES + s, rows, stride=SUBLANES), :] = val[:, s * LANES:(s + 1) * LANES]


def _residual_norm_route(x, h, lg_ref, lb_ref, rw_ref, rb_ref, x1_ref, te_ref, gt_ref):
    x1 = _layer_norm(ALPHA * x + h, lg_ref[...], lb_ref[...])
    _store_row_tiles(x1_ref, x1)
    _route(x1, rw_ref, rb_ref, te_ref, gt_ref)


def _ret_out_kernel(op_ref, os_ref, g_ref, xp_ref, xs_ref, w_ref, lg_ref, lb_ref, rw_ref, rb_ref,
                    x1_ref, te_ref, gt_ref):
    is_prompt = pl.program_id(0) < TP // TM
    g = g_ref[...].astype(F32)
    o = jnp.where(is_prompt, op_ref[...], os_ref[...])
    a = (g * jax.nn.sigmoid(g) * o.astype(F32)).astype(BF16)
    h = jnp.dot(a, w_ref[...], preferred_element_type=F32)
    x = jnp.where(is_prompt, xp_ref[...], xs_ref[...])
    _residual_norm_route(x, h, lg_ref, lb_ref, rw_ref, rb_ref, x1_ref, te_ref, gt_ref)


def _row_spec(cols, col_block=0):
    return pl.BlockSpec((TM, cols), lambda i: (i, col_block))


def _const_spec(shape):
    nd = len(shape)
    return pl.BlockSpec(shape, lambda i: (0,) * nd)


def _tile_rows_spec(rows, index_map):
    return pl.BlockSpec((rows * SUBLANES, LANES), index_map)


_EPILOGUE_OUT_SPECS = [_tile_rows_spec(TM, lambda i: (i, 0)), _row_spec(TOP_K), _row_spec(TOP_K)]
_EPILOGUE_OUT_SHAPE = [
    jax.ShapeDtypeStruct((T * SUBLANES, LANES), F32),
    jax.ShapeDtypeStruct((T, TOP_K), I32),
    jax.ShapeDtypeStruct((T, TOP_K), F32),
]


def _ret_out(o_prompt, o_sample, p_all, x_prompt, x_sample, w_out_bf16, ln_g, ln_b, router_w,
             router_b):
    n_prompt_tiles = TP // TM

    def prompt_spec(cols):
        return pl.BlockSpec((TM, cols), lambda i: (jnp.minimum(i, n_prompt_tiles - 1), 0))

    def sample_spec(cols):
        return pl.BlockSpec((TM, cols), lambda i: (jnp.maximum(i - n_prompt_tiles, 0), 0))

    return pl.pallas_call(
        _ret_out_kernel,
        grid=(T // TM,),
        in_specs=[
            prompt_spec(RET_V), sample_spec(RET_V),
            _row_spec(RET_V, col_block=2),
            prompt_spec(D), sample_spec(D),
            _const_spec((RET_V, D)),
            _const_spec((1, D)), _const_spec((1, D)),
            _const_spec((D, N_EXPERTS)), _const_spec((1, N_EXPERTS)),
        ],
        out_specs=_EPILOGUE_OUT_SPECS,
        out_shape=_EPILOGUE_OUT_SHAPE,
        compiler_params=_cparams(("arbitrary",)),
        name="ret_out_norm_route",
    )(o_prompt, o_sample, p_all, x_prompt, x_sample, w_out_bf16, ln_g, ln_b, router_w, router_b)


SEQS_PER_TILE = TM // DEC_SEQ


def _group_cols(gi):
    return slice(gi * POOL_GC, (gi + 1) * POOL_GC)


def _window_pool(src_ref, rows, pos, dst_ref, row0):
    for gi, w in enumerate(POOL_WINDOWS):
        cols = _group_cols(gi)
        cur = src_ref[CARRY:CARRY + rows, cols]
        acc = cur
        for back in range(1, w):
            acc = acc + src_ref[CARRY - back:CARRY - back + rows, cols]
        cnt = jnp.minimum(pos + 1.0, float(w))
        dst_ref[row0:row0 + rows, cols] = acc / cnt - cur


def _pool_kernel(x_ref, pre_ref, win_ref, wg_ref, sc_ref, wout_ref, lg_ref, lb_ref, rw_ref, rb_ref,
                 x1_ref, te_ref, gt_ref, cachep_ref, caches_ref, full_ref, seq_ref, pooled_ref):
    i = pl.program_id(0)
    n_prompt_tiles = TP // TM
    tiles_per_seq = SEQ // TM
    x = x_ref[...]
    u = jnp.dot(x.astype(BF16), win_ref[...], preferred_element_type=F32)

    @pl.when(i < n_prompt_tiles)
    def _():
        j = i % tiles_per_seq

        @pl.when(j == 0)
        def _():
            full_ref[0:CARRY, :] = jnp.zeros((CARRY, D), F32)

        full_ref[CARRY:CARRY + TM, :] = u
        pos = (j * TM + lax.broadcasted_iota(I32, (TM, 1), 0)).astype(F32)
        _window_pool(full_ref, TM, pos, pooled_ref, 0)

        @pl.when(j == tiles_per_seq - 1)
        def _():
            cachep_ref[0] = full_ref[TM:CARRY + TM, :]

        full_ref[0:CARRY, :] = full_ref[TM:CARRY + TM, :]

    @pl.when(i >= n_prompt_tiles)
    def _():
        pos = (PAST_LEN + lax.broadcasted_iota(I32, (DEC_SEQ, 1), 0)).astype(F32)
        seq_ref[0:1, :] = jnp.zeros((1, D), F32)
        for r in range(SEQS_PER_TILE):
            seq_ref[1:CARRY, :] = pre_ref[r]
            seq_ref[CARRY:CARRY + DEC_SEQ, :] = u[r * DEC_SEQ:(r + 1) * DEC_SEQ, :]
            _window_pool(seq_ref, DEC_SEQ, pos, pooled_ref, r * DEC_SEQ)
            caches_ref[r] = seq_ref[DEC_SEQ:CARRY + DEC_SEQ, :]

    mixed = [jnp.dot(pooled_ref[:, _group_cols(gi)].astype(BF16), wg_ref[gi],
                     preferred_element_type=F32) for gi in range(len(POOL_WINDOWS))]
    mixed = jnp.concatenate(mixed, axis=-1) * sc_ref[...]
    h = jnp.dot(mixed.astype(BF16), wout_ref[...], preferred_element_type=F32)
    _residual_norm_route(x, h, lg_ref, lb_ref, rw_ref, rb_ref, x1_ref, te_ref, gt_ref)


def _pool_mixer(x, prefix, win, wgrp, scale, wout, ln_g, ln_b, router_w, router_b):
    n_prompt_tiles = TP // TM
    tiles_per_seq = SEQ // TM
    return pl.pallas_call(
        _pool_kernel,
        grid=(T // TM,),
        in_specs=[
            _row_spec(D),
            pl.BlockSpec((SEQS_PER_TILE, POOL_STATE, D),
                         lambda i: (jnp.maximum(i - n_prompt_tiles, 0), 0, 0)),
            _const_spec((D, D)), _const_spec((len(POOL_WINDOWS), POOL_GC, POOL_GC)),
            _const_spec((1, D)), _const_spec((D, D)),
            _const_spec((1, D)), _const_spec((1, D)),
            _const_spec((D, N_EXPERTS)), _const_spec((1, N_EXPERTS)),
        ],
        out_specs=_EPILOGUE_OUT_SPECS + [
            pl.BlockSpec((1, CARRY, D),
                         lambda i: (jnp.minimum(i, n_prompt_tiles - 1) // tiles_per_seq, 0, 0)),
            pl.BlockSpec((SEQS_PER_TILE, CARRY, D),
                         lambda i: (jnp.maximum(i - n_prompt_tiles, 0), 0, 0)),
        ],
        out_shape=_EPILOGUE_OUT_SHAPE + [
            jax.ShapeDtypeStruct((BATCH, CARRY, D), F32),
            jax.ShapeDtypeStruct((DEC_BATCH, CARRY, D), F32),
        ],
        scratch_shapes=[
            pltpu.VMEM((CARRY + TM, D), F32),
            pltpu.VMEM((CARRY + DEC_SEQ, D), F32),
            pltpu.VMEM((TM, D), F32),
        ],
        compiler_params=_cparams(("arbitrary",)),
        name="pool_norm_route",
    )(x, prefix, win, wgrp, scale, wout, ln_g, ln_b, router_w, router_b)


def _lane_cumsum(v):
    lane = lax.broadcasted_iota(I32, v.shape, 1)
    shift = 1
    while shift < LANES:
        v = v + jnp.where(lane >= shift, pltpu.roll(v, shift, axis=1), 0.0)
        shift *= 2
    return v


def _positions_kernel(te_ref, dest_ref, blk_ref, stat_ref, cnt_ref, base_ref):
    phase = pl.program_id(0)
    i = pl.program_id(1)
    te = te_ref[...]
    lane = lax.broadcasted_iota(I32, (TM, LANES), 1)
    hits = [lane == te[:, j:j + 1] for j in range(TOP_K)]
    onehot = sum(h.astype(F32) for h in hits)
    tile_cnt = jnp.sum(onehot, axis=0, keepdims=True)

    @pl.when((phase == 0) & (i == 0))
    def _():
        cnt_ref[...] = jnp.zeros_like(cnt_ref)

    @pl.when(phase == 0)
    def _():
        cnt_ref[...] = cnt_ref[...] + tile_cnt

    @pl.when((phase == 1) & (i == 0))
    def _():
        cnt = cnt_ref[...].astype(I32)
        padded = ((cnt + (BM - 1)) & ~(BM - 1)).astype(F32)
        pad_end = _lane_cumsum(padded)
        base_ref[...] = pad_end - padded
        stat_ref[...] = jnp.concatenate([cnt_ref[...], pad_end - padded, pad_end], axis=0)
        lane8 = lax.broadcasted_iota(I32, (NB_PAD, LANES), 1)
        start = (lax.broadcasted_iota(I32, (NB_PAD, LANES), 0) * BM).astype(F32)
        done = jnp.where((lane8 < N_EXPERTS) & (pad_end[0:1, :] <= start), 1.0, 0.0)
        blk = jnp.minimum(jnp.sum(done, axis=-1, keepdims=True), N_EXPERTS - 1.0)
        blk_ref[...] = blk.astype(I32)

    @pl.when(phase == 1)
    def _():
        r = lax.broadcasted_iota(I32, (TM, TM), 0)
        c = lax.broadcasted_iota(I32, (TM, TM), 1)
        tri = jnp.where(r > c, 1.0, 0.0).astype(BF16)
        before = jnp.dot(tri, onehot.astype(BF16), preferred_element_type=F32)
        slot = base_ref[0:1, :] + before
        lane_k = lax.broadcasted_iota(I32, (TM, TOP_K), 1)
        dest = jnp.zeros((TM, TOP_K), F32)
        for j in range(TOP_K):
            dj = jnp.sum(jnp.where(hits[j], slot, 0.0), axis=-1, keepdims=True)
            dest = jnp.where(lane_k == j, dj, dest)
        dest_ref[...] = dest.astype(I32)
        base_ref[...] = base_ref[...] + tile_cnt


def _positions(top_e):
    return pl.pallas_call(
        _positions_kernel,
        grid=(2, T // TM),
        in_specs=[pl.BlockSpec((TM, TOP_K), lambda p, i: (i, 0))],
        out_specs=[
            pl.BlockSpec((TM, TOP_K), lambda p, i: (i * p, 0)),
            pl.BlockSpec((NB_PAD, 1), lambda p, i: (0, 0)),
            pl.BlockSpec((24, LANES), lambda p, i: (0, 0)),
        ],
        out_shape=[
            jax.ShapeDtypeStruct((T, TOP_K), I32),
            jax.ShapeDtypeStruct((NB_PAD, 1), I32),
            jax.ShapeDtypeStruct((24, LANES), F32),
        ],
        scratch_shapes=[pltpu.VMEM((8, LANES), F32), pltpu.VMEM((8, LANES), F32)],
        compiler_params=_cparams(("arbitrary", "arbitrary")),
        name="moe_positions",
    )(top_e)


N_SLOTS = T * TOP_K
R_INV = (NB + 2) * BM
SC_LANES = 16
SC_WORKERS = 32
INV_PER_WORKER = R_INV // SC_WORKERS
DEST_CHUNK = N_SLOTS // 8
assert R_INV % (SC_WORKERS * SC_LANES) == 0 and DEST_CHUNK % SC_LANES == 0


def _inverse_permutation(dest_flat):
    mesh = plsc.VectorSubcoreMesh(core_axis_name="c", subcore_axis_name="s")

    @functools.partial(
        pl.kernel, mesh=mesh,
        out_type=jax.ShapeDtypeStruct((R_INV,), I32),
        scratch_types=[pltpu.VMEM((INV_PER_WORKER,), I32), pltpu.VMEM((DEST_CHUNK,), I32)],
        compiler_params=pltpu.CompilerParams(needs_layout_passes=False),
        name="moe_inverse_permutation",
    )
    def body(dest_hbm, inv_hbm, local, staged):
        worker = lax.axis_index("s") * 2 + lax.axis_index("c")
        base = worker * INV_PER_WORKER
        lane = lax.iota(I32, SC_LANES)

        @pl.loop(0, INV_PER_WORKER // SC_LANES)
        def _(v):
            local[pl.ds(v * SC_LANES, SC_LANES)] = jnp.zeros((SC_LANES,), I32)

        @pl.loop(0, N_SLOTS // DEST_CHUNK)
        def _(c):
            pltpu.sync_copy(dest_hbm.at[pl.ds(c * DEST_CHUNK, DEST_CHUNK)], staged)

            @pl.loop(0, DEST_CHUNK // SC_LANES)
            def _(j):
                rel = staged[pl.ds(j * SC_LANES, SC_LANES)] - base
                mine = (rel >= 0) & (rel < INV_PER_WORKER)
                tok = (c * DEST_CHUNK + j * SC_LANES + lane) >> 2
                plsc.store_scatter(local, [jnp.where(mine, rel, 0)], tok, mask=mine)

        pltpu.sync_copy(local, inv_hbm.at[pl.ds(base, INV_PER_WORKER)])

    return body(dest_flat)


def _tok_rows(ref, tok, n_tok):
    start = tok * SUBLANES
    if not isinstance(start, int):
        start = pl.multiple_of(start, SUBLANES)
    return ref.at[pl.ds(start, n_tok * SUBLANES)]


W_CHUNK = 64


def _split_even_odd(x):
    rows, cols = x.shape
    lane = lax.broadcasted_iota(I32, (rows, LANES), 1)
    idx_even = (2 * lane) % LANES
    idx_odd = (2 * lane + 1) % LANES
    low = lane < LANES // 2
    even, odd = [], []
    for c in range(cols // (2 * LANES)):
        a = x[:, (2 * c) * LANES:(2 * c + 1) * LANES]
        b = x[:, (2 * c + 1) * LANES:(2 * c + 2) * LANES]
        even.append(jnp.where(low, jnp.take_along_axis(a, idx_even, axis=1),
                              jnp.take_along_axis(b, idx_even, axis=1)))
        odd.append(jnp.where(low, jnp.take_along_axis(a, idx_odd, axis=1),
                             jnp.take_along_axis(b, idx_odd, axis=1)))
    return jnp.concatenate(even, axis=1), jnp.concatenate(odd, axis=1)


IDX_CHUNK = 2 * BM
IDX_RING = 2
assert IDX_CHUNK == 1024 and R_INV % IDX_CHUNK == 0 and NB % 2 == 0


def _expert_kernel(blk_ref, used_ref, tok_hbm, x1_hbm, wu_ref, wd_ref, bu_ref, bd_ref, ys_ref,
                   wg_s, wl_s, wd_s, bg_s, bl_s, xbuf0, xbuf1, idx_ref, sem_idx, sem_g):
    s = pl.program_id(0)
    n_used = used_ref[0]
    new_expert = (s == 0) | (blk_ref[s] != blk_ref[jnp.maximum(s - 1, 0)])
    xbuf = (xbuf0, xbuf1)

    def idx_base(block):
        return ((block >> 1) & (IDX_RING - 1)) * IDX_CHUNK + (block & 1) * BM

    def fetch_chunk(chunk):
        slot = chunk & (IDX_RING - 1)
        cp = pltpu.make_async_copy(
            tok_hbm.at[pl.ds(pl.multiple_of(chunk * IDX_CHUNK, IDX_CHUNK), IDX_CHUNK)],
            idx_ref.at[pl.ds(pl.multiple_of(slot * IDX_CHUNK, IDX_CHUNK), IDX_CHUNK)], sem_idx)
        cp.start()
        cp.wait()

    def start_gather(block, slot):
        base = idx_base(block)
        for r in range(BM):
            pltpu.make_async_copy(_tok_rows(x1_hbm, idx_ref[base + r], 1),
                                  _tok_rows(xbuf[slot], r, 1), sem_g.at[slot]).start()

    def wait_gather(slot):
        pltpu.make_async_copy(_tok_rows(x1_hbm, 0, BM), xbuf[slot], sem_g.at[slot]).wait()

    def run_block(cur):
        other = 1 - cur
        wait_gather(cur)
        start_gather(s + 1, other)
        xb = _load_row_tiles(xbuf[cur], 0, BM).astype(BF16)
        glu = jnp.dot(xb, wg_s[...], preferred_element_type=F32) + bg_s[0:1, :]
        lin = jnp.dot(xb, wl_s[...], preferred_element_type=F32) + bl_s[0:1, :]
        glu = jnp.minimum(glu, SWIGLU_LIMIT)
        lin = jnp.clip(lin, -SWIGLU_LIMIT, SWIGLU_LIMIT)
        a = glu * jax.nn.sigmoid(SWIGLU_ALPHA * glu) * (lin + 1.0)
        y = jnp.dot(a.astype(BF16), wd_s[...], preferred_element_type=F32) + bd_ref[0, 0]
        _store_row_tiles(ys_ref, y)

        @pl.when(s == n_used - 1)
        def _():
            wait_gather(other)

    @pl.when(s >= n_used)
    def _():
        ys_ref[...] = jnp.zeros_like(ys_ref)

    @pl.when(s < n_used)
    def _():
        @pl.when(s == 0)
        def _():
            fetch_chunk(0)
            start_gather(0, 0)

        @pl.when((s & 1) == 1)
        def _():
            fetch_chunk(lax.shift_right_logical(s + 1, 1))

        @pl.when(new_expert)
        def _():
            def convert(r, carry):
                rows = pl.ds(pl.multiple_of(r * W_CHUNK, W_CHUNK), W_CHUNK)
                even, odd = _split_even_odd(wu_ref[0, 0, rows, :])
                wg_s[rows, :] = even.astype(BF16)
                wl_s[rows, :] = odd.astype(BF16)
                wd_s[rows, :] = wd_ref[0, 0, rows, :].astype(BF16)
                return carry

            lax.fori_loop(0, D // W_CHUNK, convert, 0)
            even, odd = _split_even_odd(jnp.broadcast_to(bu_ref[0, 0], (SUBLANES, 2 * D)))
            bg_s[...] = even
            bl_s[...] = odd

        for parity in range(2):
            @pl.when((s & 1) == parity)
            def _(parity=parity):
                run_block(parity)


def _experts(layer, x1, row_tok, blk_e, n_used, w_up, w_down, b_up, b_down):
    def w_map(i, blk, used):
        return (layer, blk[jnp.minimum(i, used[0] - 1)], 0, 0)

    return pl.pallas_call(
        _expert_kernel,
        grid_spec=pltpu.PrefetchScalarGridSpec(
            num_scalar_prefetch=2,
            grid=(NB,),
            in_specs=[
                pl.BlockSpec(memory_space=pl.ANY),
                pl.BlockSpec(memory_space=pl.ANY),
                pl.BlockSpec((1, 1, D, 2 * D), w_map),
                pl.BlockSpec((1, 1, D, D), w_map),
                pl.BlockSpec((1, 1, 1, 2 * D), w_map),
                pl.BlockSpec((1, 1, 1, D), w_map),
            ],
            out_specs=_tile_rows_spec(BM, lambda i, blk, used: (i, 0)),
            scratch_shapes=[
                pltpu.VMEM((D, D), BF16), pltpu.VMEM((D, D), BF16), pltpu.VMEM((D, D), BF16),
                pltpu.VMEM((SUBLANES, D), F32), pltpu.VMEM((SUBLANES, D), F32),
                pltpu.VMEM((BM * SUBLANES, LANES), F32), pltpu.VMEM((BM * SUBLANES, LANES), F32),
                pltpu.SMEM((IDX_RING * IDX_CHUNK,), I32),
                pltpu.SemaphoreType.DMA(()),
                pltpu.SemaphoreType.DMA((2,)),
            ],
        ),
        out_shape=jax.ShapeDtypeStruct((R_ROWS * SUBLANES, LANES), F32),
        compiler_params=_cparams(("arbitrary",)),
        name="moe_experts",
    )(blk_e, n_used, row_tok, x1, w_up, w_down, b_up, b_down)


N_COMB = T // TM_COMB
N_COMB_PROMPT = TP // TM_COMB


def _combine_kernel(split_out, x_ref, gt_ref, lg_ref, lb_ref, dest_hbm, ys_hbm, *refs):
    if split_out:
        op_ref, os_ref, idx_ref, buf_ref, sem_idx, sem_row = refs
    else:
        o_ref, idx_ref, buf_ref, sem_idx, sem_row = refs
    s = pl.program_id(0)
    n_idx = TM_COMB * TOP_K

    def start_gathers(slot):
        base = slot * n_idx
        idx_copy = pltpu.make_async_copy(dest_hbm.at[pl.ds(pl.multiple_of(s * n_idx, n_idx), n_idx)],
                                         idx_ref.at[pl.ds(base, n_idx)], sem_idx)
        idx_copy.start()
        idx_copy.wait()
        for t in range(TM_COMB):
            for j in range(TOP_K):
                d = idx_ref[base + t * TOP_K + j]
                pltpu.make_async_copy(_tok_rows(ys_hbm, d, 1),
                                      _tok_rows(buf_ref, base + j * TM_COMB + t, 1),
                                      sem_row.at[slot]).start(priority=j % 2)

    def finish(slot):
        base = slot * n_idx
        pltpu.make_async_copy(_tok_rows(ys_hbm, 0, n_idx), _tok_rows(buf_ref, base, n_idx),
                              sem_row.at[slot]).wait()
        gt = gt_ref[...]
        m = gt[:, 0:1] * _load_row_tiles(buf_ref, base, TM_COMB)
        for j in range(1, TOP_K):
            m = m + gt[:, j:j + 1] * _load_row_tiles(buf_ref, base + j * TM_COMB, TM_COMB)
        x1 = _load_row_tiles(x_ref, 0, TM_COMB)
        y = _layer_norm(ALPHA * x1 + m, lg_ref[...], lb_ref[...])
        if split_out:
            @pl.when(s - 1 < N_COMB_PROMPT)
            def _():
                op_ref[...] = y

            @pl.when(s - 1 >= N_COMB_PROMPT)
            def _():
                os_ref[...] = y
        else:
            o_ref[...] = y

    for slot in range(2):
        @pl.when((s < N_COMB) & (s % 2 == slot))
        def _(slot=slot):
            start_gathers(slot)

    for slot in range(2):
        @pl.when((s > 0) & ((s - 1) % 2 == slot))
        def _(slot=slot):
            finish(slot)


def _combine(x1, gates, ln_g, ln_b, dest_flat, ys, split_out):
    def tile_map(s):
        return (jnp.maximum(s - 1, 0), 0)

    if split_out:
        out_specs = [
            pl.BlockSpec((TM_COMB, D), lambda s: (jnp.clip(s - 1, 0, N_COMB_PROMPT - 1), 0)),
            pl.BlockSpec((TM_COMB, D), lambda s: (jnp.maximum(s - 1 - N_COMB_PROMPT, 0), 0)),
        ]
        out_shape = [jax.ShapeDtypeStruct((TP, D), F32), jax.ShapeDtypeStruct((TS, D), F32)]
    else:
        out_specs = pl.BlockSpec((TM_COMB, D), tile_map)
        out_shape = jax.ShapeDtypeStruct((T, D), F32)
    return pl.pallas_call(
        functools.partial(_combine_kernel, split_out),
        grid=(N_COMB + 1,),
        in_specs=[
            _tile_rows_spec(TM_COMB, tile_map),
            pl.BlockSpec((TM_COMB, TOP_K), tile_map),
            _const_spec((1, D)), _const_spec((1, D)),
            pl.BlockSpec(memory_space=pl.ANY),
            pl.BlockSpec(memory_space=pl.ANY),
        ],
        out_specs=out_specs,
        out_shape=out_shape,
        scratch_shapes=[
            pltpu.SMEM((2 * TM_COMB * TOP_K,), I32),
            pltpu.VMEM((2 * TM_COMB * TOP_K * SUBLANES, LANES), F32),
            pltpu.SemaphoreType.DMA(()),
            pltpu.SemaphoreType.DMA((2,)),
        ],
        compiler_params=_cparams(("arbitrary",)),
        name="moe_combine_norm",
    )(x1, gates, ln_g, ln_b, dest_flat, ys)


def _moe_layer(layer, x1, top_e, gates, w_up, b_up, w_down, b_down, ln_g, ln_b, split_out):
    dest, blk_e, stat = _positions(top_e)
    dest_flat = dest.reshape(N_SLOTS)
    n_used = (stat[16, N_EXPERTS - 1:N_EXPERTS].astype(I32)) // BM
    row_tok = _inverse_permutation(dest_flat)
    ys = _experts(layer, x1, row_tok, blk_e.reshape(NB_PAD), n_used, w_up, w_down,
                  b_up[:, :, None, :], b_down[:, :, None, :])
    return _combine(x1, gates, ln_g, ln_b, dest_flat, ys, split_out)


def _rope_tables():
    half = DK // 2
    inv = 1.0 / (ROPE_BASE ** jnp.linspace(0.0, 1.0, half, dtype=F32))
    pos = jnp.concatenate([jnp.arange(SEQ), PAST_LEN + (jnp.arange(TM_PROJ) % DEC_SEQ)]).astype(F32)
    ang = pos[:, None] * inv[None, :]
    return jnp.cos(ang), jnp.sin(ang)


def kernel(x_prompt, x_sample, state_ret, cache_pool, ret_w_in, ret_w_out, pool_w_in, pool_w_grp,
           pool_scale, pool_w_out, ln1_g, ln1_b, ln2_g, ln2_b, router_w, router_b, w_up, b_up,
           w_down, b_down):
    xp = x_prompt.reshape(TP, D)
    xs = x_sample.reshape(TS, D)

    def vec(a):
        return a.reshape(1, -1)

    cos_t, sin_t = _rope_tables()
    p_all = _ret_project(xp, xs, ret_w_in[0].astype(BF16), cos_t, sin_t)
    o_prompt, s_prompt = _ret_core(p_all, BATCH, SUPER, SEQ // SUPER, 0)
    o_sample, s_sample = _ret_core(p_all, DEC_BATCH, DEC_SEQ, 1, TP // DEC_SEQ,
                                   s0=state_ret[0].reshape(DEC_BATCH * HEADS, DK, DV))
    x1, top_e, gates = _ret_out(o_prompt, o_sample, p_all, xp, xs, ret_w_out[0].astype(BF16),
                                vec(ln1_g[0]), vec(ln1_b[0]), router_w[0], vec(router_b[0]))
    x = _moe_layer(0, x1, top_e, gates, w_up, b_up, w_down, b_down, vec(ln2_g[0]), vec(ln2_b[0]),
                   split_out=False)

    x1, top_e, gates, cache_p, cache_s = _pool_mixer(
        x, cache_pool[0], pool_w_in[0].astype(BF16), pool_w_grp[0].astype(BF16),
        vec(pool_scale[0]), pool_w_out[0].astype(BF16), vec(ln1_g[1]), vec(ln1_b[1]), router_w[1],
        vec(router_b[1]))
    yp, ys = _moe_layer(1, x1, top_e, gates, w_up, b_up, w_down, b_down, vec(ln2_g[1]),
                        vec(ln2_b[1]), split_out=True)

    y_prompt = yp.reshape(BATCH, SEQ, D)
    y_sample = ys.reshape(DEC_BATCH, DEC_SEQ, D)
    state_ret_prompt = s_prompt.reshape(1, BATCH, HEADS, DK, DV)
    state_ret_sample = s_sample.reshape(1, DEC_BATCH, HEADS, DK, DV)
    cache_pool_prompt = cache_p[None, :, 1:, :]
    cache_pool_sample = cache_s[None, :, 1:, :]
    return (y_prompt, y_sample, state_ret_prompt, state_ret_sample, cache_pool_prompt,
            cache_pool_sample)
```

```python
import functools

import jax
import jax.numpy as jnp
from jax import lax
from jax.experimental import pallas as pl
from jax.experimental.pallas import tpu as pltpu

F32 = jnp.float32
BF16 = jnp.bfloat16
I32 = jnp.int32

D = 1024
BATCH = 16
SEQ = 2048
DEC_BATCH = 32
DEC_SEQ = 64
PAST_LEN = 4096
TP = BATCH * SEQ
TS = DEC_BATCH * DEC_SEQ
T = TP + TS

HEADS = 4
DK = 256
DV = 512
RET_QK = HEADS * DK
RET_V = HEADS * DV
RET_IN = 2 * RET_QK + 2 * RET_V
ROPE_BASE = 10000.0
RMS_EPS = 1e-6
LN_EPS = 1e-5
ALPHA = 4.0 ** 0.25

POOL_WINDOWS = (2, 4, 8, 16)
POOL_GC = D // 4
POOL_STATE = 15
CARRY = 16

N_EXPERTS = 32
TOP_K = 4
SWIGLU_LIMIT = 7.0
SWIGLU_ALPHA = 1.702

LANES = 128
SUBLANES = 8
assert D == SUBLANES * LANES
SUPER = 256
CHUNK = 64

TM_PROJ = 1024
TM = 512
BM = 512
NB = -(-(T * TOP_K) // BM) + N_EXPERTS
NB_PAD = -(-NB // 8) * 8
R_ROWS = NB * BM

VMEM_LIMIT = 56 * 1024 * 1024


def _cparams(sem, vmem=VMEM_LIMIT):
    return pltpu.CompilerParams(dimension_semantics=sem, vmem_limit_bytes=vmem)


def _proj_kernel(xp_ref, xs_ref, w_ref, cos_ref, sin_ref, o_ref, xb_ref):
    n = pl.program_id(1)

    @pl.when(n == 0)
    def _():
        x = jnp.where(pl.program_id(0) < TP // TM_PROJ, xp_ref[...], xs_ref[...])
        xb_ref[...] = x.astype(BF16)

    xb = xb_ref[...]

    @pl.when(n < 2)
    def _():
        scale = jnp.where(n == 1, DK ** -0.5, 1.0).astype(F32)
        cos = cos_ref[...] * scale
        sin = sin_ref[...] * scale
        half = DK // 2
        for h in range(HEADS):
            p = jnp.dot(xb, w_ref[:, h * DK:(h + 1) * DK], preferred_element_type=F32)
            t1 = p[:, :half]
            t2 = p[:, half:]
            o_ref[:, h * DK:h * DK + half] = (t1 * cos - t2 * sin).astype(BF16)
            o_ref[:, h * DK + half:(h + 1) * DK] = (t1 * sin + t2 * cos).astype(BF16)

    @pl.when(n >= 2)
    def _():
        for h in range(4):
            p = jnp.dot(xb, w_ref[:, h * 256:(h + 1) * 256], preferred_element_type=F32)
            o_ref[:, h * 256:(h + 1) * 256] = p.astype(BF16)


def _ret_project(x_prompt, x_sample, w_bf16, cos_t, sin_t):
    n_prompt_tiles = TP // TM_PROJ
    tiles_per_seq = SEQ // TM_PROJ

    def tab_map(i, n):
        return (jnp.where(i < n_prompt_tiles, i % tiles_per_seq, tiles_per_seq), 0)

    return pl.pallas_call(
        _proj_kernel,
        grid=(T // TM_PROJ, RET_IN // 1024),
        in_specs=[
            pl.BlockSpec((TM_PROJ, D), lambda i, n: (jnp.minimum(i, n_prompt_tiles - 1), 0)),
            pl.BlockSpec((TM_PROJ, D), lambda i, n: (jnp.maximum(i - n_prompt_tiles, 0), 0)),
            pl.BlockSpec((D, 1024), lambda i, n: (0, n)),
            pl.BlockSpec((TM_PROJ, LANES), tab_map),
            pl.BlockSpec((TM_PROJ, LANES), tab_map),
        ],
        out_specs=pl.BlockSpec((TM_PROJ, 1024), lambda i, n: (i, n)),
        out_shape=jax.ShapeDtypeStruct((T, RET_IN), BF16),
        scratch_shapes=[pltpu.VMEM((TM_PROJ, D), BF16)],
        compiler_params=_cparams(("arbitrary", "arbitrary")),
        name="ret_project",
    )(x_prompt, x_sample, w_bf16, cos_t, sin_t)


def _ret_core_kernel(has_init, n_steps, *refs):
    if has_init:
        (q_ref, k_ref, v_ref, mask_ref, qd_ref, kd_ref, bd_ref, s0_ref,
         o_ref, sout_ref, s_ref) = refs
    else:
        (q_ref, k_ref, v_ref, mask_ref, qd_ref, kd_ref, bd_ref,
         o_ref, sout_ref, s_ref) = refs
    c = pl.program_id(2)

    @pl.when(c == 0)
    def _():
        if has_init:
            s_ref[...] = s0_ref[0].astype(F32)
        else:
            s_ref[...] = jnp.zeros_like(s_ref)

    q = q_ref[...]
    k = k_ref[...]
    v = v_ref[...]
    s_prev = s_ref[...]
    scores = lax.dot_general(q, k, (((1,), (1,)), ((), ())), preferred_element_type=F32)
    scores = scores * mask_ref[0]
    qd = (q.astype(F32) * qd_ref[0]).astype(BF16)
    o = (jnp.dot(scores.astype(BF16), v, preferred_element_type=F32)
         + jnp.dot(qd, s_prev.astype(BF16), preferred_element_type=F32))
    kd = (k.astype(F32) * kd_ref[0]).astype(BF16)
    s_new = s_prev * bd_ref[0] + lax.dot_general(kd, v, (((0,), (0,)), ((), ())),
                                                 preferred_element_type=F32)
    s_ref[...] = s_new
    o = o * lax.rsqrt(jnp.mean(o * o, axis=-1, keepdims=True) + RMS_EPS)
    o_ref[...] = o.astype(BF16)

    @pl.when(c == n_steps - 1)
    def _():
        sout_ref[0] = s_new


def _decay_tables(rows):
    lg = jnp.log1p(-jnp.exp2(-5.0 - jnp.arange(HEADS, dtype=F32)))
    idx = jnp.arange(rows, dtype=F32)
    ch = jnp.arange(rows) // CHUNK
    diff = idx[:, None] - idx[None, :]
    same = ch[:, None] == ch[None, :]
    earlier = ch[None, :] < ch[:, None]
    expo = jnp.where(same, jnp.abs(diff), diff)
    w = jnp.exp(lg[:, None, None] * expo[None])
    mask = jnp.where((same | earlier)[None], w, 0.0).astype(F32)
    q_dec = jnp.exp(lg[:, None] * (idx + 1.0))[:, :, None]
    k_dec = jnp.exp(lg[:, None] * (rows - 1.0 - idx))[:, :, None]
    blk = jnp.broadcast_to(jnp.exp(lg * rows)[:, None, None], (HEADS, 1, DV))
    return mask, q_dec, k_dec, blk.astype(F32)


def _ret_core(p_all, n_seq, rows, n_steps, row_block0, s0=None):
    mask, q_dec, k_dec, blk = _decay_tables(rows)
    has_init = s0 is not None

    def rb(b, c):
        return row_block0 + b * n_steps + c

    in_specs = [
        pl.BlockSpec((rows, DK), lambda b, h, c: (rb(b, c), h)),
        pl.BlockSpec((rows, DK), lambda b, h, c: (rb(b, c), HEADS + h)),
        pl.BlockSpec((rows, DV), lambda b, h, c: (rb(b, c), HEADS + h)),
        pl.BlockSpec((1, rows, rows), lambda b, h, c: (h, 0, 0)),
        pl.BlockSpec((1, rows, 1), lambda b, h, c: (h, 0, 0)),
        pl.BlockSpec((1, rows, 1), lambda b, h, c: (h, 0, 0)),
        pl.BlockSpec((1, 1, DV), lambda b, h, c: (h, 0, 0)),
    ]
    args = [p_all, p_all, p_all, mask, q_dec, k_dec, blk]
    if has_init:
        in_specs.append(pl.BlockSpec((1, DK, DV), lambda b, h, c: (b * HEADS + h, 0, 0)))
        args.append(s0)
    return pl.pallas_call(
        functools.partial(_ret_core_kernel, has_init, n_steps),
        grid=(n_seq, HEADS, n_steps),
        in_specs=in_specs,
        out_specs=[
            pl.BlockSpec((rows, DV), lambda b, h, c: (b * n_steps + c, h)),
            pl.BlockSpec((1, DK, DV), lambda b, h, c: (b * HEADS + h, 0, 0)),
        ],
        out_shape=[
            jax.ShapeDtypeStruct((n_seq * n_steps * rows, RET_V), BF16),
            jax.ShapeDtypeStruct((n_seq * HEADS, DK, DV), F32),
        ],
        scratch_shapes=[pltpu.VMEM((DK, DV), F32)],
        compiler_params=_cparams(("arbitrary", "arbitrary", "arbitrary")),
        name="ret_core_sample" if has_init else "ret_core_prompt",
    )(*args)


def _layer_norm(y, g, b):
    mu = jnp.mean(y, axis=-1, keepdims=True)
    yc = y - mu
    var = jnp.mean(yc * yc, axis=-1, keepdims=True)
    return yc * lax.rsqrt(var + LN_EPS) * g + b


def _split_bf16(a):
    hi = a.astype(BF16)
    lo = (a - hi.astype(F32)).astype(BF16)
    return hi, lo


def _route(x1, rw_ref, rb_ref, te_ref, gt_ref):
    rows = x1.shape[0]
    xh, xl = _split_bf16(x1)
    wh, wl = _split_bf16(rw_ref[...])
    logits = (jnp.dot(xh, wh, preferred_element_type=F32)
              + jnp.dot(xl, wh, preferred_element_type=F32)
              + jnp.dot(xh, wl, preferred_element_type=F32)) + rb_ref[...]
    lane = lax.broadcasted_iota(I32, (rows, N_EXPERTS), 1)
    lane_k = lax.broadcasted_iota(I32, (rows, TOP_K), 1)
    te = jnp.zeros((rows, TOP_K), I32)
    tv = jnp.zeros((rows, TOP_K), F32)
    cur = logits
    for j in range(TOP_K):
        m = jnp.max(cur, axis=-1, keepdims=True)
        idx = jnp.min(jnp.where(cur == m, lane, N_EXPERTS), axis=-1, keepdims=True)
        te = jnp.where(lane_k == j, idx, te)
        tv = jnp.where(lane_k == j, m, tv)
        cur = jnp.where(lane == idx, -jnp.inf, cur)
    ex = jnp.exp(tv - jnp.max(tv, axis=-1, keepdims=True))
    gt_ref[...] = ex / jnp.sum(ex, axis=-1, keepdims=True)
    te_ref[...] = te


def _load_row_tiles(ref, row0, rows, tok_stride=1):
    return jnp.concatenate(
        [ref[pl.ds(row0 * SUBLANES + s, rows, stride=SUBLANES * tok_stride), :]
         for s in range(SUBLANES)], axis=1)


def _store_row_tiles(ref, val, row0=0):
    rows = val.shape[0]
    for s in range(SUBLANES):
        ref[pl.ds(row0 * SUBLANES + s, rows, stride=SUBLANES), :] = val[:, s * LANES:(s + 1) * LANES]


def _residual_norm_route(x, h, lg_ref, lb_ref, rw_ref, rb_ref, x1_ref, te_ref, gt_ref):
    x1 = _layer_norm(ALPHA * x + h, lg_ref[...], lb_ref[...])
    x1_ref[...] = x1
    _route(x1, rw_ref, rb_ref, te_ref, gt_ref)


def _ret_out_kernel(op_ref, os_ref, g_ref, xp_ref, xs_ref, w_ref, lg_ref, lb_ref, rw_ref, rb_ref,
                    x1_ref, te_ref, gt_ref):
    is_prompt = pl.program_id(0) < TP // TM
    g = g_ref[...].astype(F32)
    o = jnp.where(is_prompt, op_ref[...], os_ref[...])
    a = (g * jax.nn.sigmoid(g) * o.astype(F32)).astype(BF16)
    h = jnp.dot(a, w_ref[...], preferred_element_type=F32)
    x = jnp.where(is_prompt, xp_ref[...], xs_ref[...])
    _residual_norm_route(x, h, lg_ref, lb_ref, rw_ref, rb_ref, x1_ref, te_ref, gt_ref)


def _row_spec(cols, col_block=0):
    return pl.BlockSpec((TM, cols), lambda i: (i, col_block))


def _const_spec(shape):
    nd = len(shape)
    return pl.BlockSpec(shape, lambda i: (0,) * nd)


def _tile_rows_spec(rows, index_map):
    return pl.BlockSpec((rows * SUBLANES, LANES), index_map)


_EPILOGUE_OUT_SPECS = [_row_spec(D), _row_spec(TOP_K), _row_spec(TOP_K)]
_EPILOGUE_OUT_SHAPE = [
    jax.ShapeDtypeStruct((T, D), F32),
    jax.ShapeDtypeStruct((T, TOP_K), I32),
    jax.ShapeDtypeStruct((T, TOP_K), F32),
]


def _ret_out(o_prompt, o_sample, p_all, x_prompt, x_sample, w_out_bf16, ln_g, ln_b, router_w,
             router_b):
    n_prompt_tiles = TP // TM

    def prompt_spec(cols):
        return pl.BlockSpec((TM, cols), lambda i: (jnp.minimum(i, n_prompt_tiles - 1), 0))

    def sample_spec(cols):
        return pl.BlockSpec((TM, cols), lambda i: (jnp.maximum(i - n_prompt_tiles, 0), 0))

    return pl.pallas_call(
        _ret_out_kernel,
        grid=(T // TM,),
        in_specs=[
            prompt_spec(RET_V), sample_spec(RET_V),
            _row_spec(RET_V, col_block=2),
            prompt_spec(D), sample_spec(D),
            _const_spec((RET_V, D)),
            _const_spec((1, D)), _const_spec((1, D)),
            _const_spec((D, N_EXPERTS)), _const_spec((1, N_EXPERTS)),
        ],
        out_specs=_EPILOGUE_OUT_SPECS,
        out_shape=_EPILOGUE_OUT_SHAPE,
        compiler_params=_cparams(("arbitrary",)),
        name="ret_out_norm_route",
    )(o_prompt, o_sample, p_all, x_prompt, x_sample, w_out_bf16, ln_g, ln_b, router_w, router_b)


SEQS_PER_TILE = TM // DEC_SEQ


def _group_cols(gi):
    return slice(gi * POOL_GC, (gi + 1) * POOL_GC)


def _window_pool(src_ref, rows, pos, dst_ref, row0):
    for gi, w in enumerate(POOL_WINDOWS):
        cols = _group_cols(gi)
        cur = src_ref[CARRY:CARRY + rows, cols]
        acc = cur
        for back in range(1, w):
            acc = acc + src_ref[CARRY - back:CARRY - back + rows, cols]
        cnt = jnp.minimum(pos + 1.0, float(w))
        dst_ref[row0:row0 + rows, cols] = acc / cnt - cur


def _pool_kernel(x_ref, pre_ref, win_ref, wg_ref, sc_ref, wout_ref, lg_ref, lb_ref, rw_ref, rb_ref,
                 x1_ref, te_ref, gt_ref, cachep_ref, caches_ref, full_ref, seq_ref, pooled_ref):
    i = pl.program_id(0)
    n_prompt_tiles = TP // TM
    tiles_per_seq = SEQ // TM
    x = x_ref[...]
    u = jnp.dot(x.astype(BF16), win_ref[...], preferred_element_type=F32)

    @pl.when(i < n_prompt_tiles)
    def _():
        j = i % tiles_per_seq

        @pl.when(j == 0)
        def _():
            full_ref[0:CARRY, :] = jnp.zeros((CARRY, D), F32)

        full_ref[CARRY:CARRY + TM, :] = u
        pos = (j * TM + lax.broadcasted_iota(I32, (TM, 1), 0)).astype(F32)
        _window_pool(full_ref, TM, pos, pooled_ref, 0)

        @pl.when(j == tiles_per_seq - 1)
        def _():
            cachep_ref[0] = full_ref[TM:CARRY + TM, :]

        full_ref[0:CARRY, :] = full_ref[TM:CARRY + TM, :]

    @pl.when(i >= n_prompt_tiles)
    def _():
        pos = (PAST_LEN + lax.broadcasted_iota(I32, (DEC_SEQ, 1), 0)).astype(F32)
        seq_ref[0:1, :] = jnp.zeros((1, D), F32)
        for r in range(SEQS_PER_TILE):
            seq_ref[1:CARRY, :] = pre_ref[r]
            seq_ref[CARRY:CARRY + DEC_SEQ, :] = u[r * DEC_SEQ:(r + 1) * DEC_SEQ, :]
            _window_pool(seq_ref, DEC_SEQ, pos, pooled_ref, r * DEC_SEQ)
            caches_ref[r] = seq_ref[DEC_SEQ:CARRY + DEC_SEQ, :]

    mixed = [jnp.dot(pooled_ref[:, _group_cols(gi)].astype(BF16), wg_ref[gi],
                     preferred_element_type=F32) for gi in range(len(POOL_WINDOWS))]
    mixed = jnp.concatenate(mixed, axis=-1) * sc_ref[...]
    h = jnp.dot(mixed.astype(BF16), wout_ref[...], preferred_element_type=F32)
    _residual_norm_route(x, h, lg_ref, lb_ref, rw_ref, rb_ref, x1_ref, te_ref, gt_ref)


def _pool_mixer(x, prefix, win, wgrp, scale, wout, ln_g, ln_b, router_w, router_b):
    n_prompt_tiles = TP // TM
    tiles_per_seq = SEQ // TM
    return pl.pallas_call(
        _pool_kernel,
        grid=(T // TM,),
        in_specs=[
            _row_spec(D),
            pl.BlockSpec((SEQS_PER_TILE, POOL_STATE, D),
                         lambda i: (jnp.maximum(i - n_prompt_tiles, 0), 0, 0)),
            _const_spec((D, D)), _const_spec((len(POOL_WINDOWS), POOL_GC, POOL_GC)),
            _const_spec((1, D)), _const_spec((D, D)),
            _const_spec((1, D)), _const_spec((1, D)),
            _const_spec((D, N_EXPERTS)), _const_spec((1, N_EXPERTS)),
        ],
        out_specs=_EPILOGUE_OUT_SPECS + [
            pl.BlockSpec((1, CARRY, D),
                         lambda i: (jnp.minimum(i, n_prompt_tiles - 1) // tiles_per_seq, 0, 0)),
            pl.BlockSpec((SEQS_PER_TILE, CARRY, D),
                         lambda i: (jnp.maximum(i - n_prompt_tiles, 0), 0, 0)),
        ],
        out_shape=_EPILOGUE_OUT_SHAPE + [
            jax.ShapeDtypeStruct((BATCH, CARRY, D), F32),
            jax.ShapeDtypeStruct((DEC_BATCH, CARRY, D), F32),
        ],
        scratch_shapes=[
            pltpu.VMEM((CARRY + TM, D), F32),
            pltpu.VMEM((CARRY + DEC_SEQ, D), F32),
            pltpu.VMEM((TM, D), F32),
        ],
        compiler_params=_cparams(("arbitrary",)),
        name="pool_norm_route",
    )(x, prefix, win, wgrp, scale, wout, ln_g, ln_b, router_w, router_b)


def _lane_cumsum(v):
    lane = lax.broadcasted_iota(I32, v.shape, 1)
    shift = 1
    while shift < LANES:
        v = v + jnp.where(lane >= shift, pltpu.roll(v, shift, axis=1), 0.0)
        shift *= 2
    return v


def _positions_kernel(te_ref, lpos_ref, tile_ref, blk_ref, stat_ref, cnt_ref, base_ref):
    phase = pl.program_id(0)
    i = pl.program_id(1)
    te = te_ref[...]
    lane = lax.broadcasted_iota(I32, (TM, LANES), 1)
    hits = [lane == te[:, j:j + 1] for j in range(TOP_K)]
    onehot = sum(h.astype(F32) for h in hits)
    tile_cnt = jnp.sum(onehot, axis=0, keepdims=True)

    @pl.when((phase == 0) & (i == 0))
    def _():
        cnt_ref[...] = jnp.zeros_like(cnt_ref)

    @pl.when(phase == 0)
    def _():
        cnt_ref[...] = cnt_ref[...] + tile_cnt

    @pl.when((phase == 1) & (i == 0))
    def _():
        cnt = cnt_ref[...].astype(I32)
        padded = ((cnt + (BM - 1)) & ~(BM - 1)).astype(F32)
        pad_end = _lane_cumsum(padded)
        base_ref[...] = pad_end - padded
        stat_ref[...] = jnp.concatenate([cnt_ref[...], pad_end - padded, pad_end], axis=0)
        lane8 = lax.broadcasted_iota(I32, (NB_PAD, LANES), 1)
        start = (lax.broadcasted_iota(I32, (NB_PAD, LANES), 0) * BM).astype(F32)
        done = jnp.where((lane8 < N_EXPERTS) & (pad_end[0:1, :] <= start), 1.0, 0.0)
        blk = jnp.minimum(jnp.sum(done, axis=-1, keepdims=True), N_EXPERTS - 1.0)
        blk_ref[...] = blk.astype(I32)

    @pl.when(phase == 1)
    def _():
        r = lax.broadcasted_iota(I32, (TM, TM), 0)
        c = lax.broadcasted_iota(I32, (TM, TM), 1)
        tri = jnp.where(r > c, 1.0, 0.0).astype(BF16)
        before = jnp.dot(tri, onehot.astype(BF16), preferred_element_type=F32)
        cnt8 = jnp.broadcast_to(tile_cnt, (SUBLANES, LANES))
        local_start = _lane_cumsum(cnt8) - cnt8
        slot = local_start[0:1, :] + before
        lpos = jnp.zeros((TM, LANES), F32)
        for j in range(TOP_K):
            pj = jnp.sum(jnp.where(hits[j], slot, 0.0), axis=-1, keepdims=True)
            lpos = jnp.where(lane == j, pj, lpos)
        lpos_ref[...] = lpos.astype(I32)
        row = lax.broadcasted_iota(I32, (SUBLANES, LANES), 0)
        tile_ref[0] = jnp.where(row == 0, cnt8, jnp.where(row == 1, base_ref[...], local_start))
        base_ref[...] = base_ref[...] + tile_cnt


def _positions(top_e):
    return pl.pallas_call(
        _positions_kernel,
        grid=(2, T // TM),
        in_specs=[pl.BlockSpec((TM, TOP_K), lambda p, i: (i, 0))],
        out_specs=[
            pl.BlockSpec((TM, LANES), lambda p, i: (i * p, 0)),
            pl.BlockSpec((1, SUBLANES, LANES), lambda p, i: (i * p, 0, 0)),
            pl.BlockSpec((NB_PAD, 1), lambda p, i: (0, 0)),
            pl.BlockSpec((24, LANES), lambda p, i: (0, 0)),
        ],
        out_shape=[
            jax.ShapeDtypeStruct((T, LANES), I32),
            jax.ShapeDtypeStruct((T // TM, SUBLANES, LANES), F32),
            jax.ShapeDtypeStruct((NB_PAD, 1), I32),
            jax.ShapeDtypeStruct((24, LANES), F32),
        ],
        scratch_shapes=[pltpu.VMEM((8, LANES), F32), pltpu.VMEM((8, LANES), F32)],
        compiler_params=_cparams(("arbitrary", "arbitrary")),
        name="moe_positions",
    )(top_e)


N_TILES = T // TM
TILE_ROWS = TM * TOP_K
RUN_SIZES = tuple(TM >> b for b in range(TM.bit_length()))
ZROWS = 256


def _tok_rows(ref, tok, n_tok):
    start = tok * SUBLANES
    if not isinstance(start, int):
        start = pl.multiple_of(start, SUBLANES)
    return ref.at[pl.ds(start, n_tok * SUBLANES)]


def _copy_runs(tile, cnt_ref, goff_ref, lst_ref, make_copy):
    def run(e, carry):
        n = cnt_ref[tile * N_EXPERTS + e]
        g = goff_ref[tile * N_EXPERTS + e]
        l = lst_ref[tile * N_EXPERTS + e]
        for size in RUN_SIZES:
            take = (n & size) != 0

            @pl.when(take)
            def _(g=g, l=l, size=size):
                make_copy(l, g, size).start()

            step = jnp.where(take, size, 0)
            g = g + step
            l = l + step
        return carry

    lax.fori_loop(0, N_EXPERTS, run, 0)


def _dispatch_kernel(cnt_ref, goff_ref, lst_ref, ecnt_ref, estart_ref, x_ref, lpos_ref, xs_hbm,
                     buf0, buf1, zero_ref, sem_zero, sem_run):
    i = pl.program_id(0)
    bufs = (buf0, buf1)

    @pl.when(i == 0)
    def _():
        zero_ref[...] = jnp.zeros_like(zero_ref)

        def zero_copy(tok, n_tok):
            return pltpu.make_async_copy(zero_ref.at[pl.ds(0, n_tok * SUBLANES)],
                                         _tok_rows(xs_hbm, tok, n_tok), sem_zero)

        def fill_expert(e, carry):
            tok = estart_ref[e] + ecnt_ref[e]
            n_pad = (-ecnt_ref[e]) & (BM - 1)
            size = ZROWS
            while size >= 1:
                take = (n_pad & size) != 0

                @pl.when(take)
                def _(tok=tok, size=size):
                    cp = zero_copy(tok, size)
                    cp.start()
                    cp.wait()

                tok = tok + jnp.where(take, size, 0)
                size //= 2
            return carry

        lax.fori_loop(0, N_EXPERTS, fill_expert, 0)

        def fill_unused(c, carry):
            cp = zero_copy(c * ZROWS, ZROWS)
            cp.start()
            cp.wait()
            return carry

        last = N_EXPERTS - 1
        used_rows = estart_ref[last] + ((ecnt_ref[last] + (BM - 1)) & ~(BM - 1))
        lax.fori_loop(used_rows // ZROWS, R_ROWS // ZROWS, fill_unused, 0)

    xb = x_ref[...].astype(BF16)
    pos_t = lpos_ref[...].astype(F32).T

    def wait_runs(slot):
        pltpu.make_async_copy(bufs[slot], _tok_rows(xs_hbm, 0, TILE_ROWS), sem_run.at[slot]).wait()

    def sort_and_send(slot):
        buf = bufs[slot]

        @pl.when(i >= 2)
        def _():
            wait_runs(slot)

        for c in range(TILE_ROWS // TM):
            row = (c * TM + lax.broadcasted_iota(I32, (TM, TM), 0)).astype(F32)
            perm = jnp.zeros((TM, TM), F32)
            for k in range(TOP_K):
                perm = perm + jnp.where(row == pos_t[k:k + 1, :], 1.0, 0.0)
            rows = jnp.dot(perm.astype(BF16), xb, preferred_element_type=F32)
            _store_row_tiles(buf, rows, c * TM)

        _copy_runs(i, cnt_ref, goff_ref, lst_ref,
                   lambda l, g, size: pltpu.make_async_copy(_tok_rows(buf, l, size),
                                                            _tok_rows(xs_hbm, g, size),
                                                            sem_run.at[slot]))

        @pl.when(i == N_TILES - 1)
        def _():
            wait_runs(1 - slot)
            wait_runs(slot)

    for slot in range(2):
        @pl.when((i & 1) == slot)
        def _(slot=slot):
            sort_and_send(slot)


def _dispatch(x1, lpos, tile_cnt, tile_goff, tile_lst, counts, starts):
    return pl.pallas_call(
        _dispatch_kernel,
        grid_spec=pltpu.PrefetchScalarGridSpec(
            num_scalar_prefetch=5,
            grid=(N_TILES,),
            in_specs=[
                pl.BlockSpec((TM, D), lambda i, *_: (i, 0)),
                pl.BlockSpec((TM, LANES), lambda i, *_: (i, 0)),
            ],
            out_specs=pl.BlockSpec(memory_space=pl.ANY),
            scratch_shapes=[
                pltpu.VMEM((TILE_ROWS * SUBLANES, LANES), F32),
                pltpu.VMEM((TILE_ROWS * SUBLANES, LANES), F32),
                pltpu.VMEM((ZROWS * SUBLANES, LANES), F32),
                pltpu.SemaphoreType.DMA(()),
                pltpu.SemaphoreType.DMA((2,)),
            ],
        ),
        out_shape=jax.ShapeDtypeStruct((R_ROWS * SUBLANES, LANES), F32),
        compiler_params=_cparams(("arbitrary",)),
        name="moe_dispatch",
    )(tile_cnt, tile_goff, tile_lst, counts, starts, x1, lpos)


W_CHUNK = 64


def _split_even_odd(x):
    rows, cols = x.shape
    lane = lax.broadcasted_iota(I32, (rows, LANES), 1)
    idx_even = (2 * lane) % LANES
    idx_odd = (2 * lane + 1) % LANES
    low = lane < LANES // 2
    even, odd = [], []
    for c in range(cols // (2 * LANES)):
        a = x[:, (2 * c) * LANES:(2 * c + 1) * LANES]
        b = x[:, (2 * c + 1) * LANES:(2 * c + 2) * LANES]
        even.append(jnp.where(low, jnp.take_along_axis(a, idx_even, axis=1),
                              jnp.take_along_axis(b, idx_even, axis=1)))
        odd.append(jnp.where(low, jnp.take_along_axis(a, idx_odd, axis=1),
                             jnp.take_along_axis(b, idx_odd, axis=1)))
    return jnp.concatenate(even, axis=1), jnp.concatenate(odd, axis=1)


def _expert_kernel(blk_ref, used_ref, xs_ref, wu_ref, wd_ref, bu_ref, bd_ref, ys_ref,
                   wg_s, wl_s, wd_s, bg_s, bl_s):
    i = pl.program_id(0)
    active = i < used_ref[0]
    new_expert = (i == 0) | (blk_ref[i] != blk_ref[jnp.maximum(i - 1, 0)])

    @pl.when(active & new_expert)
    def _():
        def convert(r, carry):
            rows = pl.ds(pl.multiple_of(r * W_CHUNK, W_CHUNK), W_CHUNK)
            even, odd = _split_even_odd(wu_ref[0, 0, rows, :])
            wg_s[rows, :] = even.astype(BF16)
            wl_s[rows, :] = odd.astype(BF16)
            wd_s[rows, :] = wd_ref[0, 0, rows, :].astype(BF16)
            return carry

        lax.fori_loop(0, D // W_CHUNK, convert, 0)
        even, odd = _split_even_odd(jnp.broadcast_to(bu_ref[0, 0], (SUBLANES, 2 * D)))
        bg_s[...] = even
        bl_s[...] = odd

    @pl.when(active)
    def _():
        xb = _load_row_tiles(xs_ref, 0, BM).astype(BF16)
        glu = jnp.dot(xb, wg_s[...], preferred_element_type=F32) + bg_s[0:1, :]
        lin = jnp.dot(xb, wl_s[...], preferred_element_type=F32) + bl_s[0:1, :]
        glu = jnp.minimum(glu, SWIGLU_LIMIT)
        lin = jnp.clip(lin, -SWIGLU_LIMIT, SWIGLU_LIMIT)
        a = glu * jax.nn.sigmoid(SWIGLU_ALPHA * glu) * (lin + 1.0)
        y = jnp.dot(a.astype(BF16), wd_s[...], preferred_element_type=F32) + bd_ref[0, 0]
        _store_row_tiles(ys_ref, y)

    @pl.when(jnp.logical_not(active))
    def _():
        ys_ref[...] = jnp.zeros_like(ys_ref)


def _experts(layer, xs, blk_e, n_used, w_up, w_down, b_up, b_down):
    def row_map(i, blk, used):
        return (jnp.minimum(i, used[0] - 1), 0)

    def w_map(i, blk, used):
        return (layer, blk[jnp.minimum(i, used[0] - 1)], 0, 0)

    return pl.pallas_call(
        _expert_kernel,
        grid_spec=pltpu.PrefetchScalarGridSpec(
            num_scalar_prefetch=2,
            grid=(NB,),
            in_specs=[
                _tile_rows_spec(BM, row_map),
                pl.BlockSpec((1, 1, D, 2 * D), w_map),
                pl.BlockSpec((1, 1, D, D), w_map),
                pl.BlockSpec((1, 1, 1, 2 * D), w_map),
                pl.BlockSpec((1, 1, 1, D), w_map),
            ],
            out_specs=_tile_rows_spec(BM, lambda i, blk, used: (i, 0)),
            scratch_shapes=[
                pltpu.VMEM((D, D), BF16), pltpu.VMEM((D, D), BF16), pltpu.VMEM((D, D), BF16),
                pltpu.VMEM((SUBLANES, D), F32), pltpu.VMEM((SUBLANES, D), F32),
            ],
        ),
        out_shape=jax.ShapeDtypeStruct((R_ROWS * SUBLANES, LANES), F32),
        compiler_params=_cparams(("arbitrary",)),
        name="moe_experts",
    )(blk_e, n_used, xs, w_up, w_down, b_up, b_down)


N_TILES_PROMPT = TP // TM


def _combine_kernel(split_out, cnt_ref, goff_ref, lst_ref, x_ref, lpos_ref, gt_ref, lg_ref, lb_ref,
                    ys_hbm, *refs):
    if split_out:
        op_ref, os_ref, buf0, buf1, sem_run = refs
    else:
        o_ref, buf0, buf1, sem_run = refs
    bufs = (buf0, buf1)
    s = pl.program_id(0)

    def start_runs(slot):
        _copy_runs(s, cnt_ref, goff_ref, lst_ref,
                   lambda l, g, size: pltpu.make_async_copy(_tok_rows(ys_hbm, g, size),
                                                            _tok_rows(bufs[slot], l, size),
                                                            sem_run.at[slot]))

    def finish(slot):
        buf = bufs[slot]
        pltpu.make_async_copy(_tok_rows(ys_hbm, 0, TILE_ROWS), buf, sem_run.at[slot]).wait()
        gt = gt_ref[...]
        pos = lpos_ref[...]
        m = jnp.zeros((TM, D), F32)
        for c in range(TILE_ROWS // TM):
            col = c * TM + lax.broadcasted_iota(I32, (TM, TM), 1)
            weight = jnp.zeros((TM, TM), F32)
            for k in range(TOP_K):
                weight = weight + jnp.where(col == pos[:, k:k + 1], gt[:, k:k + 1], 0.0)
            rows = _load_row_tiles(buf, c * TM, TM).astype(BF16)
            m = m + jnp.dot(weight.astype(BF16), rows, preferred_element_type=F32)
        y = _layer_norm(ALPHA * x_ref[...] + m, lg_ref[...], lb_ref[...])
        if split_out:
            @pl.when(s - 1 < N_TILES_PROMPT)
            def _():
                op_ref[...] = y

            @pl.when(s - 1 >= N_TILES_PROMPT)
            def _():
                os_ref[...] = y
        else:
            o_ref[...] = y

    for slot in range(2):
        @pl.when((s < N_TILES) & ((s & 1) == slot))
        def _(slot=slot):
            start_runs(slot)

    for slot in range(2):
        @pl.when((s > 0) & (((s - 1) & 1) == slot))
        def _(slot=slot):
            finish(slot)


def _combine(x1, lpos, gates, ln_g, ln_b, ys, tile_cnt, tile_goff, tile_lst, split_out):
    def tile_map(s, *_):
        return (jnp.maximum(s - 1, 0), 0)

    if split_out:
        out_specs = [
            pl.BlockSpec((TM, D), lambda s, *_: (jnp.clip(s - 1, 0, N_TILES_PROMPT - 1), 0)),
            pl.BlockSpec((TM, D), lambda s, *_: (jnp.maximum(s - 1 - N_TILES_PROMPT, 0), 0)),
        ]
        out_shape = [jax.ShapeDtypeStruct((TP, D), F32), jax.ShapeDtypeStruct((TS, D), F32)]
    else:
        out_specs = pl.BlockSpec((TM, D), tile_map)
        out_shape = jax.ShapeDtypeStruct((T, D), F32)
    return pl.pallas_call(
        functools.partial(_combine_kernel, split_out),
        grid_spec=pltpu.PrefetchScalarGridSpec(
            num_scalar_prefetch=3,
            grid=(N_TILES + 1,),
            in_specs=[
                pl.BlockSpec((TM, D), tile_map),
                pl.BlockSpec((TM, LANES), tile_map),
                pl.BlockSpec((TM, TOP_K), tile_map),
                pl.BlockSpec((1, D), lambda s, *_: (0, 0)),
                pl.BlockSpec((1, D), lambda s, *_: (0, 0)),
                pl.BlockSpec(memory_space=pl.ANY),
            ],
            out_specs=out_specs,
            scratch_shapes=[
                pltpu.VMEM((TILE_ROWS * SUBLANES, LANES), F32),
                pltpu.VMEM((TILE_ROWS * SUBLANES, LANES), F32),
                pltpu.SemaphoreType.DMA((2,)),
            ],
        ),
        out_shape=out_shape,
        compiler_params=_cparams(("arbitrary",)),
        name="moe_combine_norm",
    )(tile_cnt, tile_goff, tile_lst, x1, lpos, gates, ln_g, ln_b, ys)


def _moe_layer(layer, x1, top_e, gates, w_up, b_up, w_down, b_down, ln_g, ln_b, split_out):
    lpos, tile_tab, blk_e, stat = _positions(top_e)

    def per_tile(row):
        return tile_tab[:, row, :N_EXPERTS].astype(I32).reshape(N_TILES * N_EXPERTS)

    tile_cnt, tile_goff, tile_lst = per_tile(0), per_tile(1), per_tile(2)
    counts = stat[0, :N_EXPERTS].astype(I32)
    starts = stat[8, :N_EXPERTS].astype(I32)
    n_used = (stat[16, N_EXPERTS - 1:N_EXPERTS].astype(I32)) // BM
    xs = _dispatch(x1, lpos, tile_cnt, tile_goff, tile_lst, counts, starts)
    ys = _experts(layer, xs, blk_e.reshape(NB_PAD), n_used, w_up, w_down, b_up[:, :, None, :],
                  b_down[:, :, None, :])
    return _combine(x1, lpos, gates, ln_g, ln_b, ys, tile_cnt, tile_goff, tile_lst, split_out)


def _rope_tables():
    half = DK // 2
    inv = 1.0 / (ROPE_BASE ** jnp.linspace(0.0, 1.0, half, dtype=F32))
    pos = jnp.concatenate([jnp.arange(SEQ), PAST_LEN + (jnp.arange(TM_PROJ) % DEC_SEQ)]).astype(F32)
    ang = pos[:, None] * inv[None, :]
    return jnp.cos(ang), jnp.sin(ang)


def kernel(x_prompt, x_sample, state_ret, cache_pool, ret_w_in, ret_w_out, pool_w_in, pool_w_grp,
           pool_scale, pool_w_out, ln1_g, ln1_b, ln2_g, ln2_b, router_w, router_b, w_up, b_up,
           w_down, b_down):
    xp = x_prompt.reshape(TP, D)
    xs = x_sample.reshape(TS, D)

    def vec(a):
        return a.reshape(1, -1)

    cos_t, sin_t = _rope_tables()
    p_all = _ret_project(xp, xs, ret_w_in[0].astype(BF16), cos_t, sin_t)
    o_prompt, s_prompt = _ret_core(p_all, BATCH, SUPER, SEQ // SUPER, 0)
    o_sample, s_sample = _ret_core(p_all, DEC_BATCH, DEC_SEQ, 1, TP // DEC_SEQ,
                                   s0=state_ret[0].reshape(DEC_BATCH * HEADS, DK, DV))
    x1, top_e, gates = _ret_out(o_prompt, o_sample, p_all, xp, xs, ret_w_out[0].astype(BF16),
                                vec(ln1_g[0]), vec(ln1_b[0]), router_w[0], vec(router_b[0]))
    x = _moe_layer(0, x1, top_e, gates, w_up, b_up, w_down, b_down, vec(ln2_g[0]), vec(ln2_b[0]),
                   split_out=False)

    x1, top_e, gates, cache_p, cache_s = _pool_mixer(
        x, cache_pool[0], pool_w_in[0].astype(BF16), pool_w_grp[0].astype(BF16),
        vec(pool_scale[0]), pool_w_out[0].astype(BF16), vec(ln1_g[1]), vec(ln1_b[1]), router_w[1],
        vec(router_b[1]))
    yp, ys = _moe_layer(1, x1, top_e, gates, w_up, b_up, w_down, b_down, vec(ln2_g[1]),
                        vec(ln2_b[1]), split_out=True)

    y_prompt = yp.reshape(BATCH, SEQ, D)
    y_sample = ys.reshape(DEC_BATCH, DEC_SEQ, D)
    state_ret_prompt = s_prompt.reshape(1, BATCH, HEADS, DK, DV)
    state_ret_sample = s_sample.reshape(1, DEC_BATCH, HEADS, DK, DV)
    cache_pool_prompt = cache_p[None, :, 1:, :]
    cache_pool_sample = cache_s[None, :, 1:, :]
    return (y_prompt, y_sample, state_ret_prompt, state_ret_sample, cache_pool_prompt,
            cache_pool_sample)
```

```python
import functools

import jax
import jax.numpy as jnp
from jax import lax
from jax.experimental import pallas as pl
from jax.experimental.pallas import tpu as pltpu

F32 = jnp.float32
BF16 = jnp.bfloat16
I32 = jnp.int32

D = 1024
BATCH = 16
SEQ = 2048
DEC_BATCH = 32
DEC_SEQ = 64
PAST_LEN = 4096
TP = BATCH * SEQ
TS = DEC_BATCH * DEC_SEQ
T = TP + TS

HEADS = 4
DK = 256
DV = 512
RET_QK = HEADS * DK
RET_V = HEADS * DV
RET_IN = 2 * RET_QK + 2 * RET_V
ROPE_BASE = 10000.0
RMS_EPS = 1e-6
LN_EPS = 1e-5
ALPHA = 4.0 ** 0.25

POOL_WINDOWS = (2, 4, 8, 16)
POOL_GC = D // 4
POOL_STATE = 15
CARRY = 16

N_EXPERTS = 32
TOP_K = 4
SWIGLU_LIMIT = 7.0
SWIGLU_ALPHA = 1.702

LANES = 128
SUBLANES = 8
assert D == SUBLANES * LANES
SUPER = 256
CHUNK = 64

TM_PROJ = 1024
TM = 512
BM = 512
NB = -(-(T * TOP_K) // BM) + N_EXPERTS
NB_PAD = -(-NB // 8) * 8
R_ROWS = NB * BM

VMEM_LIMIT = 56 * 1024 * 1024


def _cparams(sem, vmem=VMEM_LIMIT):
    return pltpu.CompilerParams(dimension_semantics=sem, vmem_limit_bytes=vmem)


def _proj_kernel(xp_ref, xs_ref, w_ref, cos_ref, sin_ref, o_ref, xb_ref):
    n = pl.program_id(1)

    @pl.when(n == 0)
    def _():
        x = jnp.where(pl.program_id(0) < TP // TM_PROJ, xp_ref[...], xs_ref[...])
        xb_ref[...] = x.astype(BF16)

    xb = xb_ref[...]

    @pl.when(n < 2)
    def _():
        scale = jnp.where(n == 1, DK ** -0.5, 1.0).astype(F32)
        cos = cos_ref[...] * scale
        sin = sin_ref[...] * scale
        half = DK // 2
        for h in range(HEADS):
            p = jnp.dot(xb, w_ref[:, h * DK:(h + 1) * DK], preferred_element_type=F32)
            t1 = p[:, :half]
            t2 = p[:, half:]
            o_ref[:, h * DK:h * DK + half] = (t1 * cos - t2 * sin).astype(BF16)
            o_ref[:, h * DK + half:(h + 1) * DK] = (t1 * sin + t2 * cos).astype(BF16)

    @pl.when(n >= 2)
    def _():
        for h in range(4):
            p = jnp.dot(xb, w_ref[:, h * 256:(h + 1) * 256], preferred_element_type=F32)
            o_ref[:, h * 256:(h + 1) * 256] = p.astype(BF16)


def _ret_project(x_prompt, x_sample, w_bf16, cos_t, sin_t):
    n_prompt_tiles = TP // TM_PROJ
    tiles_per_seq = SEQ // TM_PROJ

    def tab_map(i, n):
        return (jnp.where(i < n_prompt_tiles, i % tiles_per_seq, tiles_per_seq), 0)

    return pl.pallas_call(
        _proj_kernel,
        grid=(T // TM_PROJ, RET_IN // 1024),
        in_specs=[
            pl.BlockSpec((TM_PROJ, D), lambda i, n: (jnp.minimum(i, n_prompt_tiles - 1), 0)),
            pl.BlockSpec((TM_PROJ, D), lambda i, n: (jnp.maximum(i - n_prompt_tiles, 0), 0)),
            pl.BlockSpec((D, 1024), lambda i, n: (0, n)),
            pl.BlockSpec((TM_PROJ, LANES), tab_map),
            pl.BlockSpec((TM_PROJ, LANES), tab_map),
        ],
        out_specs=pl.BlockSpec((TM_PROJ, 1024), lambda i, n: (i, n)),
        out_shape=jax.ShapeDtypeStruct((T, RET_IN), BF16),
        scratch_shapes=[pltpu.VMEM((TM_PROJ, D), BF16)],
        compiler_params=_cparams(("arbitrary", "arbitrary")),
        name="ret_project",
    )(x_prompt, x_sample, w_bf16, cos_t, sin_t)


def _ret_core_kernel(has_init, n_steps, *refs):
    if has_init:
        (q_ref, k_ref, v_ref, mask_ref, qd_ref, kd_ref, bd_ref, s0_ref,
         o_ref, sout_ref, s_ref) = refs
    else:
        (q_ref, k_ref, v_ref, mask_ref, qd_ref, kd_ref, bd_ref,
         o_ref, sout_ref, s_ref) = refs
    c = pl.program_id(1)

    @pl.when(c == 0)
    def _():
        if has_init:
            s_ref[...] = s0_ref[...].astype(F32)
        else:
            s_ref[...] = jnp.zeros_like(s_ref)

    for h in range(HEADS):
        q = q_ref[:, h * DK:(h + 1) * DK]
        k = k_ref[:, h * DK:(h + 1) * DK]
        v = v_ref[:, h * DV:(h + 1) * DV]
        s_prev = s_ref[h]
        scores = lax.dot_general(q, k, (((1,), (1,)), ((), ())), preferred_element_type=F32)
        scores = scores * mask_ref[h]
        qd = (q.astype(F32) * qd_ref[h]).astype(BF16)
        o = (jnp.dot(scores.astype(BF16), v, preferred_element_type=F32)
             + jnp.dot(qd, s_prev.astype(BF16), preferred_element_type=F32))
        kd = (k.astype(F32) * kd_ref[h]).astype(BF16)
        s_new = s_prev * bd_ref[h] + lax.dot_general(kd, v, (((0,), (0,)), ((), ())),
                                                     preferred_element_type=F32)
        s_ref[h] = s_new
        o = o * lax.rsqrt(jnp.mean(o * o, axis=-1, keepdims=True) + RMS_EPS)
        o_ref[:, h * DV:(h + 1) * DV] = o.astype(BF16)

    @pl.when(c == n_steps - 1)
    def _():
        sout_ref[...] = s_ref[...]


def _decay_tables(rows):
    lg = jnp.log1p(-jnp.exp2(-5.0 - jnp.arange(HEADS, dtype=F32)))
    idx = jnp.arange(rows, dtype=F32)
    ch = jnp.arange(rows) // CHUNK
    diff = idx[:, None] - idx[None, :]
    same = ch[:, None] == ch[None, :]
    earlier = ch[None, :] < ch[:, None]
    expo = jnp.where(same, jnp.abs(diff), diff)
    w = jnp.exp(lg[:, None, None] * expo[None])
    mask = jnp.where((same | earlier)[None], w, 0.0).astype(F32)
    q_dec = jnp.exp(lg[:, None] * (idx + 1.0))[:, :, None]
    k_dec = jnp.exp(lg[:, None] * (rows - 1.0 - idx))[:, :, None]
    blk = jnp.broadcast_to(jnp.exp(lg * rows)[:, None, None], (HEADS, 1, DV))
    return mask, q_dec, k_dec, blk.astype(F32)


def _ret_core(p_all, n_seq, rows, n_steps, row_block0, s0=None):
    mask, q_dec, k_dec, blk = _decay_tables(rows)
    has_init = s0 is not None

    def rb(b, c):
        return row_block0 + b * n_steps + c

    def whole(shape):
        nd = len(shape)
        return pl.BlockSpec(shape, lambda b, c: (0,) * nd)

    in_specs = [
        pl.BlockSpec((rows, RET_QK), lambda b, c: (rb(b, c), 0)),
        pl.BlockSpec((rows, RET_QK), lambda b, c: (rb(b, c), 1)),
        pl.BlockSpec((rows, RET_V), lambda b, c: (rb(b, c), 1)),
        whole((HEADS, rows, rows)), whole((HEADS, rows, 1)), whole((HEADS, rows, 1)),
        whole((HEADS, 1, DV)),
    ]
    args = [p_all, p_all, p_all, mask, q_dec, k_dec, blk]
    if has_init:
        in_specs.append(pl.BlockSpec((HEADS, DK, DV), lambda b, c: (b, 0, 0)))
        args.append(s0)
    return pl.pallas_call(
        functools.partial(_ret_core_kernel, has_init, n_steps),
        grid=(n_seq, n_steps),
        in_specs=in_specs,
        out_specs=[
            pl.BlockSpec((rows, RET_V), lambda b, c: (b * n_steps + c, 0)),
            pl.BlockSpec((HEADS, DK, DV), lambda b, c: (b, 0, 0)),
        ],
        out_shape=[
            jax.ShapeDtypeStruct((n_seq * n_steps * rows, RET_V), BF16),
            jax.ShapeDtypeStruct((n_seq * HEADS, DK, DV), F32),
        ],
        scratch_shapes=[pltpu.VMEM((HEADS, DK, DV), F32)],
        compiler_params=_cparams(("arbitrary", "arbitrary")),
        name="ret_core_sample" if has_init else "ret_core_prompt",
    )(*args)


def _layer_norm(y, g, b):
    mu = jnp.mean(y, axis=-1, keepdims=True)
    yc = y - mu
    var = jnp.mean(yc * yc, axis=-1, keepdims=True)
    return yc * lax.rsqrt(var + LN_EPS) * g + b


def _split_bf16(a):
    hi = a.astype(BF16)
    lo = (a - hi.astype(F32)).astype(BF16)
    return hi, lo


def _route(x1, rw_ref, rb_ref, te_ref, gt_ref):
    rows = x1.shape[0]
    xh, xl = _split_bf16(x1)
    wh, wl = _split_bf16(rw_ref[...])
    logits = (jnp.dot(xh, wh, preferred_element_type=F32)
              + jnp.dot(xl, wh, preferred_element_type=F32)
              + jnp.dot(xh, wl, preferred_element_type=F32)) + rb_ref[...]
    lane = lax.broadcasted_iota(I32, (rows, N_EXPERTS), 1)
    lane_k = lax.broadcasted_iota(I32, (rows, TOP_K), 1)
    te = jnp.zeros((rows, TOP_K), I32)
    tv = jnp.zeros((rows, TOP_K), F32)
    cur = logits
    for j in range(TOP_K):
        m = jnp.max(cur, axis=-1, keepdims=True)
        idx = jnp.min(jnp.where(cur == m, lane, N_EXPERTS), axis=-1, keepdims=True)
        te = jnp.where(lane_k == j, idx, te)
        tv = jnp.where(lane_k == j, m, tv)
        cur = jnp.where(lane == idx, -jnp.inf, cur)
    ex = jnp.exp(tv - jnp.max(tv, axis=-1, keepdims=True))
    gt_ref[...] = ex / jnp.sum(ex, axis=-1, keepdims=True)
    te_ref[...] = te


def _load_row_tiles(ref, row0, rows, tok_stride=1):
    return jnp.concatenate(
        [ref[pl.ds(row0 * SUBLANES + s, rows, stride=SUBLANES * tok_stride), :]
         for s in range(SUBLANES)], axis=1)


def _store_row_tiles(ref, val, row0=0):
    rows = val.shape[0]
    for s in range(SUBLANES):
        ref[pl.ds(row0 * SUBLANES + s, rows, stride=SUBLANES), :] = val[:, s * LANES:(s + 1) * LANES]


def _residual_norm_route(x, h, lg_ref, lb_ref, rw_ref, rb_ref, x1_ref, te_ref, gt_ref):
    x1 = _layer_norm(ALPHA * x + h, lg_ref[...], lb_ref[...])
    x1_ref[...] = x1
    _route(x1, rw_ref, rb_ref, te_ref, gt_ref)


def _ret_out_kernel(op_ref, os_ref, g_ref, xp_ref, xs_ref, w_ref, lg_ref, lb_ref, rw_ref, rb_ref,
                    x1_ref, te_ref, gt_ref):
    is_prompt = pl.program_id(0) < TP // TM
    g = g_ref[...].astype(F32)
    o = jnp.where(is_prompt, op_ref[...], os_ref[...])
    a = (g * jax.nn.sigmoid(g) * o.astype(F32)).astype(BF16)
    h = jnp.dot(a, w_ref[...], preferred_element_type=F32)
    x = jnp.where(is_prompt, xp_ref[...], xs_ref[...])
    _residual_norm_route(x, h, lg_ref, lb_ref, rw_ref, rb_ref, x1_ref, te_ref, gt_ref)


def _row_spec(cols, col_block=0):
    return pl.BlockSpec((TM, cols), lambda i: (i, col_block))


def _const_spec(shape):
    nd = len(shape)
    return pl.BlockSpec(shape, lambda i: (0,) * nd)


def _tile_rows_spec(rows, index_map):
    return pl.BlockSpec((rows * SUBLANES, LANES), index_map)


_EPILOGUE_OUT_SPECS = [_row_spec(D), _row_spec(TOP_K), _row_spec(TOP_K)]
_EPILOGUE_OUT_SHAPE = [
    jax.ShapeDtypeStruct((T, D), F32),
    jax.ShapeDtypeStruct((T, TOP_K), I32),
    jax.ShapeDtypeStruct((T, TOP_K), F32),
]


def _ret_out(o_prompt, o_sample, p_all, x_prompt, x_sample, w_out_bf16, ln_g, ln_b, router_w,
             router_b):
    n_prompt_tiles = TP // TM

    def prompt_spec(cols):
        return pl.BlockSpec((TM, cols), lambda i: (jnp.minimum(i, n_prompt_tiles - 1), 0))

    def sample_spec(cols):
        return pl.BlockSpec((TM, cols), lambda i: (jnp.maximum(i - n_prompt_tiles, 0), 0))

    return pl.pallas_call(
        _ret_out_kernel,
        grid=(T // TM,),
        in_specs=[
            prompt_spec(RET_V), sample_spec(RET_V),
            _row_spec(RET_V, col_block=2),
            prompt_spec(D), sample_spec(D),
            _const_spec((RET_V, D)),
            _const_spec((1, D)), _const_spec((1, D)),
            _const_spec((D, N_EXPERTS)), _const_spec((1, N_EXPERTS)),
        ],
        out_specs=_EPILOGUE_OUT_SPECS,
        out_shape=_EPILOGUE_OUT_SHAPE,
        compiler_params=_cparams(("arbitrary",)),
        name="ret_out_norm_route",
    )(o_prompt, o_sample, p_all, x_prompt, x_sample, w_out_bf16, ln_g, ln_b, router_w, router_b)


SEQS_PER_TILE = TM // DEC_SEQ


def _group_cols(gi):
    return slice(gi * POOL_GC, (gi + 1) * POOL_GC)


def _band_tables():
    i = jnp.arange(TM)[:, None]
    j = jnp.arange(TM)[None, :]
    hi = jnp.arange(CARRY)[:, None]
    hr = jnp.arange(CARRY)[None, :]
    band_p, band_s, hist = [], [], []
    for w in POOL_WINDOWS:
        inside = (j <= i) & (j > i - w)
        band_p.append(inside)
        band_s.append(inside & (i // DEC_SEQ == j // DEC_SEQ))
        hist.append(hr >= hi + CARRY + 1 - w)
    as_bf16 = lambda ms: jnp.stack(ms).astype(BF16)
    return as_bf16(band_p), as_bf16(band_s), as_bf16(hist)


def _pool_kernel(x_ref, pre_ref, bandp_ref, bands_ref, hband_ref, win_ref, wg_ref, sc_ref, wout_ref,
                 lg_ref, lb_ref, rw_ref, rb_ref,
                 x1_ref, te_ref, gt_ref, cachep_ref, caches_ref, carry_ref, seq_ref, pooled_ref):
    i = pl.program_id(0)
    n_prompt_tiles = TP // TM
    tiles_per_seq = SEQ // TM
    is_prompt = i < n_prompt_tiles
    x = x_ref[...]
    u = jnp.dot(x.astype(BF16), win_ref[...], preferred_element_type=F32)
    u_hi, u_lo = _split_bf16(u)

    sums = []
    for gi in range(len(POOL_WINDOWS)):
        cols = _group_cols(gi)
        band = jnp.where(is_prompt, bandp_ref[gi], bands_ref[gi])
        sums.append(jnp.dot(band, u_hi[:, cols], preferred_element_type=F32)
                    + jnp.dot(band, u_lo[:, cols], preferred_element_type=F32))

    def history_sums(hist, gi):
        cols = _group_cols(gi)
        h_hi, h_lo = _split_bf16(hist[:, cols])
        return (jnp.dot(hband_ref[gi], h_hi, preferred_element_type=F32)
                + jnp.dot(hband_ref[gi], h_lo, preferred_element_type=F32))

    @pl.when(is_prompt)
    def _():
        j = i % tiles_per_seq

        @pl.when(j == 0)
        def _():
            carry_ref[...] = jnp.zeros_like(carry_ref)

        pos = (j * TM + lax.broadcasted_iota(I32, (TM, 1), 0)).astype(F32)
        hist = carry_ref[...]
        for gi, w in enumerate(POOL_WINDOWS):
            cols = _group_cols(gi)
            cnt = jnp.minimum(pos + 1.0, float(w))
            pooled_ref[:, cols] = sums[gi] / cnt - u[:, cols]
            head = sums[gi][0:CARRY, :] + history_sums(hist, gi)
            pooled_ref[0:CARRY, cols] = head / cnt[0:CARRY, :] - u[0:CARRY, cols]

        @pl.when(j == tiles_per_seq - 1)
        def _():
            cachep_ref[0] = u[TM - CARRY:TM, :]

        carry_ref[...] = u[TM - CARRY:TM, :]

    @pl.when(jnp.logical_not(is_prompt))
    def _():
        for gi, w in enumerate(POOL_WINDOWS):
            cols = _group_cols(gi)
            pooled_ref[:, cols] = sums[gi] * (1.0 / w) - u[:, cols]
        seq_ref[0:1, :] = jnp.zeros((1, D), F32)
        for r in range(SEQS_PER_TILE):
            r0 = r * DEC_SEQ
            seq_ref[1:CARRY, :] = pre_ref[r]
            hist = seq_ref[...]
            for gi, w in enumerate(POOL_WINDOWS):
                cols = _group_cols(gi)
                head = sums[gi][r0:r0 + CARRY, :] + history_sums(hist, gi)
                pooled_ref[r0:r0 + CARRY, cols] = head * (1.0 / w) - u[r0:r0 + CARRY, cols]
            caches_ref[r] = u[r0 + DEC_SEQ - CARRY:r0 + DEC_SEQ, :]

    mixed = [jnp.dot(pooled_ref[:, _group_cols(gi)].astype(BF16), wg_ref[gi],
                     preferred_element_type=F32) for gi in range(len(POOL_WINDOWS))]
    mixed = jnp.concatenate(mixed, axis=-1) * sc_ref[...]
    h = jnp.dot(mixed.astype(BF16), wout_ref[...], preferred_element_type=F32)
    _residual_norm_route(x, h, lg_ref, lb_ref, rw_ref, rb_ref, x1_ref, te_ref, gt_ref)


def _pool_mixer(x, prefix, win, wgrp, scale, wout, ln_g, ln_b, router_w, router_b):
    n_prompt_tiles = TP // TM
    tiles_per_seq = SEQ // TM
    n_win = len(POOL_WINDOWS)
    band_p, band_s, hist = _band_tables()
    return pl.pallas_call(
        _pool_kernel,
        grid=(T // TM,),
        in_specs=[
            _row_spec(D),
            pl.BlockSpec((SEQS_PER_TILE, POOL_STATE, D),
                         lambda i: (jnp.maximum(i - n_prompt_tiles, 0), 0, 0)),
            _const_spec((n_win, TM, TM)), _const_spec((n_win, TM, TM)),
            _const_spec((n_win, CARRY, CARRY)),
            _const_spec((D, D)), _const_spec((n_win, POOL_GC, POOL_GC)),
            _const_spec((1, D)), _const_spec((D, D)),
            _const_spec((1, D)), _const_spec((1, D)),
            _const_spec((D, N_EXPERTS)), _const_spec((1, N_EXPERTS)),
        ],
        out_specs=_EPILOGUE_OUT_SPECS + [
            pl.BlockSpec((1, CARRY, D),
                         lambda i: (jnp.minimum(i, n_prompt_tiles - 1) // tiles_per_seq, 0, 0)),
            pl.BlockSpec((SEQS_PER_TILE, CARRY, D),
                         lambda i: (jnp.maximum(i - n_prompt_tiles, 0), 0, 0)),
        ],
        out_shape=_EPILOGUE_OUT_SHAPE + [
            jax.ShapeDtypeStruct((BATCH, CARRY, D), F32),
            jax.ShapeDtypeStruct((DEC_BATCH, CARRY, D), F32),
        ],
        scratch_shapes=[
            pltpu.VMEM((CARRY, D), F32),
            pltpu.VMEM((CARRY, D), F32),
            pltpu.VMEM((TM, D), F32),
        ],
        compiler_params=_cparams(("arbitrary",)),
        name="pool_norm_route",
    )(x, prefix, band_p, band_s, hist, win, wgrp, scale, wout, ln_g, ln_b, router_w, router_b)


def _lane_cumsum(v):
    lane = lax.broadcasted_iota(I32, v.shape, 1)
    shift = 1
    while shift < LANES:
        v = v + jnp.where(lane >= shift, pltpu.roll(v, shift, axis=1), 0.0)
        shift *= 2
    return v


def _positions_kernel(te_ref, lpos_ref, tile_ref, blk_ref, stat_ref, cnt_ref, base_ref):
    phase = pl.program_id(0)
    i = pl.program_id(1)
    te = te_ref[...]
    lane = lax.broadcasted_iota(I32, (TM, LANES), 1)
    hits = [lane == te[:, j:j + 1] for j in range(TOP_K)]
    onehot = sum(h.astype(F32) for h in hits)
    tile_cnt = jnp.sum(onehot, axis=0, keepdims=True)

    @pl.when((phase == 0) & (i == 0))
    def _():
        cnt_ref[...] = jnp.zeros_like(cnt_ref)

    @pl.when(phase == 0)
    def _():
        cnt_ref[...] = cnt_ref[...] + tile_cnt

    @pl.when((phase == 1) & (i == 0))
    def _():
        cnt = cnt_ref[...].astype(I32)
        padded = ((cnt + (BM - 1)) & ~(BM - 1)).astype(F32)
        pad_end = _lane_cumsum(padded)
        base_ref[...] = pad_end - padded
        stat_ref[...] = jnp.concatenate([cnt_ref[...], pad_end - padded, pad_end], axis=0)
        lane8 = lax.broadcasted_iota(I32, (NB_PAD, LANES), 1)
        start = (lax.broadcasted_iota(I32, (NB_PAD, LANES), 0) * BM).astype(F32)
        done = jnp.where((lane8 < N_EXPERTS) & (pad_end[0:1, :] <= start), 1.0, 0.0)
        blk = jnp.minimum(jnp.sum(done, axis=-1, keepdims=True), N_EXPERTS - 1.0)
        blk_ref[...] = blk.astype(I32)

    @pl.when(phase == 1)
    def _():
        r = lax.broadcasted_iota(I32, (TM, TM), 0)
        c = lax.broadcasted_iota(I32, (TM, TM), 1)
        tri = jnp.where(r > c, 1.0, 0.0).astype(BF16)
        before = jnp.dot(tri, onehot.astype(BF16), preferred_element_type=F32)
        cnt8 = jnp.broadcast_to(tile_cnt, (SUBLANES, LANES))
        local_start = _lane_cumsum(cnt8) - cnt8
        slot = local_start[0:1, :] + before
        lpos = jnp.zeros((TM, LANES), F32)
        for j in range(TOP_K):
            pj = jnp.sum(jnp.where(hits[j], slot, 0.0), axis=-1, keepdims=True)
            lpos = jnp.where(lane == j, pj, lpos)
        lpos_ref[...] = lpos.astype(I32)
        row = lax.broadcasted_iota(I32, (SUBLANES, LANES), 0)
        tile_ref[0] = jnp.where(row == 0, cnt8, jnp.where(row == 1, base_ref[...], local_start))
        base_ref[...] = base_ref[...] + tile_cnt


def _positions(top_e):
    return pl.pallas_call(
        _positions_kernel,
        grid=(2, T // TM),
        in_specs=[pl.BlockSpec((TM, TOP_K), lambda p, i: (i, 0))],
        out_specs=[
            pl.BlockSpec((TM, LANES), lambda p, i: (i * p, 0)),
            pl.BlockSpec((1, SUBLANES, LANES), lambda p, i: (i * p, 0, 0)),
            pl.BlockSpec((NB_PAD, 1), lambda p, i: (0, 0)),
            pl.BlockSpec((24, LANES), lambda p, i: (0, 0)),
        ],
        out_shape=[
            jax.ShapeDtypeStruct((T, LANES), I32),
            jax.ShapeDtypeStruct((T // TM, SUBLANES, LANES), F32),
            jax.ShapeDtypeStruct((NB_PAD, 1), I32),
            jax.ShapeDtypeStruct((24, LANES), F32),
        ],
        scratch_shapes=[pltpu.VMEM((8, LANES), F32), pltpu.VMEM((8, LANES), F32)],
        compiler_params=_cparams(("arbitrary", "arbitrary")),
        name="moe_positions",
    )(top_e)


N_TILES = T // TM
TILE_ROWS = TM * TOP_K
RUN_SIZES = tuple(TM >> b for b in range(TM.bit_length()))
ZROWS = 256


def _tok_rows(ref, tok, n_tok):
    start = tok * SUBLANES
    if not isinstance(start, int):
        start = pl.multiple_of(start, SUBLANES)
    return ref.at[pl.ds(start, n_tok * SUBLANES)]


def _copy_runs(tile, cnt_ref, goff_ref, lst_ref, make_copy):
    def run(e, carry):
        n = cnt_ref[tile * N_EXPERTS + e]
        g = goff_ref[tile * N_EXPERTS + e]
        l = lst_ref[tile * N_EXPERTS + e]
        for size in RUN_SIZES:
            take = (n & size) != 0

            @pl.when(take)
            def _(g=g, l=l, size=size):
                make_copy(l, g, size).start()

            step = jnp.where(take, size, 0)
            g = g + step
            l = l + step
        return carry

    lax.fori_loop(0, N_EXPERTS, run, 0)


def _dispatch_kernel(cnt_ref, goff_ref, lst_ref, ecnt_ref, estart_ref, x_ref, lpos_ref, xs_hbm,
                     buf0, buf1, zero_ref, sem_zero, sem_run):
    i = pl.program_id(0)
    bufs = (buf0, buf1)

    @pl.when(i == 0)
    def _():
        zero_ref[...] = jnp.zeros_like(zero_ref)

        def zero_copy(tok, n_tok):
            return pltpu.make_async_copy(zero_ref.at[pl.ds(0, n_tok * SUBLANES)],
                                         _tok_rows(xs_hbm, tok, n_tok), sem_zero)

        def fill_expert(e, carry):
            tok = estart_ref[e] + ecnt_ref[e]
            n_pad = (-ecnt_ref[e]) & (BM - 1)
            size = ZROWS
            while size >= 1:
                take = (n_pad & size) != 0

                @pl.when(take)
                def _(tok=tok, size=size):
                    cp = zero_copy(tok, size)
                    cp.start()
                    cp.wait()

                tok = tok + jnp.where(take, size, 0)
                size //= 2
            return carry

        lax.fori_loop(0, N_EXPERTS, fill_expert, 0)

        def fill_unused(c, carry):
            cp = zero_copy(c * ZROWS, ZROWS)
            cp.start()
            cp.wait()
            return carry

        last = N_EXPERTS - 1
        used_rows = estart_ref[last] + ((ecnt_ref[last] + (BM - 1)) & ~(BM - 1))
        lax.fori_loop(used_rows // ZROWS, R_ROWS // ZROWS, fill_unused, 0)

    xb = x_ref[...].astype(BF16)
    pos_t = lpos_ref[...].astype(F32).T

    def wait_runs(slot):
        pltpu.make_async_copy(bufs[slot], _tok_rows(xs_hbm, 0, TILE_ROWS), sem_run.at[slot]).wait()

    def sort_and_send(slot):
        buf = bufs[slot]

        @pl.when(i >= 2)
        def _():
            wait_runs(slot)

        for c in range(TILE_ROWS // TM):
            row = (c * TM + lax.broadcasted_iota(I32, (TM, TM), 0)).astype(F32)
            perm = jnp.zeros((TM, TM), F32)
            for k in range(TOP_K):
                perm = perm + jnp.where(row == pos_t[k:k + 1, :], 1.0, 0.0)
            rows = jnp.dot(perm.astype(BF16), xb, preferred_element_type=F32)
            _store_row_tiles(buf, rows, c * TM)

        _copy_runs(i, cnt_ref, goff_ref, lst_ref,
                   lambda l, g, size: pltpu.make_async_copy(_tok_rows(buf, l, size),
                                                            _tok_rows(xs_hbm, g, size),
                                                            sem_run.at[slot]))

        @pl.when(i == N_TILES - 1)
        def _():
            wait_runs(1 - slot)
            wait_runs(slot)

    for slot in range(2):
        @pl.when((i & 1) == slot)
        def _(slot=slot):
            sort_and_send(slot)


def _dispatch(x1, lpos, tile_cnt, tile_goff, tile_lst, counts, starts):
    return pl.pallas_call(
        _dispatch_kernel,
        grid_spec=pltpu.PrefetchScalarGridSpec(
            num_scalar_prefetch=5,
            grid=(N_TILES,),
            in_specs=[
                pl.BlockSpec((TM, D), lambda i, *_: (i, 0)),
                pl.BlockSpec((TM, LANES), lambda i, *_: (i, 0)),
            ],
            out_specs=pl.BlockSpec(memory_space=pl.ANY),
            scratch_shapes=[
                pltpu.VMEM((TILE_ROWS * SUBLANES, LANES), F32),
                pltpu.VMEM((TILE_ROWS * SUBLANES, LANES), F32),
                pltpu.VMEM((ZROWS * SUBLANES, LANES), F32),
                pltpu.SemaphoreType.DMA(()),
                pltpu.SemaphoreType.DMA((2,)),
            ],
        ),
        out_shape=jax.ShapeDtypeStruct((R_ROWS * SUBLANES, LANES), F32),
        compiler_params=_cparams(("arbitrary",)),
        name="moe_dispatch",
    )(tile_cnt, tile_goff, tile_lst, counts, starts, x1, lpos)


W_CHUNK = 64


def _split_even_odd(x):
    rows, cols = x.shape
    lane = lax.broadcasted_iota(I32, (rows, LANES), 1)
    idx_even = (2 * lane) % LANES
    idx_odd = (2 * lane + 1) % LANES
    low = lane < LANES // 2
    even, odd = [], []
    for c in range(cols // (2 * LANES)):
        a = x[:, (2 * c) * LANES:(2 * c + 1) * LANES]
        b = x[:, (2 * c + 1) * LANES:(2 * c + 2) * LANES]
        even.append(jnp.where(low, jnp.take_along_axis(a, idx_even, axis=1),
                              jnp.take_along_axis(b, idx_even, axis=1)))
        odd.append(jnp.where(low, jnp.take_along_axis(a, idx_odd, axis=1),
                             jnp.take_along_axis(b, idx_odd, axis=1)))
    return jnp.concatenate(even, axis=1), jnp.concatenate(odd, axis=1)


def _expert_kernel(blk_ref, used_ref, xs_ref, wu_ref, wd_ref, bu_ref, bd_ref, ys_ref,
                   wg_s, wl_s, wd_s, bg_s, bl_s):
    i = pl.program_id(0)
    active = i < used_ref[0]
    new_expert = (i == 0) | (blk_ref[i] != blk_ref[jnp.maximum(i - 1, 0)])

    @pl.when(active & new_expert)
    def _():
        def convert(r, carry):
            rows = pl.ds(pl.multiple_of(r * W_CHUNK, W_CHUNK), W_CHUNK)
            even, odd = _split_even_odd(wu_ref[0, 0, rows, :])
            wg_s[rows, :] = even.astype(BF16)
            wl_s[rows, :] = odd.astype(BF16)
            wd_s[rows, :] = wd_ref[0, 0, rows, :].astype(BF16)
            return carry

        lax.fori_loop(0, D // W_CHUNK, convert, 0)
        even, odd = _split_even_odd(jnp.broadcast_to(bu_ref[0, 0], (SUBLANES, 2 * D)))
        bg_s[...] = even
        bl_s[...] = odd

    @pl.when(active)
    def _():
        xb = _load_row_tiles(xs_ref, 0, BM).astype(BF16)
        glu = jnp.dot(xb, wg_s[...], preferred_element_type=F32) + bg_s[0:1, :]
        lin = jnp.dot(xb, wl_s[...], preferred_element_type=F32) + bl_s[0:1, :]
        glu = jnp.minimum(glu, SWIGLU_LIMIT)
        lin = jnp.clip(lin, -SWIGLU_LIMIT, SWIGLU_LIMIT)
        a = glu * jax.nn.sigmoid(SWIGLU_ALPHA * glu) * (lin + 1.0)
        y = jnp.dot(a.astype(BF16), wd_s[...], preferred_element_type=F32) + bd_ref[0, 0]
        _store_row_tiles(ys_ref, y)

    @pl.when(jnp.logical_not(active))
    def _():
        ys_ref[...] = jnp.zeros_like(ys_ref)


def _experts(layer, xs, blk_e, n_used, w_up, w_down, b_up, b_down):
    def row_map(i, blk, used):
        return (jnp.minimum(i, used[0] - 1), 0)

    def w_map(i, blk, used):
        return (layer, blk[jnp.minimum(i, used[0] - 1)], 0, 0)

    return pl.pallas_call(
        _expert_kernel,
        grid_spec=pltpu.PrefetchScalarGridSpec(
            num_scalar_prefetch=2,
            grid=(NB,),
            in_specs=[
                _tile_rows_spec(BM, row_map),
                pl.BlockSpec((1, 1, D, 2 * D), w_map),
                pl.BlockSpec((1, 1, D, D), w_map),
                pl.BlockSpec((1, 1, 1, 2 * D), w_map),
                pl.BlockSpec((1, 1, 1, D), w_map),
            ],
            out_specs=_tile_rows_spec(BM, lambda i, blk, used: (i, 0)),
            scratch_shapes=[
                pltpu.VMEM((D, D), BF16), pltpu.VMEM((D, D), BF16), pltpu.VMEM((D, D), BF16),
                pltpu.VMEM((SUBLANES, D), F32), pltpu.VMEM((SUBLANES, D), F32),
            ],
        ),
        out_shape=jax.ShapeDtypeStruct((R_ROWS * SUBLANES, LANES), F32),
        compiler_params=_cparams(("arbitrary",)),
        name="moe_experts",
    )(blk_e, n_used, xs, w_up, w_down, b_up, b_down)


N_TILES_PROMPT = TP // TM


def _combine_kernel(split_out, cnt_ref, goff_ref, lst_ref, x_ref, lpos_ref, gt_ref, lg_ref, lb_ref,
                    ys_hbm, *refs):
    if split_out:
        op_ref, os_ref, buf0, buf1, sem_run = refs
    else:
        o_ref, buf0, buf1, sem_run = refs
    bufs = (buf0, buf1)
    s = pl.program_id(0)

    def start_runs(slot):
        _copy_runs(s, cnt_ref, goff_ref, lst_ref,
                   lambda l, g, size: pltpu.make_async_copy(_tok_rows(ys_hbm, g, size),
                                                            _tok_rows(bufs[slot], l, size),
                                                            sem_run.at[slot]))

    def finish(slot):
        buf = bufs[slot]
        pltpu.make_async_copy(_tok_rows(ys_hbm, 0, TILE_ROWS), buf, sem_run.at[slot]).wait()
        gt = gt_ref[...]
        pos = lpos_ref[...]
        m = jnp.zeros((TM, D), F32)
        for c in range(TILE_ROWS // TM):
            col = c * TM + lax.broadcasted_iota(I32, (TM, TM), 1)
            weight = jnp.zeros((TM, TM), F32)
            for k in range(TOP_K):
                weight = weight + jnp.where(col == pos[:, k:k + 1], gt[:, k:k + 1], 0.0)
            rows = _load_row_tiles(buf, c * TM, TM).astype(BF16)
            m = m + jnp.dot(weight.astype(BF16), rows, preferred_element_type=F32)
        y = _layer_norm(ALPHA * x_ref[...] + m, lg_ref[...], lb_ref[...])
        if split_out:
            @pl.when(s - 1 < N_TILES_PROMPT)
            def _():
                op_ref[...] = y

            @pl.when(s - 1 >= N_TILES_PROMPT)
            def _():
                os_ref[...] = y
        else:
            o_ref[...] = y

    for slot in range(2):
        @pl.when((s < N_TILES) & ((s & 1) == slot))
        def _(slot=slot):
            start_runs(slot)

    for slot in range(2):
        @pl.when((s > 0) & (((s - 1) & 1) == slot))
        def _(slot=slot):
            finish(slot)


def _combine(x1, lpos, gates, ln_g, ln_b, ys, tile_cnt, tile_goff, tile_lst, split_out):
    def tile_map(s, *_):
        return (jnp.maximum(s - 1, 0), 0)

    if split_out:
        out_specs = [
            pl.BlockSpec((TM, D), lambda s, *_: (jnp.clip(s - 1, 0, N_TILES_PROMPT - 1), 0)),
            pl.BlockSpec((TM, D), lambda s, *_: (jnp.maximum(s - 1 - N_TILES_PROMPT, 0), 0)),
        ]
        out_shape = [jax.ShapeDtypeStruct((TP, D), F32), jax.ShapeDtypeStruct((TS, D), F32)]
    else:
        out_specs = pl.BlockSpec((TM, D), tile_map)
        out_shape = jax.ShapeDtypeStruct((T, D), F32)
    return pl.pallas_call(
        functools.partial(_combine_kernel, split_out),
        grid_spec=pltpu.PrefetchScalarGridSpec(
            num_scalar_prefetch=3,
            grid=(N_TILES + 1,),
            in_specs=[
                pl.BlockSpec((TM, D), tile_map),
                pl.BlockSpec((TM, LANES), tile_map),
                pl.BlockSpec((TM, TOP_K), tile_map),
                pl.BlockSpec((1, D), lambda s, *_: (0, 0)),
                pl.BlockSpec((1, D), lambda s, *_: (0, 0)),
                pl.BlockSpec(memory_space=pl.ANY),
            ],
            out_specs=out_specs,
            scratch_shapes=[
                pltpu.VMEM((TILE_ROWS * SUBLANES, LANES), F32),
                pltpu.VMEM((TILE_ROWS * SUBLANES, LANES), F32),
                pltpu.SemaphoreType.DMA((2,)),
            ],
        ),
        out_shape=out_shape,
        compiler_params=_cparams(("arbitrary",)),
        name="moe_combine_norm",
    )(tile_cnt, tile_goff, tile_lst, x1, lpos, gates, ln_g, ln_b, ys)


def _moe_layer(layer, x1, top_e, gates, w_up, b_up, w_down, b_down, ln_g, ln_b, split_out):
    lpos, tile_tab, blk_e, stat = _positions(top_e)

    def per_tile(row):
        return tile_tab[:, row, :N_EXPERTS].astype(I32).reshape(N_TILES * N_EXPERTS)

    tile_cnt, tile_goff, tile_lst = per_tile(0), per_tile(1), per_tile(2)
    counts = stat[0, :N_EXPERTS].astype(I32)
    starts = stat[8, :N_EXPERTS].astype(I32)
    n_used = (stat[16, N_EXPERTS - 1:N_EXPERTS].astype(I32)) // BM
    xs = _dispatch(x1, lpos, tile_cnt, tile_goff, tile_lst, counts, starts)
    ys = _experts(layer, xs, blk_e.reshape(NB_PAD), n_used, w_up, w_down, b_up[:, :, None, :],
                  b_down[:, :, None, :])
    return _combine(x1, lpos, gates, ln_g, ln_b, ys, tile_cnt, tile_goff, tile_lst, split_out)


def _rope_tables():
    half = DK // 2
    inv = 1.0 / (ROPE_BASE ** jnp.linspace(0.0, 1.0, half, dtype=F32))
    pos = jnp.concatenate([jnp.arange(SEQ), PAST_LEN + (jnp.arange(TM_PROJ) % DEC_SEQ)]).astype(F32)
    ang = pos[:, None] * inv[None, :]
    return jnp.cos(ang), jnp.sin(ang)


def kernel(x_prompt, x_sample, state_ret, cache_pool, ret_w_in, ret_w_out, pool_w_in, pool_w_grp,
           pool_scale, pool_w_out, ln1_g, ln1_b, ln2_g, ln2_b, router_w, router_b, w_up, b_up,
           w_down, b_down):
    xp = x_prompt.reshape(TP, D)
    xs = x_sample.reshape(TS, D)

    def vec(a):
        return a.reshape(1, -1)

    cos_t, sin_t = _rope_tables()
    p_all = _ret_project(xp, xs, ret_w_in[0].astype(BF16), cos_t, sin_t)
    o_prompt, s_prompt = _ret_core(p_all, BATCH, SUPER, SEQ // SUPER, 0)
    o_sample, s_sample = _ret_core(p_all, DEC_BATCH, DEC_SEQ, 1, TP // DEC_SEQ,
                                   s0=state_ret[0].reshape(DEC_BATCH * HEADS, DK, DV))
    x1, top_e, gates = _ret_out(o_prompt, o_sample, p_all, xp, xs, ret_w_out[0].astype(BF16),
                                vec(ln1_g[0]), vec(ln1_b[0]), router_w[0], vec(router_b[0]))
    x = _moe_layer(0, x1, top_e, gates, w_up, b_up, w_down, b_down, vec(ln2_g[0]), vec(ln2_b[0]),
                   split_out=False)

    x1, top_e, gates, cache_p, cache_s = _pool_mixer(
        x, cache_pool[0], pool_w_in[0].astype(BF16), pool_w_grp[0].astype(BF16),
        vec(pool_scale[0]), pool_w_out[0].astype(BF16), vec(ln1_g[1]), vec(ln1_b[1]), router_w[1],
        vec(router_b[1]))
    yp, ys = _moe_layer(1, x1, top_e, gates, w_up, b_up, w_down, b_down, vec(ln2_g[1]),
                        vec(ln2_b[1]), split_out=True)

    y_prompt = yp.reshape(BATCH, SEQ, D)
    y_sample = ys.reshape(DEC_BATCH, DEC_SEQ, D)
    state_ret_prompt = s_prompt.reshape(1, BATCH, HEADS, DK, DV)
    state_ret_sample = s_sample.reshape(1, DEC_BATCH, HEADS, DK, DV)
    cache_pool_prompt = cache_p[None, :, 1:, :]
    cache_pool_sample = cache_s[None, :, 1:, :]
    return (y_prompt, y_sample, state_ret_prompt, state_ret_sample, cache_pool_prompt,
            cache_pool_sample)
```

```python
import functools

import jax
import jax.numpy as jnp
from jax import lax
from jax.experimental import pallas as pl
from jax.experimental.pallas import tpu as pltpu

F32 = jnp.float32
BF16 = jnp.bfloat16
I32 = jnp.int32

D = 1024
BATCH = 16
SEQ = 2048
DEC_BATCH = 32
DEC_SEQ = 64
PAST_LEN = 4096
TP = BATCH * SEQ
TS = DEC_BATCH * DEC_SEQ
T = TP + TS

HEADS = 4
DK = 256
DV = 512
RET_QK = HEADS * DK
RET_V = HEADS * DV
RET_IN = 2 * RET_QK + 2 * RET_V
ROPE_BASE = 10000.0
RMS_EPS = 1e-6
LN_EPS = 1e-5
ALPHA = 4.0 ** 0.25

POOL_WINDOWS = (2, 4, 8, 16)
POOL_GC = D // 4
POOL_STATE = 15
CARRY = 16

N_EXPERTS = 32
TOP_K = 4
SWIGLU_LIMIT = 7.0
SWIGLU_ALPHA = 1.702

LANES = 128
SUBLANES = 8
assert D == SUBLANES * LANES
SUPER = 256
CHUNK = 64

TM_PROJ = 1024
TM = 512
BM = 512
NB = -(-(T * TOP_K) // BM) + N_EXPERTS
NB_PAD = -(-NB // 8) * 8
R_ROWS = NB * BM

VMEM_LIMIT = 56 * 1024 * 1024


def _cparams(sem, vmem=VMEM_LIMIT):
    return pltpu.CompilerParams(dimension_semantics=sem, vmem_limit_bytes=vmem)


def _proj_kernel(xp_ref, xs_ref, w_ref, cos_ref, sin_ref, o_ref, xb_ref):
    n = pl.program_id(1)

    @pl.when(n == 0)
    def _():
        x = jnp.where(pl.program_id(0) < TP // TM_PROJ, xp_ref[...], xs_ref[...])
        xb_ref[...] = x.astype(BF16)

    xb = xb_ref[...]

    @pl.when(n < 2)
    def _():
        scale = jnp.where(n == 1, DK ** -0.5, 1.0).astype(F32)
        cos = cos_ref[...] * scale
        sin = sin_ref[...] * scale
        half = DK // 2
        for h in range(HEADS):
            p = jnp.dot(xb, w_ref[:, h * DK:(h + 1) * DK], preferred_element_type=F32)
            t1 = p[:, :half]
            t2 = p[:, half:]
            o_ref[:, h * DK:h * DK + half] = (t1 * cos - t2 * sin).astype(BF16)
            o_ref[:, h * DK + half:(h + 1) * DK] = (t1 * sin + t2 * cos).astype(BF16)

    @pl.when(n >= 2)
    def _():
        for h in range(4):
            p = jnp.dot(xb, w_ref[:, h * 256:(h + 1) * 256], preferred_element_type=F32)
            o_ref[:, h * 256:(h + 1) * 256] = p.astype(BF16)


def _ret_project(x_prompt, x_sample, w_bf16, cos_t, sin_t):
    n_prompt_tiles = TP // TM_PROJ
    tiles_per_seq = SEQ // TM_PROJ

    def tab_map(i, n):
        return (jnp.where(i < n_prompt_tiles, i % tiles_per_seq, tiles_per_seq), 0)

    return pl.pallas_call(
        _proj_kernel,
        grid=(T // TM_PROJ, RET_IN // 1024),
        in_specs=[
            pl.BlockSpec((TM_PROJ, D), lambda i, n: (jnp.minimum(i, n_prompt_tiles - 1), 0)),
            pl.BlockSpec((TM_PROJ, D), lambda i, n: (jnp.maximum(i - n_prompt_tiles, 0), 0)),
            pl.BlockSpec((D, 1024), lambda i, n: (0, n)),
            pl.BlockSpec((TM_PROJ, LANES), tab_map),
            pl.BlockSpec((TM_PROJ, LANES), tab_map),
        ],
        out_specs=pl.BlockSpec((TM_PROJ, 1024), lambda i, n: (i, n)),
        out_shape=jax.ShapeDtypeStruct((T, RET_IN), BF16),
        scratch_shapes=[pltpu.VMEM((TM_PROJ, D), BF16)],
        compiler_params=_cparams(("arbitrary", "arbitrary")),
        name="ret_project",
    )(x_prompt, x_sample, w_bf16, cos_t, sin_t)


def _ret_core_kernel(has_init, n_steps, *refs):
    if has_init:
        (q_ref, k_ref, v_ref, mask_ref, qd_ref, kd_ref, bd_ref, s0_ref,
         o_ref, sout_ref, s_ref) = refs
    else:
        (q_ref, k_ref, v_ref, mask_ref, qd_ref, kd_ref, bd_ref,
         o_ref, sout_ref, s_ref) = refs
    c = pl.program_id(1)

    @pl.when(c == 0)
    def _():
        if has_init:
            s_ref[...] = s0_ref[...].astype(F32)
        else:
            s_ref[...] = jnp.zeros_like(s_ref)

    for h in range(HEADS):
        q = q_ref[:, h * DK:(h + 1) * DK]
        k = k_ref[:, h * DK:(h + 1) * DK]
        v = v_ref[:, h * DV:(h + 1) * DV]
        s_prev = s_ref[h]
        scores = lax.dot_general(q, k, (((1,), (1,)), ((), ())), preferred_element_type=F32)
        scores = scores * mask_ref[h]
        qd = (q.astype(F32) * qd_ref[h]).astype(BF16)
        o = (jnp.dot(scores.astype(BF16), v, preferred_element_type=F32)
             + jnp.dot(qd, s_prev.astype(BF16), preferred_element_type=F32))
        kd = (k.astype(F32) * kd_ref[h]).astype(BF16)
        s_new = s_prev * bd_ref[h] + lax.dot_general(kd, v, (((0,), (0,)), ((), ())),
                                                     preferred_element_type=F32)
        s_ref[h] = s_new
        o = o * lax.rsqrt(jnp.mean(o * o, axis=-1, keepdims=True) + RMS_EPS)
        o_ref[:, h * DV:(h + 1) * DV] = o.astype(BF16)

    @pl.when(c == n_steps - 1)
    def _():
        sout_ref[...] = s_ref[...]


def _decay_tables(rows):
    lg = jnp.log1p(-jnp.exp2(-5.0 - jnp.arange(HEADS, dtype=F32)))
    idx = jnp.arange(rows, dtype=F32)
    ch = jnp.arange(rows) // CHUNK
    diff = idx[:, None] - idx[None, :]
    same = ch[:, None] == ch[None, :]
    earlier = ch[None, :] < ch[:, None]
    expo = jnp.where(same, jnp.abs(diff), diff)
    w = jnp.exp(lg[:, None, None] * expo[None])
    mask = jnp.where((same | earlier)[None], w, 0.0).astype(F32)
    q_dec = jnp.exp(lg[:, None] * (idx + 1.0))[:, :, None]
    k_dec = jnp.exp(lg[:, None] * (rows - 1.0 - idx))[:, :, None]
    blk = jnp.broadcast_to(jnp.exp(lg * rows)[:, None, None], (HEADS, 1, DV))
    return mask, q_dec, k_dec, blk.astype(F32)


def _ret_core(p_all, n_seq, rows, n_steps, row_block0, s0=None):
    mask, q_dec, k_dec, blk = _decay_tables(rows)
    has_init = s0 is not None

    def rb(b, c):
        return row_block0 + b * n_steps + c

    def whole(shape):
        nd = len(shape)
        return pl.BlockSpec(shape, lambda b, c: (0,) * nd)

    in_specs = [
        pl.BlockSpec((rows, RET_QK), lambda b, c: (rb(b, c), 0)),
        pl.BlockSpec((rows, RET_QK), lambda b, c: (rb(b, c), 1)),
        pl.BlockSpec((rows, RET_V), lambda b, c: (rb(b, c), 1)),
        whole((HEADS, rows, rows)), whole((HEADS, rows, 1)), whole((HEADS, rows, 1)),
        whole((HEADS, 1, DV)),
    ]
    args = [p_all, p_all, p_all, mask, q_dec, k_dec, blk]
    if has_init:
        in_specs.append(pl.BlockSpec((HEADS, DK, DV), lambda b, c: (b, 0, 0)))
        args.append(s0)
    return pl.pallas_call(
        functools.partial(_ret_core_kernel, has_init, n_steps),
        grid=(n_seq, n_steps),
        in_specs=in_specs,
        out_specs=[
            pl.BlockSpec((rows, RET_V), lambda b, c: (b * n_steps + c, 0)),
            pl.BlockSpec((HEADS, DK, DV), lambda b, c: (b, 0, 0)),
        ],
        out_shape=[
            jax.ShapeDtypeStruct((n_seq * n_steps * rows, RET_V), BF16),
            jax.ShapeDtypeStruct((n_seq * HEADS, DK, DV), F32),
        ],
        scratch_shapes=[pltpu.VMEM((HEADS, DK, DV), F32)],
        compiler_params=_cparams(("arbitrary", "arbitrary")),
        name="ret_core_sample" if has_init else "ret_core_prompt",
    )(*args)


def _layer_norm(y, g, b):
    mu = jnp.mean(y, axis=-1, keepdims=True)
    yc = y - mu
    var = jnp.mean(yc * yc, axis=-1, keepdims=True)
    return yc * lax.rsqrt(var + LN_EPS) * g + b


def _split_bf16(a):
    hi = a.astype(BF16)
    lo = (a - hi.astype(F32)).astype(BF16)
    return hi, lo


def _route(x1, rw_ref, rb_ref, te_ref, gt_ref):
    rows = x1.shape[0]
    xh, xl = _split_bf16(x1)
    wh, wl = _split_bf16(rw_ref[...])
    logits = (jnp.dot(xh, wh, preferred_element_type=F32)
              + jnp.dot(xl, wh, preferred_element_type=F32)
              + jnp.dot(xh, wl, preferred_element_type=F32)) + rb_ref[...]
    lane = lax.broadcasted_iota(I32, (rows, N_EXPERTS), 1)
    lane_k = lax.broadcasted_iota(I32, (rows, TOP_K), 1)
    te = jnp.zeros((rows, TOP_K), I32)
    tv = jnp.zeros((rows, TOP_K), F32)
    cur = logits
    for j in range(TOP_K):
        m = jnp.max(cur, axis=-1, keepdims=True)
        idx = jnp.min(jnp.where(cur == m, lane, N_EXPERTS), axis=-1, keepdims=True)
        te = jnp.where(lane_k == j, idx, te)
        tv = jnp.where(lane_k == j, m, tv)
        cur = jnp.where(lane == idx, -jnp.inf, cur)
    ex = jnp.exp(tv - jnp.max(tv, axis=-1, keepdims=True))
    gt_ref[...] = ex / jnp.sum(ex, axis=-1, keepdims=True)
    te_ref[...] = te


def _load_row_tiles(ref, row0, rows, tok_stride=1):
    return jnp.concatenate(
        [ref[pl.ds(row0 * SUBLANES + s, rows, stride=SUBLANES * tok_stride), :]
         for s in range(SUBLANES)], axis=1)


def _store_row_tiles(ref, val, row0=0):
    rows = val.shape[0]
    for s in range(SUBLANES):
        ref[pl.ds(row0 * SUBLANES + s, rows, stride=SUBLANES), :] = val[:, s * LANES:(s + 1) * LANES]


def _residual_norm_route(x, h, lg_ref, lb_ref, rw_ref, rb_ref, x1_ref, te_ref, gt_ref):
    x1 = _layer_norm(ALPHA * x + h, lg_ref[...], lb_ref[...])
    x1_ref[...] = x1
    _route(x1, rw_ref, rb_ref, te_ref, gt_ref)


def _ret_out_kernel(op_ref, os_ref, g_ref, xp_ref, xs_ref, w_ref, lg_ref, lb_ref, rw_ref, rb_ref,
                    x1_ref, te_ref, gt_ref):
    is_prompt = pl.program_id(0) < TP // TM
    g = g_ref[...].astype(F32)
    o = jnp.where(is_prompt, op_ref[...], os_ref[...])
    a = (g * jax.nn.sigmoid(g) * o.astype(F32)).astype(BF16)
    h = jnp.dot(a, w_ref[...], preferred_element_type=F32)
    x = jnp.where(is_prompt, xp_ref[...], xs_ref[...])
    _residual_norm_route(x, h, lg_ref, lb_ref, rw_ref, rb_ref, x1_ref, te_ref, gt_ref)


def _row_spec(cols, col_block=0):
    return pl.BlockSpec((TM, cols), lambda i: (i, col_block))


def _const_spec(shape):
    nd = len(shape)
    return pl.BlockSpec(shape, lambda i: (0,) * nd)


def _tile_rows_spec(rows, index_map):
    return pl.BlockSpec((rows * SUBLANES, LANES), index_map)


_EPILOGUE_OUT_SPECS = [_row_spec(D), _row_spec(TOP_K), _row_spec(TOP_K)]
_EPILOGUE_OUT_SHAPE = [
    jax.ShapeDtypeStruct((T, D), F32),
    jax.ShapeDtypeStruct((T, TOP_K), I32),
    jax.ShapeDtypeStruct((T, TOP_K), F32),
]


def _ret_out(o_prompt, o_sample, p_all, x_prompt, x_sample, w_out_bf16, ln_g, ln_b, router_w,
             router_b):
    n_prompt_tiles = TP // TM

    def prompt_spec(cols):
        return pl.BlockSpec((TM, cols), lambda i: (jnp.minimum(i, n_prompt_tiles - 1), 0))

    def sample_spec(cols):
        return pl.BlockSpec((TM, cols), lambda i: (jnp.maximum(i - n_prompt_tiles, 0), 0))

    return pl.pallas_call(
        _ret_out_kernel,
        grid=(T // TM,),
        in_specs=[
            prompt_spec(RET_V), sample_spec(RET_V),
            _row_spec(RET_V, col_block=2),
            prompt_spec(D), sample_spec(D),
            _const_spec((RET_V, D)),
            _const_spec((1, D)), _const_spec((1, D)),
            _const_spec((D, N_EXPERTS)), _const_spec((1, N_EXPERTS)),
        ],
        out_specs=_EPILOGUE_OUT_SPECS,
        out_shape=_EPILOGUE_OUT_SHAPE,
        compiler_params=_cparams(("arbitrary",)),
        name="ret_out_norm_route",
    )(o_prompt, o_sample, p_all, x_prompt, x_sample, w_out_bf16, ln_g, ln_b, router_w, router_b)


SEQS_PER_TILE = TM // DEC_SEQ


def _group_cols(gi):
    return slice(gi * POOL_GC, (gi + 1) * POOL_GC)


def _band_tables():
    i = jnp.arange(TM)[:, None]
    j = jnp.arange(TM)[None, :]
    hi = jnp.arange(CARRY)[:, None]
    hr = jnp.arange(CARRY)[None, :]
    band_p, band_s, hist = [], [], []
    for w in POOL_WINDOWS:
        inside = (j <= i) & (j > i - w)
        band_p.append(inside)
        band_s.append(inside & (i // DEC_SEQ == j // DEC_SEQ))
        hist.append(hr >= hi + CARRY + 1 - w)
    as_bf16 = lambda ms: jnp.stack(ms).astype(BF16)
    return as_bf16(band_p), as_bf16(band_s), as_bf16(hist)


def _pool_kernel(x_ref, pre_ref, bandp_ref, bands_ref, hband_ref, win_ref, wg_ref, sc_ref, wout_ref,
                 lg_ref, lb_ref, rw_ref, rb_ref,
                 x1_ref, te_ref, gt_ref, cachep_ref, caches_ref, carry_ref, seq_ref, pooled_ref):
    i = pl.program_id(0)
    n_prompt_tiles = TP // TM
    tiles_per_seq = SEQ // TM
    is_prompt = i < n_prompt_tiles
    x = x_ref[...]
    u = jnp.dot(x.astype(BF16), win_ref[...], preferred_element_type=F32)
    u_hi, u_lo = _split_bf16(u)

    sums = []
    for gi in range(len(POOL_WINDOWS)):
        cols = _group_cols(gi)
        band = jnp.where(is_prompt, bandp_ref[gi], bands_ref[gi])
        sums.append(jnp.dot(band, u_hi[:, cols], preferred_element_type=F32)
                    + jnp.dot(band, u_lo[:, cols], preferred_element_type=F32))

    def history_sums(hist, gi):
        cols = _group_cols(gi)
        h_hi, h_lo = _split_bf16(hist[:, cols])
        return (jnp.dot(hband_ref[gi], h_hi, preferred_element_type=F32)
                + jnp.dot(hband_ref[gi], h_lo, preferred_element_type=F32))

    @pl.when(is_prompt)
    def _():
        j = i % tiles_per_seq

        @pl.when(j == 0)
        def _():
            carry_ref[...] = jnp.zeros_like(carry_ref)

        pos = (j * TM + lax.broadcasted_iota(I32, (TM, 1), 0)).astype(F32)
        hist = carry_ref[...]
        for gi, w in enumerate(POOL_WINDOWS):
            cols = _group_cols(gi)
            cnt = jnp.minimum(pos + 1.0, float(w))
            pooled_ref[:, cols] = sums[gi] / cnt - u[:, cols]
            head = sums[gi][0:CARRY, :] + history_sums(hist, gi)
            pooled_ref[0:CARRY, cols] = head / cnt[0:CARRY, :] - u[0:CARRY, cols]

        @pl.when(j == tiles_per_seq - 1)
        def _():
            cachep_ref[0] = u[TM - CARRY:TM, :]

        carry_ref[...] = u[TM - CARRY:TM, :]

    @pl.when(jnp.logical_not(is_prompt))
    def _():
        for gi, w in enumerate(POOL_WINDOWS):
            cols = _group_cols(gi)
            pooled_ref[:, cols] = sums[gi] * (1.0 / w) - u[:, cols]
        seq_ref[0:1, :] = jnp.zeros((1, D), F32)
        for r in range(SEQS_PER_TILE):
            r0 = r * DEC_SEQ
            seq_ref[1:CARRY, :] = pre_ref[r]
            hist = seq_ref[...]
            for gi, w in enumerate(POOL_WINDOWS):
                cols = _group_cols(gi)
                head = sums[gi][r0:r0 + CARRY, :] + history_sums(hist, gi)
                pooled_ref[r0:r0 + CARRY, cols] = head * (1.0 / w) - u[r0:r0 + CARRY, cols]
            caches_ref[r] = u[r0 + DEC_SEQ - CARRY:r0 + DEC_SEQ, :]

    mixed = [jnp.dot(pooled_ref[:, _group_cols(gi)].astype(BF16), wg_ref[gi],
                     preferred_element_type=F32) for gi in range(len(POOL_WINDOWS))]
    mixed = jnp.concatenate(mixed, axis=-1) * sc_ref[...]
    h = jnp.dot(mixed.astype(BF16), wout_ref[...], preferred_element_type=F32)
    _residual_norm_route(x, h, lg_ref, lb_ref, rw_ref, rb_ref, x1_ref, te_ref, gt_ref)


def _pool_mixer(x, prefix, win, wgrp, scale, wout, ln_g, ln_b, router_w, router_b):
    n_prompt_tiles = TP // TM
    tiles_per_seq = SEQ // TM
    n_win = len(POOL_WINDOWS)
    band_p, band_s, hist = _band_tables()
    return pl.pallas_call(
        _pool_kernel,
        grid=(T // TM,),
        in_specs=[
            _row_spec(D),
            pl.BlockSpec((SEQS_PER_TILE, POOL_STATE, D),
                         lambda i: (jnp.maximum(i - n_prompt_tiles, 0), 0, 0)),
            _const_spec((n_win, TM, TM)), _const_spec((n_win, TM, TM)),
            _const_spec((n_win, CARRY, CARRY)),
            _const_spec((D, D)), _const_spec((n_win, POOL_GC, POOL_GC)),
            _const_spec((1, D)), _const_spec((D, D)),
            _const_spec((1, D)), _const_spec((1, D)),
            _const_spec((D, N_EXPERTS)), _const_spec((1, N_EXPERTS)),
        ],
        out_specs=_EPILOGUE_OUT_SPECS + [
            pl.BlockSpec((1, CARRY, D),
                         lambda i: (jnp.minimum(i, n_prompt_tiles - 1) // tiles_per_seq, 0, 0)),
            pl.BlockSpec((SEQS_PER_TILE, CARRY, D),
                         lambda i: (jnp.maximum(i - n_prompt_tiles, 0), 0, 0)),
        ],
        out_shape=_EPILOGUE_OUT_SHAPE + [
            jax.ShapeDtypeStruct((BATCH, CARRY, D), F32),
            jax.ShapeDtypeStruct((DEC_BATCH, CARRY, D), F32),
        ],
        scratch_shapes=[
            pltpu.VMEM((CARRY, D), F32),
            pltpu.VMEM((CARRY, D), F32),
            pltpu.VMEM((TM, D), F32),
        ],
        compiler_params=_cparams(("arbitrary",)),
        name="pool_norm_route",
    )(x, prefix, band_p, band_s, hist, win, wgrp, scale, wout, ln_g, ln_b, router_w, router_b)


def _lane_cumsum(v):
    lane = lax.broadcasted_iota(I32, v.shape, 1)
    shift = 1
    while shift < LANES:
        v = v + jnp.where(lane >= shift, pltpu.roll(v, shift, axis=1), 0.0)
        shift *= 2
    return v


def _positions_kernel(te_ref, lpos_ref, tile_ref, blk_ref, stat_ref, cnt_ref, base_ref):
    phase = pl.program_id(0)
    i = pl.program_id(1)
    te = te_ref[...]
    lane = lax.broadcasted_iota(I32, (TM, LANES), 1)
    hits = [lane == te[:, j:j + 1] for j in range(TOP_K)]
    onehot = sum(h.astype(F32) for h in hits)
    tile_cnt = jnp.sum(onehot, axis=0, keepdims=True)

    @pl.when((phase == 0) & (i == 0))
    def _():
        cnt_ref[...] = jnp.zeros_like(cnt_ref)

    @pl.when(phase == 0)
    def _():
        cnt_ref[...] = cnt_ref[...] + tile_cnt

    @pl.when((phase == 1) & (i == 0))
    def _():
        cnt = cnt_ref[...].astype(I32)
        padded = ((cnt + (BM - 1)) & ~(BM - 1)).astype(F32)
        pad_end = _lane_cumsum(padded)
        base_ref[...] = pad_end - padded
        stat_ref[...] = jnp.concatenate([cnt_ref[...], pad_end - padded, pad_end], axis=0)
        lane8 = lax.broadcasted_iota(I32, (NB_PAD, LANES), 1)
        start = (lax.broadcasted_iota(I32, (NB_PAD, LANES), 0) * BM).astype(F32)
        done = jnp.where((lane8 < N_EXPERTS) & (pad_end[0:1, :] <= start), 1.0, 0.0)
        blk = jnp.minimum(jnp.sum(done, axis=-1, keepdims=True), N_EXPERTS - 1.0)
        blk_ref[...] = blk.astype(I32)

    @pl.when(phase == 1)
    def _():
        r = lax.broadcasted_iota(I32, (TM, TM), 0)
        c = lax.broadcasted_iota(I32, (TM, TM), 1)
        tri = jnp.where(r > c, 1.0, 0.0).astype(BF16)
        before = jnp.dot(tri, onehot.astype(BF16), preferred_element_type=F32)
        cnt8 = jnp.broadcast_to(tile_cnt, (SUBLANES, LANES))
        local_start = _lane_cumsum(cnt8) - cnt8
        slot = local_start[0:1, :] + before
        lpos = jnp.zeros((TM, LANES), F32)
        for j in range(TOP_K):
            pj = jnp.sum(jnp.where(hits[j], slot, 0.0), axis=-1, keepdims=True)
            lpos = jnp.where(lane == j, pj, lpos)
        lpos_ref[...] = lpos.astype(I32)
        row = lax.broadcasted_iota(I32, (SUBLANES, LANES), 0)
        tile_ref[0] = jnp.where(row == 0, cnt8, jnp.where(row == 1, base_ref[...], local_start))
        base_ref[...] = base_ref[...] + tile_cnt


def _positions(top_e):
    return pl.pallas_call(
        _positions_kernel,
        grid=(2, T // TM),
        in_specs=[pl.BlockSpec((TM, TOP_K), lambda p, i: (i, 0))],
        out_specs=[
            pl.BlockSpec((TM, LANES), lambda p, i: (i * p, 0)),
            pl.BlockSpec((1, SUBLANES, LANES), lambda p, i: (i * p, 0, 0)),
            pl.BlockSpec((NB_PAD, 1), lambda p, i: (0, 0)),
            pl.BlockSpec((24, LANES), lambda p, i: (0, 0)),
        ],
        out_shape=[
            jax.ShapeDtypeStruct((T, LANES), I32),
            jax.ShapeDtypeStruct((T // TM, SUBLANES, LANES), F32),
            jax.ShapeDtypeStruct((NB_PAD, 1), I32),
            jax.ShapeDtypeStruct((24, LANES), F32),
        ],
        scratch_shapes=[pltpu.VMEM((8, LANES), F32), pltpu.VMEM((8, LANES), F32)],
        compiler_params=_cparams(("arbitrary", "arbitrary")),
        name="moe_positions",
    )(top_e)


N_TILES = T // TM
TILE_ROWS = TM * TOP_K
RUN_SIZES = tuple(TM >> b for b in range(TM.bit_length()))
ZROWS = 256


def _tok_rows(ref, tok, n_tok):
    start = tok * SUBLANES
    if not isinstance(start, int):
        start = pl.multiple_of(start, SUBLANES)
    return ref.at[pl.ds(start, n_tok * SUBLANES)]


def _copy_runs(tile, cnt_ref, goff_ref, lst_ref, make_copy):
    def run(e, carry):
        n = cnt_ref[tile * N_EXPERTS + e]
        g = goff_ref[tile * N_EXPERTS + e]
        l = lst_ref[tile * N_EXPERTS + e]
        for piece, size in enumerate(RUN_SIZES):
            take = (n & size) != 0

            @pl.when(take)
            def _(g=g, l=l, size=size, piece=piece):
                make_copy(l, g, size).start(priority=piece % 2)

            step = jnp.where(take, size, 0)
            g = g + step
            l = l + step
        return carry

    lax.fori_loop(0, N_EXPERTS, run, 0)


def _dispatch_kernel(cnt_ref, goff_ref, lst_ref, ecnt_ref, estart_ref, x_ref, lpos_ref, xs_hbm,
                     buf0, buf1, zero_ref, sem_zero, sem_run):
    i = pl.program_id(0)
    bufs = (buf0, buf1)

    @pl.when(i == 0)
    def _():
        zero_ref[...] = jnp.zeros_like(zero_ref)

        def zero_copy(tok, n_tok):
            return pltpu.make_async_copy(zero_ref.at[pl.ds(0, n_tok * SUBLANES)],
                                         _tok_rows(xs_hbm, tok, n_tok), sem_zero)

        def fill_expert(e, carry):
            tok = estart_ref[e] + ecnt_ref[e]
            n_pad = (-ecnt_ref[e]) & (BM - 1)
            size = ZROWS
            while size >= 1:
                take = (n_pad & size) != 0

                @pl.when(take)
                def _(tok=tok, size=size):
                    cp = zero_copy(tok, size)
                    cp.start()
                    cp.wait()

                tok = tok + jnp.where(take, size, 0)
                size //= 2
            return carry

        lax.fori_loop(0, N_EXPERTS, fill_expert, 0)

        def fill_unused(c, carry):
            cp = zero_copy(c * ZROWS, ZROWS)
            cp.start()
            cp.wait()
            return carry

        last = N_EXPERTS - 1
        used_rows = estart_ref[last] + ((ecnt_ref[last] + (BM - 1)) & ~(BM - 1))
        lax.fori_loop(used_rows // ZROWS, R_ROWS // ZROWS, fill_unused, 0)

    xb = x_ref[...].astype(BF16)
    pos_t = lpos_ref[...].astype(F32).T

    def wait_runs(slot):
        pltpu.make_async_copy(bufs[slot], _tok_rows(xs_hbm, 0, TILE_ROWS), sem_run.at[slot]).wait()

    def sort_and_send(slot):
        buf = bufs[slot]

        @pl.when(i >= 2)
        def _():
            wait_runs(slot)

        for c in range(TILE_ROWS // TM):
            row = (c * TM + lax.broadcasted_iota(I32, (TM, TM), 0)).astype(F32)
            perm = jnp.zeros((TM, TM), F32)
            for k in range(TOP_K):
                perm = perm + jnp.where(row == pos_t[k:k + 1, :], 1.0, 0.0)
            rows = jnp.dot(perm.astype(BF16), xb, preferred_element_type=F32)
            _store_row_tiles(buf, rows, c * TM)

        _copy_runs(i, cnt_ref, goff_ref, lst_ref,
                   lambda l, g, size: pltpu.make_async_copy(_tok_rows(buf, l, size),
                                                            _tok_rows(xs_hbm, g, size),
                                                            sem_run.at[slot]))

        @pl.when(i == N_TILES - 1)
        def _():
            wait_runs(1 - slot)
            wait_runs(slot)

    for slot in range(2):
        @pl.when((i & 1) == slot)
        def _(slot=slot):
            sort_and_send(slot)


def _dispatch(x1, lpos, tile_cnt, tile_goff, tile_lst, counts, starts):
    return pl.pallas_call(
        _dispatch_kernel,
        grid_spec=pltpu.PrefetchScalarGridSpec(
            num_scalar_prefetch=5,
            grid=(N_TILES,),
            in_specs=[
                pl.BlockSpec((TM, D), lambda i, *_: (i, 0)),
                pl.BlockSpec((TM, LANES), lambda i, *_: (i, 0)),
            ],
            out_specs=pl.BlockSpec(memory_space=pl.ANY),
            scratch_shapes=[
                pltpu.VMEM((TILE_ROWS * SUBLANES, LANES), F32),
                pltpu.VMEM((TILE_ROWS * SUBLANES, LANES), F32),
                pltpu.VMEM((ZROWS * SUBLANES, LANES), F32),
                pltpu.SemaphoreType.DMA(()),
                pltpu.SemaphoreType.DMA((2,)),
            ],
        ),
        out_shape=jax.ShapeDtypeStruct((R_ROWS * SUBLANES, LANES), F32),
        compiler_params=_cparams(("arbitrary",)),
        name="moe_dispatch",
    )(tile_cnt, tile_goff, tile_lst, counts, starts, x1, lpos)


W_CHUNK = 64


def _split_even_odd(x):
    rows, cols = x.shape
    lane = lax.broadcasted_iota(I32, (rows, LANES), 1)
    idx_even = (2 * lane) % LANES
    idx_odd = (2 * lane + 1) % LANES
    low = lane < LANES // 2
    even, odd = [], []
    for c in range(cols // (2 * LANES)):
        a = x[:, (2 * c) * LANES:(2 * c + 1) * LANES]
        b = x[:, (2 * c + 1) * LANES:(2 * c + 2) * LANES]
        even.append(jnp.where(low, jnp.take_along_axis(a, idx_even, axis=1),
                              jnp.take_along_axis(b, idx_even, axis=1)))
        odd.append(jnp.where(low, jnp.take_along_axis(a, idx_odd, axis=1),
                             jnp.take_along_axis(b, idx_odd, axis=1)))
    return jnp.concatenate(even, axis=1), jnp.concatenate(odd, axis=1)


def _expert_kernel(blk_ref, used_ref, xs_ref, wu_ref, wd_ref, bu_ref, bd_ref, ys_ref,
                   wg_s, wl_s, wd_s, bg_s, bl_s):
    i = pl.program_id(0)
    active = i < used_ref[0]
    new_expert = (i == 0) | (blk_ref[i] != blk_ref[jnp.maximum(i - 1, 0)])

    @pl.when(active & new_expert)
    def _():
        def convert(r, carry):
            rows = pl.ds(pl.multiple_of(r * W_CHUNK, W_CHUNK), W_CHUNK)
            even, odd = _split_even_odd(wu_ref[0, 0, rows, :])
            wg_s[rows, :] = even.astype(BF16)
            wl_s[rows, :] = odd.astype(BF16)
            wd_s[rows, :] = wd_ref[0, 0, rows, :].astype(BF16)
            return carry

        lax.fori_loop(0, D // W_CHUNK, convert, 0)
        even, odd = _split_even_odd(jnp.broadcast_to(bu_ref[0, 0], (SUBLANES, 2 * D)))
        bg_s[...] = even
        bl_s[...] = odd

    @pl.when(active)
    def _():
        xb = _load_row_tiles(xs_ref, 0, BM).astype(BF16)
        glu = jnp.dot(xb, wg_s[...], preferred_element_type=F32) + bg_s[0:1, :]
        lin = jnp.dot(xb, wl_s[...], preferred_element_type=F32) + bl_s[0:1, :]
        glu = jnp.minimum(glu, SWIGLU_LIMIT)
        lin = jnp.clip(lin, -SWIGLU_LIMIT, SWIGLU_LIMIT)
        a = glu * jax.nn.sigmoid(SWIGLU_ALPHA * glu) * (lin + 1.0)
        y = jnp.dot(a.astype(BF16), wd_s[...], preferred_element_type=F32) + bd_ref[0, 0]
        _store_row_tiles(ys_ref, y)

    @pl.when(jnp.logical_not(active))
    def _():
        ys_ref[...] = jnp.zeros_like(ys_ref)


def _experts(layer, xs, blk_e, n_used, w_up, w_down, b_up, b_down):
    def row_map(i, blk, used):
        return (jnp.minimum(i, used[0] - 1), 0)

    def w_map(i, blk, used):
        return (layer, blk[jnp.minimum(i, used[0] - 1)], 0, 0)

    return pl.pallas_call(
        _expert_kernel,
        grid_spec=pltpu.PrefetchScalarGridSpec(
            num_scalar_prefetch=2,
            grid=(NB,),
            in_specs=[
                _tile_rows_spec(BM, row_map),
                pl.BlockSpec((1, 1, D, 2 * D), w_map),
                pl.BlockSpec((1, 1, D, D), w_map),
                pl.BlockSpec((1, 1, 1, 2 * D), w_map),
                pl.BlockSpec((1, 1, 1, D), w_map),
            ],
            out_specs=_tile_rows_spec(BM, lambda i, blk, used: (i, 0)),
            scratch_shapes=[
                pltpu.VMEM((D, D), BF16), pltpu.VMEM((D, D), BF16), pltpu.VMEM((D, D), BF16),
                pltpu.VMEM((SUBLANES, D), F32), pltpu.VMEM((SUBLANES, D), F32),
            ],
        ),
        out_shape=jax.ShapeDtypeStruct((R_ROWS * SUBLANES, LANES), F32),
        compiler_params=_cparams(("arbitrary",)),
        name="moe_experts",
    )(blk_e, n_used, xs, w_up, w_down, b_up, b_down)


N_TILES_PROMPT = TP // TM


def _combine_kernel(split_out, cnt_ref, goff_ref, lst_ref, x_ref, lpos_ref, gt_ref, lg_ref, lb_ref,
                    ys_hbm, *refs):
    if split_out:
        op_ref, os_ref, buf0, buf1, sem_run = refs
    else:
        o_ref, buf0, buf1, sem_run = refs
    bufs = (buf0, buf1)
    s = pl.program_id(0)

    def start_runs(slot):
        _copy_runs(s, cnt_ref, goff_ref, lst_ref,
                   lambda l, g, size: pltpu.make_async_copy(_tok_rows(ys_hbm, g, size),
                                                            _tok_rows(bufs[slot], l, size),
                                                            sem_run.at[slot]))

    def finish(slot):
        buf = bufs[slot]
        pltpu.make_async_copy(_tok_rows(ys_hbm, 0, TILE_ROWS), buf, sem_run.at[slot]).wait()
        gt = gt_ref[...]
        pos = lpos_ref[...]
        m = jnp.zeros((TM, D), F32)
        for c in range(TILE_ROWS // TM):
            col = c * TM + lax.broadcasted_iota(I32, (TM, TM), 1)
            weight = jnp.zeros((TM, TM), F32)
            for k in range(TOP_K):
                weight = weight + jnp.where(col == pos[:, k:k + 1], gt[:, k:k + 1], 0.0)
            rows = _load_row_tiles(buf, c * TM, TM).astype(BF16)
            m = m + jnp.dot(weight.astype(BF16), rows, preferred_element_type=F32)
        y = _layer_norm(ALPHA * x_ref[...] + m, lg_ref[...], lb_ref[...])
        if split_out:
            @pl.when(s - 1 < N_TILES_PROMPT)
            def _():
                op_ref[...] = y

            @pl.when(s - 1 >= N_TILES_PROMPT)
            def _():
                os_ref[...] = y
        else:
            o_ref[...] = y

    for slot in range(2):
        @pl.when((s < N_TILES) & ((s & 1) == slot))
        def _(slot=slot):
            start_runs(slot)

    for slot in range(2):
        @pl.when((s > 0) & (((s - 1) & 1) == slot))
        def _(slot=slot):
            finish(slot)


def _combine(x1, lpos, gates, ln_g, ln_b, ys, tile_cnt, tile_goff, tile_lst, split_out):
    def tile_map(s, *_):
        return (jnp.maximum(s - 1, 0), 0)

    if split_out:
        out_specs = [
            pl.BlockSpec((TM, D), lambda s, *_: (jnp.clip(s - 1, 0, N_TILES_PROMPT - 1), 0)),
            pl.BlockSpec((TM, D), lambda s, *_: (jnp.maximum(s - 1 - N_TILES_PROMPT, 0), 0)),
        ]
        out_shape = [jax.ShapeDtypeStruct((TP, D), F32), jax.ShapeDtypeStruct((TS, D), F32)]
    else:
        out_specs = pl.BlockSpec((TM, D), tile_map)
        out_shape = jax.ShapeDtypeStruct((T, D), F32)
    return pl.pallas_call(
        functools.partial(_combine_kernel, split_out),
        grid_spec=pltpu.PrefetchScalarGridSpec(
            num_scalar_prefetch=3,
            grid=(N_TILES + 1,),
            in_specs=[
                pl.BlockSpec((TM, D), tile_map),
                pl.BlockSpec((TM, LANES), tile_map),
                pl.BlockSpec((TM, TOP_K), tile_map),
                pl.BlockSpec((1, D), lambda s, *_: (0, 0)),
                pl.BlockSpec((1, D), lambda s, *_: (0, 0)),
                pl.BlockSpec(memory_space=pl.ANY),
            ],
            out_specs=out_specs,
            scratch_shapes=[
                pltpu.VMEM((TILE_ROWS * SUBLANES, LANES), F32),
                pltpu.VMEM((TILE_ROWS * SUBLANES, LANES), F32),
                pltpu.SemaphoreType.DMA((2,)),
            ],
        ),
        out_shape=out_shape,
        compiler_params=_cparams(("arbitrary",)),
        name="moe_combine_norm",
    )(tile_cnt, tile_goff, tile_lst, x1, lpos, gates, ln_g, ln_b, ys)


def _moe_layer(layer, x1, top_e, gates, w_up, b_up, w_down, b_down, ln_g, ln_b, split_out):
    lpos, tile_tab, blk_e, stat = _positions(top_e)

    def per_tile(row):
        return tile_tab[:, row, :N_EXPERTS].astype(I32).reshape(N_TILES * N_EXPERTS)

    tile_cnt, tile_goff, tile_lst = per_tile(0), per_tile(1), per_tile(2)
    counts = stat[0, :N_EXPERTS].astype(I32)
    starts = stat[8, :N_EXPERTS].astype(I32)
    n_used = (stat[16, N_EXPERTS - 1:N_EXPERTS].astype(I32)) // BM
    xs = _dispatch(x1, lpos, tile_cnt, tile_goff, tile_lst, counts, starts)
    ys = _experts(layer, xs, blk_e.reshape(NB_PAD), n_used, w_up, w_down, b_up[:, :, None, :],
                  b_down[:, :, None, :])
    return _combine(x1, lpos, gates, ln_g, ln_b, ys, tile_cnt, tile_goff, tile_lst, split_out)


def _rope_tables():
    half = DK // 2
    inv = 1.0 / (ROPE_BASE ** jnp.linspace(0.0, 1.0, half, dtype=F32))
    pos = jnp.concatenate([jnp.arange(SEQ), PAST_LEN + (jnp.arange(TM_PROJ) % DEC_SEQ)]).astype(F32)
    ang = pos[:, None] * inv[None, :]
    return jnp.cos(ang), jnp.sin(ang)


def kernel(x_prompt, x_sample, state_ret, cache_pool, ret_w_in, ret_w_out, pool_w_in, pool_w_grp,
           pool_scale, pool_w_out, ln1_g, ln1_b, ln2_g, ln2_b, router_w, router_b, w_up, b_up,
           w_down, b_down):
    xp = x_prompt.reshape(TP, D)
    xs = x_sample.reshape(TS, D)

    def vec(a):
        return a.reshape(1, -1)

    cos_t, sin_t = _rope_tables()
    p_all = _ret_project(xp, xs, ret_w_in[0].astype(BF16), cos_t, sin_t)
    o_prompt, s_prompt = _ret_core(p_all, BATCH, SUPER, SEQ // SUPER, 0)
    o_sample, s_sample = _ret_core(p_all, DEC_BATCH, DEC_SEQ, 1, TP // DEC_SEQ,
                                   s0=state_ret[0].reshape(DEC_BATCH * HEADS, DK, DV))
    x1, top_e, gates = _ret_out(o_prompt, o_sample, p_all, xp, xs, ret_w_out[0].astype(BF16),
                                vec(ln1_g[0]), vec(ln1_b[0]), router_w[0], vec(router_b[0]))
    x = _moe_layer(0, x1, top_e, gates, w_up, b_up, w_down, b_down, vec(ln2_g[0]), vec(ln2_b[0]),
                   split_out=False)

    x1, top_e, gates, cache_p, cache_s = _pool_mixer(
        x, cache_pool[0], pool_w_in[0].astype(BF16), pool_w_grp[0].astype(BF16),
        vec(pool_scale[0]), pool_w_out[0].astype(BF16), vec(ln1_g[1]), vec(ln1_b[1]), router_w[1],
        vec(router_b[1]))
    yp, ys = _moe_layer(1, x1, top_e, gates, w_up, b_up, w_down, b_down, vec(ln2_g[1]),
                        vec(ln2_b[1]), split_out=True)

    y_prompt = yp.reshape(BATCH, SEQ, D)
    y_sample = ys.reshape(DEC_BATCH, DEC_SEQ, D)
    state_ret_prompt = s_prompt.reshape(1, BATCH, HEADS, DK, DV)
    state_ret_sample = s_sample.reshape(1, DEC_BATCH, HEADS, DK, DV)
    cache_pool_prompt = cache_p[None, :, 1:, :]
    cache_pool_sample = cache_s[None, :, 1:, :]
    return (y_prompt, y_sample, state_ret_prompt, state_ret_sample, cache_pool_prompt,
            cache_pool_sample)
```

```python
import functools

import jax
import jax.numpy as jnp
from jax import lax
from jax.experimental import pallas as pl
from jax.experimental.pallas import tpu as pltpu

F32 = jnp.float32
BF16 = jnp.bfloat16
I32 = jnp.int32

D = 1024
BATCH = 16
SEQ = 2048
DEC_BATCH = 32
DEC_SEQ = 64
PAST_LEN = 4096
TP = BATCH * SEQ
TS = DEC_BATCH * DEC_SEQ
T = TP + TS

HEADS = 4
DK = 256
DV = 512
RET_QK = HEADS * DK
RET_V = HEADS * DV
RET_IN = 2 * RET_QK + 2 * RET_V
ROPE_BASE = 10000.0
RMS_EPS = 1e-6
LN_EPS = 1e-5
ALPHA = 4.0 ** 0.25

POOL_WINDOWS = (2, 4, 8, 16)
POOL_GC = D // 4
POOL_STATE = 15
CARRY = 16

N_EXPERTS = 32
TOP_K = 4
SWIGLU_LIMIT = 7.0
SWIGLU_ALPHA = 1.702

LANES = 128
SUBLANES = 8
assert D == SUBLANES * LANES
SUPER = 256
CHUNK = 64

TM_PROJ = 1024
PROJ_COLS = 2 * RET_QK
assert PROJ_COLS == RET_V
TM = 512
BM = 512
NB = -(-(T * TOP_K) // BM) + N_EXPERTS
NB_PAD = -(-NB // 8) * 8
R_ROWS = NB * BM

VMEM_LIMIT = 56 * 1024 * 1024


def _cparams(sem, vmem=VMEM_LIMIT):
    return pltpu.CompilerParams(dimension_semantics=sem, vmem_limit_bytes=vmem)


def _proj_kernel(xp_ref, xs_ref, w_ref, cos_ref, sin_ref, o_ref, xb_ref):
    n = pl.program_id(1)

    @pl.when(n == 0)
    def _():
        x = jnp.where(pl.program_id(0) < TP // TM_PROJ, xp_ref[...], xs_ref[...])
        xb_ref[...] = x.astype(BF16)

    xb = xb_ref[...]

    @pl.when(n == 0)
    def _():
        half = DK // 2
        for h in range(2 * HEADS):
            scale = 1.0 if h < HEADS else DK ** -0.5
            cos = cos_ref[...] * scale
            sin = sin_ref[...] * scale
            p = jnp.dot(xb, w_ref[:, h * DK:(h + 1) * DK], preferred_element_type=F32)
            t1 = p[:, :half]
            t2 = p[:, half:]
            o_ref[:, h * DK:h * DK + half] = (t1 * cos - t2 * sin).astype(BF16)
            o_ref[:, h * DK + half:(h + 1) * DK] = (t1 * sin + t2 * cos).astype(BF16)

    @pl.when(n >= 1)
    def _():
        for h in range(PROJ_COLS // 256):
            p = jnp.dot(xb, w_ref[:, h * 256:(h + 1) * 256], preferred_element_type=F32)
            o_ref[:, h * 256:(h + 1) * 256] = p.astype(BF16)


def _ret_project(x_prompt, x_sample, w_bf16, cos_t, sin_t):
    n_prompt_tiles = TP // TM_PROJ
    tiles_per_seq = SEQ // TM_PROJ

    def tab_map(i, n):
        return (jnp.where(i < n_prompt_tiles, i % tiles_per_seq, tiles_per_seq), 0)

    return pl.pallas_call(
        _proj_kernel,
        grid=(T // TM_PROJ, RET_IN // PROJ_COLS),
        in_specs=[
            pl.BlockSpec((TM_PROJ, D), lambda i, n: (jnp.minimum(i, n_prompt_tiles - 1), 0)),
            pl.BlockSpec((TM_PROJ, D), lambda i, n: (jnp.maximum(i - n_prompt_tiles, 0), 0)),
            pl.BlockSpec((D, PROJ_COLS), lambda i, n: (0, n)),
            pl.BlockSpec((TM_PROJ, LANES), tab_map),
            pl.BlockSpec((TM_PROJ, LANES), tab_map),
        ],
        out_specs=pl.BlockSpec((TM_PROJ, PROJ_COLS), lambda i, n: (i, n)),
        out_shape=jax.ShapeDtypeStruct((T, RET_IN), BF16),
        scratch_shapes=[pltpu.VMEM((TM_PROJ, D), BF16)],
        compiler_params=_cparams(("arbitrary", "arbitrary")),
        name="ret_project",
    )(x_prompt, x_sample, w_bf16, cos_t, sin_t)


def _ret_core_kernel(has_init, n_steps, *refs):
    if has_init:
        (q_ref, k_ref, v_ref, mask_ref, qd_ref, kd_ref, bd_ref, s0_ref,
         o_ref, sout_ref, s_ref) = refs
    else:
        (q_ref, k_ref, v_ref, mask_ref, qd_ref, kd_ref, bd_ref,
         o_ref, sout_ref, s_ref) = refs
    c = pl.program_id(1)

    @pl.when(c == 0)
    def _():
        if has_init:
            s_ref[...] = s0_ref[...].astype(F32)
        else:
            s_ref[...] = jnp.zeros_like(s_ref)

    for h in range(HEADS):
        q = q_ref[:, h * DK:(h + 1) * DK]
        k = k_ref[:, h * DK:(h + 1) * DK]
        v = v_ref[:, h * DV:(h + 1) * DV]
        s_prev = s_ref[h]
        scores = lax.dot_general(q, k, (((1,), (1,)), ((), ())), preferred_element_type=F32)
        scores = scores * mask_ref[h]
        qd = (q.astype(F32) * qd_ref[h]).astype(BF16)
        o = (jnp.dot(scores.astype(BF16), v, preferred_element_type=F32)
             + jnp.dot(qd, s_prev.astype(BF16), preferred_element_type=F32))
        kd = (k.astype(F32) * kd_ref[h]).astype(BF16)
        s_new = s_prev * bd_ref[h] + lax.dot_general(kd, v, (((0,), (0,)), ((), ())),
                                                     preferred_element_type=F32)
        s_ref[h] = s_new
        o = o * lax.rsqrt(jnp.mean(o * o, axis=-1, keepdims=True) + RMS_EPS)
        o_ref[:, h * DV:(h + 1) * DV] = o.astype(BF16)

    @pl.when(c == n_steps - 1)
    def _():
        sout_ref[...] = s_ref[...]


def _decay_tables(rows):
    lg = jnp.log1p(-jnp.exp2(-5.0 - jnp.arange(HEADS, dtype=F32)))
    idx = jnp.arange(rows, dtype=F32)
    ch = jnp.arange(rows) // CHUNK
    diff = idx[:, None] - idx[None, :]
    same = ch[:, None] == ch[None, :]
    earlier = ch[None, :] < ch[:, None]
    expo = jnp.where(same, jnp.abs(diff), diff)
    w = jnp.exp(lg[:, None, None] * expo[None])
    mask = jnp.where((same | earlier)[None], w, 0.0).astype(F32)
    q_dec = jnp.exp(lg[:, None] * (idx + 1.0))[:, :, None]
    k_dec = jnp.exp(lg[:, None] * (rows - 1.0 - idx))[:, :, None]
    blk = jnp.broadcast_to(jnp.exp(lg * rows)[:, None, None], (HEADS, 1, DV))
    return mask, q_dec, k_dec, blk.astype(F32)


def _ret_core(p_all, n_seq, rows, n_steps, row_block0, s0=None):
    mask, q_dec, k_dec, blk = _decay_tables(rows)
    has_init = s0 is not None

    def rb(b, c):
        return row_block0 + b * n_steps + c

    def whole(shape):
        nd = len(shape)
        return pl.BlockSpec(shape, lambda b, c: (0,) * nd)

    in_specs = [
        pl.BlockSpec((rows, RET_QK), lambda b, c: (rb(b, c), 0)),
        pl.BlockSpec((rows, RET_QK), lambda b, c: (rb(b, c), 1)),
        pl.BlockSpec((rows, RET_V), lambda b, c: (rb(b, c), 1)),
        whole((HEADS, rows, rows)), whole((HEADS, rows, 1)), whole((HEADS, rows, 1)),
        whole((HEADS, 1, DV)),
    ]
    args = [p_all, p_all, p_all, mask, q_dec, k_dec, blk]
    if has_init:
        in_specs.append(pl.BlockSpec((HEADS, DK, DV), lambda b, c: (b, 0, 0)))
        args.append(s0)
    return pl.pallas_call(
        functools.partial(_ret_core_kernel, has_init, n_steps),
        grid=(n_seq, n_steps),
        in_specs=in_specs,
        out_specs=[
            pl.BlockSpec((rows, RET_V), lambda b, c: (b * n_steps + c, 0)),
            pl.BlockSpec((HEADS, DK, DV), lambda b, c: (b, 0, 0)),
        ],
        out_shape=[
            jax.ShapeDtypeStruct((n_seq * n_steps * rows, RET_V), BF16),
            jax.ShapeDtypeStruct((n_seq * HEADS, DK, DV), F32),
        ],
        scratch_shapes=[pltpu.VMEM((HEADS, DK, DV), F32)],
        compiler_params=_cparams(("arbitrary", "arbitrary")),
        name="ret_core_sample" if has_init else "ret_core_prompt",
    )(*args)


def _layer_norm(y, g, b):
    mu = jnp.mean(y, axis=-1, keepdims=True)
    yc = y - mu
    var = jnp.mean(yc * yc, axis=-1, keepdims=True)
    return yc * lax.rsqrt(var + LN_EPS) * g + b


def _split_bf16(a):
    hi = a.astype(BF16)
    lo = (a - hi.astype(F32)).astype(BF16)
    return hi, lo


def _route(x1, rw_ref, rb_ref, te_ref, gt_ref):
    rows = x1.shape[0]
    xh, xl = _split_bf16(x1)
    wh, wl = _split_bf16(rw_ref[...])
    logits = (jnp.dot(xh, wh, preferred_element_type=F32)
              + jnp.dot(xl, wh, preferred_element_type=F32)
              + jnp.dot(xh, wl, preferred_element_type=F32)) + rb_ref[...]
    lane = lax.broadcasted_iota(I32, (rows, N_EXPERTS), 1)
    lane_k = lax.broadcasted_iota(I32, (rows, TOP_K), 1)
    te = jnp.zeros((rows, TOP_K), I32)
    tv = jnp.zeros((rows, TOP_K), F32)
    cur = logits
    for j in range(TOP_K):
        m = jnp.max(cur, axis=-1, keepdims=True)
        idx = jnp.min(jnp.where(cur == m, lane, N_EXPERTS), axis=-1, keepdims=True)
        te = jnp.where(lane_k == j, idx, te)
        tv = jnp.where(lane_k == j, m, tv)
        cur = jnp.where(lane == idx, -jnp.inf, cur)
    ex = jnp.exp(tv - jnp.max(tv, axis=-1, keepdims=True))
    gt_ref[...] = ex / jnp.sum(ex, axis=-1, keepdims=True)
    te_ref[...] = te


def _load_row_tiles(ref, row0, rows, tok_stride=1):
    return jnp.concatenate(
        [ref[pl.ds(row0 * SUBLANES + s, rows, stride=SUBLANES * tok_stride), :]
         for s in range(SUBLANES)], axis=1)


def _store_row_tiles(ref, val, row0=0):
    rows = val.shape[0]
    for s in range(SUBLANES):
        ref[pl.ds(row0 * SUBLANES + s, rows, stride=SUBLANES), :] = val[:, s * LANES:(s + 1) * LANES]


def _residual_norm_route(x, h, lg_ref, lb_ref, rw_ref, rb_ref, x1_ref, te_ref, gt_ref):
    x1 = _layer_norm(ALPHA * x + h, lg_ref[...], lb_ref[...])
    x1_ref[...] = x1
    _route(x1, rw_ref, rb_ref, te_ref, gt_ref)


def _ret_out_kernel(op_ref, os_ref, g_ref, xp_ref, xs_ref, w_ref, lg_ref, lb_ref, rw_ref, rb_ref,
                    x1_ref, te_ref, gt_ref):
    is_prompt = pl.program_id(0) < TP // TM
    g = g_ref[...].astype(F32)
    o = jnp.where(is_prompt, op_ref[...], os_ref[...])
    a = (g * jax.nn.sigmoid(g) * o.astype(F32)).astype(BF16)
    h = jnp.dot(a, w_ref[...], preferred_element_type=F32)
    x = jnp.where(is_prompt, xp_ref[...], xs_ref[...])
    _residual_norm_route(x, h, lg_ref, lb_ref, rw_ref, rb_ref, x1_ref, te_ref, gt_ref)


def _row_spec(cols, col_block=0):
    return pl.BlockSpec((TM, cols), lambda i: (i, col_block))


def _const_spec(shape):
    nd = len(shape)
    return pl.BlockSpec(shape, lambda i: (0,) * nd)


def _tile_rows_spec(rows, index_map):
    return pl.BlockSpec((rows * SUBLANES, LANES), index_map)


_EPILOGUE_OUT_SPECS = [_row_spec(D), _row_spec(TOP_K), _row_spec(TOP_K)]
_EPILOGUE_OUT_SHAPE = [
    jax.ShapeDtypeStruct((T, D), F32),
    jax.ShapeDtypeStruct((T, TOP_K), I32),
    jax.ShapeDtypeStruct((T, TOP_K), F32),
]


def _ret_out(o_prompt, o_sample, p_all, x_prompt, x_sample, w_out_bf16, ln_g, ln_b, router_w,
             router_b):
    n_prompt_tiles = TP // TM

    def prompt_spec(cols):
        return pl.BlockSpec((TM, cols), lambda i: (jnp.minimum(i, n_prompt_tiles - 1), 0))

    def sample_spec(cols):
        return pl.BlockSpec((TM, cols), lambda i: (jnp.maximum(i - n_prompt_tiles, 0), 0))

    return pl.pallas_call(
        _ret_out_kernel,
        grid=(T // TM,),
        in_specs=[
            prompt_spec(RET_V), sample_spec(RET_V),
            _row_spec(RET_V, col_block=2),
            prompt_spec(D), sample_spec(D),
            _const_spec((RET_V, D)),
            _const_spec((1, D)), _const_spec((1, D)),
            _const_spec((D, N_EXPERTS)), _const_spec((1, N_EXPERTS)),
        ],
        out_specs=_EPILOGUE_OUT_SPECS,
        out_shape=_EPILOGUE_OUT_SHAPE,
        compiler_params=_cparams(("arbitrary",)),
        name="ret_out_norm_route",
    )(o_prompt, o_sample, p_all, x_prompt, x_sample, w_out_bf16, ln_g, ln_b, router_w, router_b)


SEQS_PER_TILE = TM // DEC_SEQ


def _group_cols(gi):
    return slice(gi * POOL_GC, (gi + 1) * POOL_GC)


def _band_tables():
    i = jnp.arange(TM)[:, None]
    j = jnp.arange(TM)[None, :]
    hi = jnp.arange(CARRY)[:, None]
    hr = jnp.arange(CARRY)[None, :]
    band_p, band_s, hist = [], [], []
    for w in POOL_WINDOWS:
        inside = (j <= i) & (j > i - w)
        band_p.append(inside)
        band_s.append(inside & (i // DEC_SEQ == j // DEC_SEQ))
        hist.append(hr >= hi + CARRY + 1 - w)
    as_bf16 = lambda ms: jnp.stack(ms).astype(BF16)
    return as_bf16(band_p), as_bf16(band_s), as_bf16(hist)


def _pool_kernel(x_ref, pre_ref, bandp_ref, bands_ref, hband_ref, win_ref, wg_ref, sc_ref, wout_ref,
                 lg_ref, lb_ref, rw_ref, rb_ref,
                 x1_ref, te_ref, gt_ref, cachep_ref, caches_ref, carry_ref, seq_ref, pooled_ref):
    i = pl.program_id(0)
    n_prompt_tiles = TP // TM
    tiles_per_seq = SEQ // TM
    is_prompt = i < n_prompt_tiles
    x = x_ref[...]
    u = jnp.dot(x.astype(BF16), win_ref[...], preferred_element_type=F32)
    u_hi, u_lo = _split_bf16(u)

    sums = []
    for gi in range(len(POOL_WINDOWS)):
        cols = _group_cols(gi)
        band = jnp.where(is_prompt, bandp_ref[gi], bands_ref[gi])
        sums.append(jnp.dot(band, u_hi[:, cols], preferred_element_type=F32)
                    + jnp.dot(band, u_lo[:, cols], preferred_element_type=F32))

    def history_sums(hist, gi):
        cols = _group_cols(gi)
        h_hi, h_lo = _split_bf16(hist[:, cols])
        return (jnp.dot(hband_ref[gi], h_hi, preferred_element_type=F32)
                + jnp.dot(hband_ref[gi], h_lo, preferred_element_type=F32))

    @pl.when(is_prompt)
    def _():
        j = i % tiles_per_seq

        @pl.when(j == 0)
        def _():
            carry_ref[...] = jnp.zeros_like(carry_ref)

        pos = (j * TM + lax.broadcasted_iota(I32, (TM, 1), 0)).astype(F32)
        hist = carry_ref[...]
        for gi, w in enumerate(POOL_WINDOWS):
            cols = _group_cols(gi)
            cnt = jnp.minimum(pos + 1.0, float(w))
            pooled_ref[:, cols] = sums[gi] / cnt - u[:, cols]
            head = sums[gi][0:CARRY, :] + history_sums(hist, gi)
            pooled_ref[0:CARRY, cols] = head / cnt[0:CARRY, :] - u[0:CARRY, cols]

        @pl.when(j == tiles_per_seq - 1)
        def _():
            cachep_ref[0] = u[TM - CARRY:TM, :]

        carry_ref[...] = u[TM - CARRY:TM, :]

    @pl.when(jnp.logical_not(is_prompt))
    def _():
        for gi, w in enumerate(POOL_WINDOWS):
            cols = _group_cols(gi)
            pooled_ref[:, cols] = sums[gi] * (1.0 / w) - u[:, cols]
        seq_ref[0:1, :] = jnp.zeros((1, D), F32)
        for r in range(SEQS_PER_TILE):
            r0 = r * DEC_SEQ
            seq_ref[1:CARRY, :] = pre_ref[r]
            hist = seq_ref[...]
            for gi, w in enumerate(POOL_WINDOWS):
                cols = _group_cols(gi)
                head = sums[gi][r0:r0 + CARRY, :] + history_sums(hist, gi)
                pooled_ref[r0:r0 + CARRY, cols] = head * (1.0 / w) - u[r0:r0 + CARRY, cols]
            caches_ref[r] = u[r0 + DEC_SEQ - CARRY:r0 + DEC_SEQ, :]

    mixed = [jnp.dot(pooled_ref[:, _group_cols(gi)].astype(BF16), wg_ref[gi],
                     preferred_element_type=F32) for gi in range(len(POOL_WINDOWS))]
    mixed = jnp.concatenate(mixed, axis=-1) * sc_ref[...]
    h = jnp.dot(mixed.astype(BF16), wout_ref[...], preferred_element_type=F32)
    _residual_norm_route(x, h, lg_ref, lb_ref, rw_ref, rb_ref, x1_ref, te_ref, gt_ref)


def _pool_mixer(x, prefix, win, wgrp, scale, wout, ln_g, ln_b, router_w, router_b):
    n_prompt_tiles = TP // TM
    tiles_per_seq = SEQ // TM
    n_win = len(POOL_WINDOWS)
    band_p, band_s, hist = _band_tables()
    return pl.pallas_call(
        _pool_kernel,
        grid=(T // TM,),
        in_specs=[
            _row_spec(D),
            pl.BlockSpec((SEQS_PER_TILE, POOL_STATE, D),
                         lambda i: (jnp.maximum(i - n_prompt_tiles, 0), 0, 0)),
            _const_spec((n_win, TM, TM)), _const_spec((n_win, TM, TM)),
            _const_spec((n_win, CARRY, CARRY)),
            _const_spec((D, D)), _const_spec((n_win, POOL_GC, POOL_GC)),
            _const_spec((1, D)), _const_spec((D, D)),
            _const_spec((1, D)), _const_spec((1, D)),
            _const_spec((D, N_EXPERTS)), _const_spec((1, N_EXPERTS)),
        ],
        out_specs=_EPILOGUE_OUT_SPECS + [
            pl.BlockSpec((1, CARRY, D),
                         lambda i: (jnp.minimum(i, n_prompt_tiles - 1) // tiles_per_seq, 0, 0)),
            pl.BlockSpec((SEQS_PER_TILE, CARRY, D),
                         lambda i: (jnp.maximum(i - n_prompt_tiles, 0), 0, 0)),
        ],
        out_shape=_EPILOGUE_OUT_SHAPE + [
            jax.ShapeDtypeStruct((BATCH, CARRY, D), F32),
            jax.ShapeDtypeStruct((DEC_BATCH, CARRY, D), F32),
        ],
        scratch_shapes=[
            pltpu.VMEM((CARRY, D), F32),
            pltpu.VMEM((CARRY, D), F32),
            pltpu.VMEM((TM, D), F32),
        ],
        compiler_params=_cparams(("arbitrary",)),
        name="pool_norm_route",
    )(x, prefix, band_p, band_s, hist, win, wgrp, scale, wout, ln_g, ln_b, router_w, router_b)


def _lane_cumsum(v):
    lane = lax.broadcasted_iota(I32, v.shape, 1)
    shift = 1
    while shift < LANES:
        v = v + jnp.where(lane >= shift, pltpu.roll(v, shift, axis=1), 0.0)
        shift *= 2
    return v


def _positions_kernel(te_ref, lpos_ref, tile_ref, blk_ref, stat_ref, cnt_ref, base_ref):
    phase = pl.program_id(0)
    i = pl.program_id(1)
    te = te_ref[...]
    lane = lax.broadcasted_iota(I32, (TM, LANES), 1)
    hits = [lane == te[:, j:j + 1] for j in range(TOP_K)]
    onehot = sum(h.astype(F32) for h in hits)
    tile_cnt = jnp.sum(onehot, axis=0, keepdims=True)

    @pl.when((phase == 0) & (i == 0))
    def _():
        cnt_ref[...] = jnp.zeros_like(cnt_ref)

    @pl.when(phase == 0)
    def _():
        cnt_ref[...] = cnt_ref[...] + tile_cnt

    @pl.when((phase == 1) & (i == 0))
    def _():
        cnt = cnt_ref[...].astype(I32)
        padded = ((cnt + (BM - 1)) & ~(BM - 1)).astype(F32)
        pad_end = _lane_cumsum(padded)
        base_ref[...] = pad_end - padded
        stat_ref[...] = jnp.concatenate([cnt_ref[...], pad_end - padded, pad_end], axis=0)
        lane8 = lax.broadcasted_iota(I32, (NB_PAD, LANES), 1)
        start = (lax.broadcasted_iota(I32, (NB_PAD, LANES), 0) * BM).astype(F32)
        done = jnp.where((lane8 < N_EXPERTS) & (pad_end[0:1, :] <= start), 1.0, 0.0)
        blk = jnp.minimum(jnp.sum(done, axis=-1, keepdims=True), N_EXPERTS - 1.0)
        blk_ref[...] = blk.astype(I32)

    @pl.when(phase == 1)
    def _():
        r = lax.broadcasted_iota(I32, (TM, TM), 0)
        c = lax.broadcasted_iota(I32, (TM, TM), 1)
        tri = jnp.where(r > c, 1.0, 0.0).astype(BF16)
        before = jnp.dot(tri, onehot.astype(BF16), preferred_element_type=F32)
        cnt8 = jnp.broadcast_to(tile_cnt, (SUBLANES, LANES))
        local_start = _lane_cumsum(cnt8) - cnt8
        slot = local_start[0:1, :] + before
        lpos = jnp.zeros((TM, LANES), F32)
        for j in range(TOP_K):
            pj = jnp.sum(jnp.where(hits[j], slot, 0.0), axis=-1, keepdims=True)
            lpos = jnp.where(lane == j, pj, lpos)
        lpos_ref[...] = lpos.astype(I32)
        row = lax.broadcasted_iota(I32, (SUBLANES, LANES), 0)
        tile_ref[0] = jnp.where(row == 0, cnt8, jnp.where(row == 1, base_ref[...], local_start))
        base_ref[...] = base_ref[...] + tile_cnt


def _positions(top_e):
    return pl.pallas_call(
        _positions_kernel,
        grid=(2, T // TM),
        in_specs=[pl.BlockSpec((TM, TOP_K), lambda p, i: (i, 0))],
        out_specs=[
            pl.BlockSpec((TM, LANES), lambda p, i: (i * p, 0)),
            pl.BlockSpec((1, SUBLANES, LANES), lambda p, i: (i * p, 0, 0)),
            pl.BlockSpec((NB_PAD, 1), lambda p, i: (0, 0)),
            pl.BlockSpec((24, LANES), lambda p, i: (0, 0)),
        ],
        out_shape=[
            jax.ShapeDtypeStruct((T, LANES), I32),
            jax.ShapeDtypeStruct((T // TM, SUBLANES, LANES), F32),
            jax.ShapeDtypeStruct((NB_PAD, 1), I32),
            jax.ShapeDtypeStruct((24, LANES), F32),
        ],
        scratch_shapes=[pltpu.VMEM((8, LANES), F32), pltpu.VMEM((8, LANES), F32)],
        compiler_params=_cparams(("arbitrary", "arbitrary")),
        name="moe_positions",
    )(top_e)


N_TILES = T // TM
TILE_ROWS = TM * TOP_K
RUN_SIZES = tuple(TM >> b for b in range(TM.bit_length()))
ZROWS = 256


def _tok_rows(ref, tok, n_tok):
    start = tok * SUBLANES
    if not isinstance(start, int):
        start = pl.multiple_of(start, SUBLANES)
    return ref.at[pl.ds(start, n_tok * SUBLANES)]


def _copy_runs(tile, cnt_ref, goff_ref, lst_ref, make_copy):
    def run(e, carry):
        n = cnt_ref[tile * N_EXPERTS + e]
        g = goff_ref[tile * N_EXPERTS + e]
        l = lst_ref[tile * N_EXPERTS + e]
        for piece, size in enumerate(RUN_SIZES):
            take = (n & size) != 0

            @pl.when(take)
            def _(g=g, l=l, size=size, piece=piece):
                make_copy(l, g, size).start(priority=piece % 2)

            step = jnp.where(take, size, 0)
            g = g + step
            l = l + step
        return carry

    lax.fori_loop(0, N_EXPERTS, run, 0)


def _dispatch_kernel(cnt_ref, goff_ref, lst_ref, ecnt_ref, estart_ref, x_ref, lpos_ref, xs_hbm,
                     buf0, buf1, zero_ref, sem_zero, sem_run):
    i = pl.program_id(0)
    bufs = (buf0, buf1)

    @pl.when(i == 0)
    def _():
        zero_ref[...] = jnp.zeros_like(zero_ref)

        def zero_copy(tok, n_tok):
            return pltpu.make_async_copy(zero_ref.at[pl.ds(0, n_tok * SUBLANES)],
                                         _tok_rows(xs_hbm, tok, n_tok), sem_zero)

        def fill_expert(e, carry):
            tok = estart_ref[e] + ecnt_ref[e]
            n_pad = (-ecnt_ref[e]) & (BM - 1)
            size = ZROWS
            while size >= 1:
                take = (n_pad & size) != 0

                @pl.when(take)
                def _(tok=tok, size=size):
                    cp = zero_copy(tok, size)
                    cp.start()
                    cp.wait()

                tok = tok + jnp.where(take, size, 0)
                size //= 2
            return carry

        lax.fori_loop(0, N_EXPERTS, fill_expert, 0)

        def fill_unused(c, carry):
            cp = zero_copy(c * ZROWS, ZROWS)
            cp.start()
            cp.wait()
            return carry

        last = N_EXPERTS - 1
        used_rows = estart_ref[last] + ((ecnt_ref[last] + (BM - 1)) & ~(BM - 1))
        lax.fori_loop(used_rows // ZROWS, R_ROWS // ZROWS, fill_unused, 0)

    xb = x_ref[...].astype(BF16)
    pos_t = lpos_ref[...].astype(F32).T

    def wait_runs(slot):
        pltpu.make_async_copy(bufs[slot], _tok_rows(xs_hbm, 0, TILE_ROWS), sem_run.at[slot]).wait()

    def sort_and_send(slot):
        buf = bufs[slot]

        @pl.when(i >= 2)
        def _():
            wait_runs(slot)

        for c in range(TILE_ROWS // TM):
            row = (c * TM + lax.broadcasted_iota(I32, (TM, TM), 0)).astype(F32)
            perm = jnp.zeros((TM, TM), F32)
            for k in range(TOP_K):
                perm = perm + jnp.where(row == pos_t[k:k + 1, :], 1.0, 0.0)
            rows = jnp.dot(perm.astype(BF16), xb, preferred_element_type=F32)
            _store_row_tiles(buf, rows, c * TM)

        _copy_runs(i, cnt_ref, goff_ref, lst_ref,
                   lambda l, g, size: pltpu.make_async_copy(_tok_rows(buf, l, size),
                                                            _tok_rows(xs_hbm, g, size),
                                                            sem_run.at[slot]))

        @pl.when(i == N_TILES - 1)
        def _():
            wait_runs(1 - slot)
            wait_runs(slot)

    for slot in range(2):
        @pl.when((i & 1) == slot)
        def _(slot=slot):
            sort_and_send(slot)


def _dispatch(x1, lpos, tile_cnt, tile_goff, tile_lst, counts, starts):
    return pl.pallas_call(
        _dispatch_kernel,
        grid_spec=pltpu.PrefetchScalarGridSpec(
            num_scalar_prefetch=5,
            grid=(N_TILES,),
            in_specs=[
                pl.BlockSpec((TM, D), lambda i, *_: (i, 0)),
                pl.BlockSpec((TM, LANES), lambda i, *_: (i, 0)),
            ],
            out_specs=pl.BlockSpec(memory_space=pl.ANY),
            scratch_shapes=[
                pltpu.VMEM((TILE_ROWS * SUBLANES, LANES), F32),
                pltpu.VMEM((TILE_ROWS * SUBLANES, LANES), F32),
                pltpu.VMEM((ZROWS * SUBLANES, LANES), F32),
                pltpu.SemaphoreType.DMA(()),
                pltpu.SemaphoreType.DMA((2,)),
            ],
        ),
        out_shape=jax.ShapeDtypeStruct((R_ROWS * SUBLANES, LANES), F32),
        compiler_params=_cparams(("arbitrary",)),
        name="moe_dispatch",
    )(tile_cnt, tile_goff, tile_lst, counts, starts, x1, lpos)


W_CHUNK = 64
H_CHUNK = 256


def _split_even_odd(x):
    rows, cols = x.shape
    lane = lax.broadcasted_iota(I32, (rows, LANES), 1)
    idx_even = (2 * lane) % LANES
    idx_odd = (2 * lane + 1) % LANES
    low = lane < LANES // 2
    even, odd = [], []
    for c in range(cols // (2 * LANES)):
        a = x[:, (2 * c) * LANES:(2 * c + 1) * LANES]
        b = x[:, (2 * c + 1) * LANES:(2 * c + 2) * LANES]
        even.append(jnp.where(low, jnp.take_along_axis(a, idx_even, axis=1),
                              jnp.take_along_axis(b, idx_even, axis=1)))
        odd.append(jnp.where(low, jnp.take_along_axis(a, idx_odd, axis=1),
                             jnp.take_along_axis(b, idx_odd, axis=1)))
    return jnp.concatenate(even, axis=1), jnp.concatenate(odd, axis=1)


def _expert_kernel(blk_ref, used_ref, xs_ref, wu_ref, wd_ref, bu_ref, bd_ref, ys_ref,
                   wg_s, wl_s, wd_s, bg_s, bl_s):
    i = pl.program_id(0)
    active = i < used_ref[0]
    new_expert = (i == 0) | (blk_ref[i] != blk_ref[jnp.maximum(i - 1, 0)])

    @pl.when(active & new_expert)
    def _():
        def convert(r, carry):
            rows = pl.ds(pl.multiple_of(r * W_CHUNK, W_CHUNK), W_CHUNK)
            even, odd = _split_even_odd(wu_ref[0, 0, rows, :])
            wg_s[rows, :] = even.astype(BF16)
            wl_s[rows, :] = odd.astype(BF16)
            wd_s[rows, :] = wd_ref[0, 0, rows, :].astype(BF16)
            return carry

        lax.fori_loop(0, D // W_CHUNK, convert, 0)
        even, odd = _split_even_odd(jnp.broadcast_to(bu_ref[0, 0], (SUBLANES, 2 * D)))
        bg_s[...] = even
        bl_s[...] = odd

    @pl.when(active)
    def _():
        xb = _load_row_tiles(xs_ref, 0, BM).astype(BF16)
        acts = []
        for c in range(D // H_CHUNK):
            cols = slice(c * H_CHUNK, (c + 1) * H_CHUNK)
            glu = jnp.dot(xb, wg_s[:, cols], preferred_element_type=F32) + bg_s[0:1, cols]
            lin = jnp.dot(xb, wl_s[:, cols], preferred_element_type=F32) + bl_s[0:1, cols]
            glu = jnp.minimum(glu, SWIGLU_LIMIT)
            lin = jnp.clip(lin, -SWIGLU_LIMIT, SWIGLU_LIMIT)
            acts.append((glu * jax.nn.sigmoid(SWIGLU_ALPHA * glu) * (lin + 1.0)).astype(BF16))
        a = jnp.concatenate(acts, axis=1)
        y = jnp.dot(a, wd_s[...], preferred_element_type=F32) + bd_ref[0, 0]
        _store_row_tiles(ys_ref, y)

    @pl.when(jnp.logical_not(active))
    def _():
        ys_ref[...] = jnp.zeros_like(ys_ref)


def _experts(layer, xs, blk_e, n_used, w_up, w_down, b_up, b_down):
    def row_map(i, blk, used):
        return (jnp.minimum(i, used[0] - 1), 0)

    def w_map(i, blk, used):
        return (layer, blk[jnp.minimum(i, used[0] - 1)], 0, 0)

    return pl.pallas_call(
        _expert_kernel,
        grid_spec=pltpu.PrefetchScalarGridSpec(
            num_scalar_prefetch=2,
            grid=(NB,),
            in_specs=[
                _tile_rows_spec(BM, row_map),
                pl.BlockSpec((1, 1, D, 2 * D), w_map),
                pl.BlockSpec((1, 1, D, D), w_map),
                pl.BlockSpec((1, 1, 1, 2 * D), w_map),
                pl.BlockSpec((1, 1, 1, D), w_map),
            ],
            out_specs=_tile_rows_spec(BM, lambda i, blk, used: (i, 0)),
            scratch_shapes=[
                pltpu.VMEM((D, D), BF16), pltpu.VMEM((D, D), BF16), pltpu.VMEM((D, D), BF16),
                pltpu.VMEM((SUBLANES, D), F32), pltpu.VMEM((SUBLANES, D), F32),
            ],
        ),
        out_shape=jax.ShapeDtypeStruct((R_ROWS * SUBLANES, LANES), F32),
        compiler_params=_cparams(("arbitrary",)),
        name="moe_experts",
    )(blk_e, n_used, xs, w_up, w_down, b_up, b_down)


N_TILES_PROMPT = TP // TM


def _combine_kernel(split_out, cnt_ref, goff_ref, lst_ref, x_ref, lpos_ref, gt_ref, lg_ref, lb_ref,
                    ys_hbm, *refs):
    if split_out:
        op_ref, os_ref, buf0, buf1, sem_run = refs
    else:
        o_ref, buf0, buf1, sem_run = refs
    bufs = (buf0, buf1)
    s = pl.program_id(0)

    def start_runs(slot):
        _copy_runs(s, cnt_ref, goff_ref, lst_ref,
                   lambda l, g, size: pltpu.make_async_copy(_tok_rows(ys_hbm, g, size),
                                                            _tok_rows(bufs[slot], l, size),
                                                            sem_run.at[slot]))

    def finish(slot):
        buf = bufs[slot]
        pltpu.make_async_copy(_tok_rows(ys_hbm, 0, TILE_ROWS), buf, sem_run.at[slot]).wait()
        gt = gt_ref[...]
        pos = lpos_ref[...]
        m = jnp.zeros((TM, D), F32)
        for c in range(TILE_ROWS // TM):
            col = c * TM + lax.broadcasted_iota(I32, (TM, TM), 1)
            weight = jnp.zeros((TM, TM), F32)
            for k in range(TOP_K):
                weight = weight + jnp.where(col == pos[:, k:k + 1], gt[:, k:k + 1], 0.0)
            rows = _load_row_tiles(buf, c * TM, TM).astype(BF16)
            m = m + jnp.dot(weight.astype(BF16), rows, preferred_element_type=F32)
        y = _layer_norm(ALPHA * x_ref[...] + m, lg_ref[...], lb_ref[...])
        if split_out:
            @pl.when(s - 1 < N_TILES_PROMPT)
            def _():
                op_ref[...] = y

            @pl.when(s - 1 >= N_TILES_PROMPT)
            def _():
                os_ref[...] = y
        else:
            o_ref[...] = y

    for slot in range(2):
        @pl.when((s < N_TILES) & ((s & 1) == slot))
        def _(slot=slot):
            start_runs(slot)

    for slot in range(2):
        @pl.when((s > 0) & (((s - 1) & 1) == slot))
        def _(slot=slot):
            finish(slot)


def _combine(x1, lpos, gates, ln_g, ln_b, ys, tile_cnt, tile_goff, tile_lst, split_out):
    def tile_map(s, *_):
        return (jnp.maximum(s - 1, 0), 0)

    if split_out:
        out_specs = [
            pl.BlockSpec((TM, D), lambda s, *_: (jnp.clip(s - 1, 0, N_TILES_PROMPT - 1), 0)),
            pl.BlockSpec((TM, D), lambda s, *_: (jnp.maximum(s - 1 - N_TILES_PROMPT, 0), 0)),
        ]
        out_shape = [jax.ShapeDtypeStruct((TP, D), F32), jax.ShapeDtypeStruct((TS, D), F32)]
    else:
        out_specs = pl.BlockSpec((TM, D), tile_map)
        out_shape = jax.ShapeDtypeStruct((T, D), F32)
    return pl.pallas_call(
        functools.partial(_combine_kernel, split_out),
        grid_spec=pltpu.PrefetchScalarGridSpec(
            num_scalar_prefetch=3,
            grid=(N_TILES + 1,),
            in_specs=[
                pl.BlockSpec((TM, D), tile_map),
                pl.BlockSpec((TM, LANES), tile_map),
                pl.BlockSpec((TM, TOP_K), tile_map),
                pl.BlockSpec((1, D), lambda s, *_: (0, 0)),
                pl.BlockSpec((1, D), lambda s, *_: (0, 0)),
                pl.BlockSpec(memory_space=pl.ANY),
            ],
            out_specs=out_specs,
            scratch_shapes=[
                pltpu.VMEM((TILE_ROWS * SUBLANES, LANES), F32),
                pltpu.VMEM((TILE_ROWS * SUBLANES, LANES), F32),
                pltpu.SemaphoreType.DMA((2,)),
            ],
        ),
        out_shape=out_shape,
        compiler_params=_cparams(("arbitrary",)),
        name="moe_combine_norm",
    )(tile_cnt, tile_goff, tile_lst, x1, lpos, gates, ln_g, ln_b, ys)


def _moe_layer(layer, x1, top_e, gates, w_up, b_up, w_down, b_down, ln_g, ln_b, split_out):
    lpos, tile_tab, blk_e, stat = _positions(top_e)

    def per_tile(row):
        return tile_tab[:, row, :N_EXPERTS].astype(I32).reshape(N_TILES * N_EXPERTS)

    tile_cnt, tile_goff, tile_lst = per_tile(0), per_tile(1), per_tile(2)
    counts = stat[0, :N_EXPERTS].astype(I32)
    starts = stat[8, :N_EXPERTS].astype(I32)
    n_used = (stat[16, N_EXPERTS - 1:N_EXPERTS].astype(I32)) // BM
    xs = _dispatch(x1, lpos, tile_cnt, tile_goff, tile_lst, counts, starts)
    ys = _experts(layer, xs, blk_e.reshape(NB_PAD), n_used, w_up, w_down, b_up[:, :, None, :],
                  b_down[:, :, None, :])
    return _combine(x1, lpos, gates, ln_g, ln_b, ys, tile_cnt, tile_goff, tile_lst, split_out)


def _rope_tables():
    half = DK // 2
    inv = 1.0 / (ROPE_BASE ** jnp.linspace(0.0, 1.0, half, dtype=F32))
    pos = jnp.concatenate([jnp.arange(SEQ), PAST_LEN + (jnp.arange(TM_PROJ) % DEC_SEQ)]).astype(F32)
    ang = pos[:, None] * inv[None, :]
    return jnp.cos(ang), jnp.sin(ang)


def kernel(x_prompt, x_sample, state_ret, cache_pool, ret_w_in, ret_w_out, pool_w_in, pool_w_grp,
           pool_scale, pool_w_out, ln1_g, ln1_b, ln2_g, ln2_b, router_w, router_b, w_up, b_up,
           w_down, b_down):
    xp = x_prompt.reshape(TP, D)
    xs = x_sample.reshape(TS, D)

    def vec(a):
        return a.reshape(1, -1)

    cos_t, sin_t = _rope_tables()
    p_all = _ret_project(xp, xs, ret_w_in[0].astype(BF16), cos_t, sin_t)
    o_prompt, s_prompt = _ret_core(p_all, BATCH, SUPER, SEQ // SUPER, 0)
    o_sample, s_sample = _ret_core(p_all, DEC_BATCH, DEC_SEQ, 1, TP // DEC_SEQ,
                                   s0=state_ret[0].reshape(DEC_BATCH * HEADS, DK, DV))
    x1, top_e, gates = _ret_out(o_prompt, o_sample, p_all, xp, xs, ret_w_out[0].astype(BF16),
                                vec(ln1_g[0]), vec(ln1_b[0]), router_w[0], vec(router_b[0]))
    x = _moe_layer(0, x1, top_e, gates, w_up, b_up, w_down, b_down, vec(ln2_g[0]), vec(ln2_b[0]),
                   split_out=False)

    x1, top_e, gates, cache_p, cache_s = _pool_mixer(
        x, cache_pool[0], pool_w_in[0].astype(BF16), pool_w_grp[0].astype(BF16),
        vec(pool_scale[0]), pool_w_out[0].astype(BF16), vec(ln1_g[1]), vec(ln1_b[1]), router_w[1],
        vec(router_b[1]))
    yp, ys = _moe_layer(1, x1, top_e, gates, w_up, b_up, w_down, b_down, vec(ln2_g[1]),
                        vec(ln2_b[1]), split_out=True)

    y_prompt = yp.reshape(BATCH, SEQ, D)
    y_sample = ys.reshape(DEC_BATCH, DEC_SEQ, D)
    state_ret_prompt = s_prompt.reshape(1, BATCH, HEADS, DK, DV)
    state_ret_sample = s_sample.reshape(1, DEC_BATCH, HEADS, DK, DV)
    cache_pool_prompt = cache_p[None, :, 1:, :]
    cache_pool_sample = cache_s[None, :, 1:, :]
    return (y_prompt, y_sample, state_ret_prompt, state_ret_sample, cache_pool_prompt,
            cache_pool_sample)
```

```python
import functools

import jax
import jax.numpy as jnp
from jax import lax
from jax.experimental import pallas as pl
from jax.experimental.pallas import tpu as pltpu

F32 = jnp.float32
BF16 = jnp.bfloat16
I32 = jnp.int32

D = 1024
BATCH = 16
SEQ = 2048
DEC_BATCH = 32
DEC_SEQ = 64
PAST_LEN = 4096
TP = BATCH * SEQ
TS = DEC_BATCH * DEC_SEQ
T = TP + TS

HEADS = 4
DK = 256
DV = 512
RET_QK = HEADS * DK
RET_V = HEADS * DV
RET_IN = 2 * RET_QK + 2 * RET_V
ROPE_BASE = 10000.0
RMS_EPS = 1e-6
LN_EPS = 1e-5
ALPHA = 4.0 ** 0.25

POOL_WINDOWS = (2, 4, 8, 16)
POOL_GC = D // 4
POOL_STATE = 15
CARRY = 16

N_EXPERTS = 32
TOP_K = 4
SWIGLU_LIMIT = 7.0
SWIGLU_ALPHA = 1.702

LANES = 128
SUBLANES = 8
assert D == SUBLANES * LANES
SUPER = 256
CHUNK = 64

TM_PROJ = 1024
PROJ_COLS = 2 * RET_QK
assert PROJ_COLS == RET_V
TM = 512
BM = 512
NB = -(-(T * TOP_K) // BM) + N_EXPERTS
NB_PAD = -(-NB // 8) * 8
R_ROWS = NB * BM

VMEM_LIMIT = 56 * 1024 * 1024


def _cparams(sem, vmem=VMEM_LIMIT):
    return pltpu.CompilerParams(dimension_semantics=sem, vmem_limit_bytes=vmem)


def _proj_kernel(xp_ref, xs_ref, w_ref, cos_ref, sin_ref, o_ref, xb_ref):
    n = pl.program_id(1)

    @pl.when(n == 0)
    def _():
        x = jnp.where(pl.program_id(0) < TP // TM_PROJ, xp_ref[...], xs_ref[...])
        xb_ref[...] = x.astype(BF16)

    xb = xb_ref[...]

    @pl.when(n == 0)
    def _():
        half = DK // 2
        for h in range(2 * HEADS):
            scale = 1.0 if h < HEADS else DK ** -0.5
            cos = cos_ref[...] * scale
            sin = sin_ref[...] * scale
            p = jnp.dot(xb, w_ref[:, h * DK:(h + 1) * DK], preferred_element_type=F32)
            t1 = p[:, :half]
            t2 = p[:, half:]
            o_ref[:, h * DK:h * DK + half] = (t1 * cos - t2 * sin).astype(BF16)
            o_ref[:, h * DK + half:(h + 1) * DK] = (t1 * sin + t2 * cos).astype(BF16)

    @pl.when(n >= 1)
    def _():
        for h in range(PROJ_COLS // 256):
            p = jnp.dot(xb, w_ref[:, h * 256:(h + 1) * 256], preferred_element_type=F32)
            o_ref[:, h * 256:(h + 1) * 256] = p.astype(BF16)


def _ret_project(x_prompt, x_sample, w_bf16, cos_t, sin_t):
    n_prompt_tiles = TP // TM_PROJ
    tiles_per_seq = SEQ // TM_PROJ

    def tab_map(i, n):
        return (jnp.where(i < n_prompt_tiles, i % tiles_per_seq, tiles_per_seq), 0)

    return pl.pallas_call(
        _proj_kernel,
        grid=(T // TM_PROJ, RET_IN // PROJ_COLS),
        in_specs=[
            pl.BlockSpec((TM_PROJ, D), lambda i, n: (jnp.minimum(i, n_prompt_tiles - 1), 0)),
            pl.BlockSpec((TM_PROJ, D), lambda i, n: (jnp.maximum(i - n_prompt_tiles, 0), 0)),
            pl.BlockSpec((D, PROJ_COLS), lambda i, n: (0, n)),
            pl.BlockSpec((TM_PROJ, LANES), tab_map),
            pl.BlockSpec((TM_PROJ, LANES), tab_map),
        ],
        out_specs=pl.BlockSpec((TM_PROJ, PROJ_COLS), lambda i, n: (i, n)),
        out_shape=jax.ShapeDtypeStruct((T, RET_IN), BF16),
        scratch_shapes=[pltpu.VMEM((TM_PROJ, D), BF16)],
        compiler_params=_cparams(("arbitrary", "arbitrary")),
        name="ret_project",
    )(x_prompt, x_sample, w_bf16, cos_t, sin_t)


def _ret_core_kernel(has_init, n_steps, *refs):
    if has_init:
        (q_ref, k_ref, v_ref, mask_ref, qd_ref, kd_ref, bd_ref, s0_ref,
         o_ref, sout_ref, s_ref) = refs
    else:
        (q_ref, k_ref, v_ref, mask_ref, qd_ref, kd_ref, bd_ref,
         o_ref, sout_ref, s_ref) = refs
    c = pl.program_id(1)

    @pl.when(c == 0)
    def _():
        if has_init:
            s_ref[...] = s0_ref[...].astype(F32)
        else:
            s_ref[...] = jnp.zeros_like(s_ref)

    for h in range(HEADS):
        q = q_ref[:, h * DK:(h + 1) * DK]
        k = k_ref[:, h * DK:(h + 1) * DK]
        v = v_ref[:, h * DV:(h + 1) * DV]
        s_prev = s_ref[h]
        scores = lax.dot_general(q, k, (((1,), (1,)), ((), ())), preferred_element_type=F32)
        scores = scores * mask_ref[h]
        qd = (q.astype(F32) * qd_ref[h]).astype(BF16)
        o = (jnp.dot(scores.astype(BF16), v, preferred_element_type=F32)
             + jnp.dot(qd, s_prev.astype(BF16), preferred_element_type=F32))
        kd = (k.astype(F32) * kd_ref[h]).astype(BF16)
        s_new = s_prev * bd_ref[h] + lax.dot_general(kd, v, (((0,), (0,)), ((), ())),
                                                     preferred_element_type=F32)
        s_ref[h] = s_new
        o = o * lax.rsqrt(jnp.mean(o * o, axis=-1, keepdims=True) + RMS_EPS)
        o_ref[:, h * DV:(h + 1) * DV] = o.astype(BF16)

    @pl.when(c == n_steps - 1)
    def _():
        sout_ref[...] = s_ref[...]


def _decay_tables(rows):
    lg = jnp.log1p(-jnp.exp2(-5.0 - jnp.arange(HEADS, dtype=F32)))
    idx = jnp.arange(rows, dtype=F32)
    ch = jnp.arange(rows) // CHUNK
    diff = idx[:, None] - idx[None, :]
    same = ch[:, None] == ch[None, :]
    earlier = ch[None, :] < ch[:, None]
    expo = jnp.where(same, jnp.abs(diff), diff)
    w = jnp.exp(lg[:, None, None] * expo[None])
    mask = jnp.where((same | earlier)[None], w, 0.0).astype(F32)
    q_dec = jnp.exp(lg[:, None] * (idx + 1.0))[:, :, None]
    k_dec = jnp.exp(lg[:, None] * (rows - 1.0 - idx))[:, :, None]
    blk = jnp.broadcast_to(jnp.exp(lg * rows)[:, None, None], (HEADS, 1, DV))
    return mask, q_dec, k_dec, blk.astype(F32)


def _ret_core(p_all, n_seq, rows, n_steps, row_block0, s0=None):
    mask, q_dec, k_dec, blk = _decay_tables(rows)
    has_init = s0 is not None

    def rb(b, c):
        return row_block0 + b * n_steps + c

    def whole(shape):
        nd = len(shape)
        return pl.BlockSpec(shape, lambda b, c: (0,) * nd)

    in_specs = [
        pl.BlockSpec((rows, RET_QK), lambda b, c: (rb(b, c), 0)),
        pl.BlockSpec((rows, RET_QK), lambda b, c: (rb(b, c), 1)),
        pl.BlockSpec((rows, RET_V), lambda b, c: (rb(b, c), 1)),
        whole((HEADS, rows, rows)), whole((HEADS, rows, 1)), whole((HEADS, rows, 1)),
        whole((HEADS, 1, DV)),
    ]
    args = [p_all, p_all, p_all, mask, q_dec, k_dec, blk]
    if has_init:
        in_specs.append(pl.BlockSpec((HEADS, DK, DV), lambda b, c: (b, 0, 0)))
        args.append(s0)
    return pl.pallas_call(
        functools.partial(_ret_core_kernel, has_init, n_steps),
        grid=(n_seq, n_steps),
        in_specs=in_specs,
        out_specs=[
            pl.BlockSpec((rows, RET_V), lambda b, c: (b * n_steps + c, 0)),
            pl.BlockSpec((HEADS, DK, DV), lambda b, c: (b, 0, 0)),
        ],
        out_shape=[
            jax.ShapeDtypeStruct((n_seq * n_steps * rows, RET_V), BF16),
            jax.ShapeDtypeStruct((n_seq * HEADS, DK, DV), F32),
        ],
        scratch_shapes=[pltpu.VMEM((HEADS, DK, DV), F32)],
        compiler_params=_cparams(("arbitrary", "arbitrary")),
        name="ret_core_sample" if has_init else "ret_core_prompt",
    )(*args)


def _layer_norm(y, g, b):
    mu = jnp.mean(y, axis=-1, keepdims=True)
    yc = y - mu
    var = jnp.mean(yc * yc, axis=-1, keepdims=True)
    return yc * lax.rsqrt(var + LN_EPS) * g + b


def _split_bf16(a):
    hi = a.astype(BF16)
    lo = (a - hi.astype(F32)).astype(BF16)
    return hi, lo


def _route(x1, rw_ref, rb_ref, te_ref, gt_ref):
    rows = x1.shape[0]
    xh, xl = _split_bf16(x1)
    wh, wl = _split_bf16(rw_ref[...])
    logits = (jnp.dot(xh, wh, preferred_element_type=F32)
              + jnp.dot(xl, wh, preferred_element_type=F32)
              + jnp.dot(xh, wl, preferred_element_type=F32)) + rb_ref[...]
    lane = lax.broadcasted_iota(I32, (rows, N_EXPERTS), 1)
    lane_k = lax.broadcasted_iota(I32, (rows, TOP_K), 1)
    te = jnp.zeros((rows, TOP_K), I32)
    tv = jnp.zeros((rows, TOP_K), F32)
    cur = logits
    for j in range(TOP_K):
        m = jnp.max(cur, axis=-1, keepdims=True)
        idx = jnp.min(jnp.where(cur == m, lane, N_EXPERTS), axis=-1, keepdims=True)
        te = jnp.where(lane_k == j, idx, te)
        tv = jnp.where(lane_k == j, m, tv)
        cur = jnp.where(lane == idx, -jnp.inf, cur)
    ex = jnp.exp(tv - jnp.max(tv, axis=-1, keepdims=True))
    gt_ref[...] = ex / jnp.sum(ex, axis=-1, keepdims=True)
    te_ref[...] = te


def _load_row_tiles(ref, row0, rows, tok_stride=1):
    return jnp.concatenate(
        [ref[pl.ds(row0 * SUBLANES + s, rows, stride=SUBLANES * tok_stride), :]
         for s in range(SUBLANES)], axis=1)


def _store_row_tiles(ref, val, row0=0):
    rows = val.shape[0]
    for s in range(SUBLANES):
        ref[pl.ds(row0 * SUBLANES + s, rows, stride=SUBLANES), :] = val[:, s * LANES:(s + 1) * LANES]


def _residual_norm_route(x, h, lg_ref, lb_ref, rw_ref, rb_ref, x1_ref, te_ref, gt_ref):
    x1 = _layer_norm(ALPHA * x + h, lg_ref[...], lb_ref[...])
    x1_ref[...] = x1
    _route(x1, rw_ref, rb_ref, te_ref, gt_ref)


def _ret_out_kernel(op_ref, os_ref, g_ref, xp_ref, xs_ref, w_ref, lg_ref, lb_ref, rw_ref, rb_ref,
                    x1_ref, te_ref, gt_ref):
    is_prompt = pl.program_id(0) < TP // TM
    g = g_ref[...].astype(F32)
    o = jnp.where(is_prompt, op_ref[...], os_ref[...])
    a = (g * jax.nn.sigmoid(g) * o.astype(F32)).astype(BF16)
    h = jnp.dot(a, w_ref[...], preferred_element_type=F32)
    x = jnp.where(is_prompt, xp_ref[...], xs_ref[...])
    _residual_norm_route(x, h, lg_ref, lb_ref, rw_ref, rb_ref, x1_ref, te_ref, gt_ref)


def _row_spec(cols, col_block=0):
    return pl.BlockSpec((TM, cols), lambda i: (i, col_block))


def _const_spec(shape):
    nd = len(shape)
    return pl.BlockSpec(shape, lambda i: (0,) * nd)


def _tile_rows_spec(rows, index_map):
    return pl.BlockSpec((rows * SUBLANES, LANES), index_map)


_EPILOGUE_OUT_SPECS = [_row_spec(D), _row_spec(TOP_K), _row_spec(TOP_K)]
_EPILOGUE_OUT_SHAPE = [
    jax.ShapeDtypeStruct((T, D), F32),
    jax.ShapeDtypeStruct((T, TOP_K), I32),
    jax.ShapeDtypeStruct((T, TOP_K), F32),
]


def _ret_out(o_prompt, o_sample, p_all, x_prompt, x_sample, w_out_bf16, ln_g, ln_b, router_w,
             router_b):
    n_prompt_tiles = TP // TM

    def prompt_spec(cols):
        return pl.BlockSpec((TM, cols), lambda i: (jnp.minimum(i, n_prompt_tiles - 1), 0))

    def sample_spec(cols):
        return pl.BlockSpec((TM, cols), lambda i: (jnp.maximum(i - n_prompt_tiles, 0), 0))

    return pl.pallas_call(
        _ret_out_kernel,
        grid=(T // TM,),
        in_specs=[
            prompt_spec(RET_V), sample_spec(RET_V),
            _row_spec(RET_V, col_block=2),
            prompt_spec(D), sample_spec(D),
            _const_spec((RET_V, D)),
            _const_spec((1, D)), _const_spec((1, D)),
            _const_spec((D, N_EXPERTS)), _const_spec((1, N_EXPERTS)),
        ],
        out_specs=_EPILOGUE_OUT_SPECS,
        out_shape=_EPILOGUE_OUT_SHAPE,
        compiler_params=_cparams(("arbitrary",)),
        name="ret_out_norm_route",
    )(o_prompt, o_sample, p_all, x_prompt, x_sample, w_out_bf16, ln_g, ln_b, router_w, router_b)


SEQS_PER_TILE = TM // DEC_SEQ


def _group_cols(gi):
    return slice(gi * POOL_GC, (gi + 1) * POOL_GC)


def _band_tables():
    i = jnp.arange(TM)[:, None]
    j = jnp.arange(TM)[None, :]
    hi = jnp.arange(CARRY)[:, None]
    hr = jnp.arange(CARRY)[None, :]
    band_p, band_s, hist = [], [], []
    for w in POOL_WINDOWS:
        inside = (j <= i) & (j > i - w)
        band_p.append(inside)
        band_s.append(inside & (i // DEC_SEQ == j // DEC_SEQ))
        hist.append(hr >= hi + CARRY + 1 - w)
    as_bf16 = lambda ms: jnp.stack(ms).astype(BF16)
    return as_bf16(band_p), as_bf16(band_s), as_bf16(hist)


def _pool_kernel(x_ref, pre_ref, bandp_ref, bands_ref, hband_ref, win_ref, wg_ref, sc_ref, wout_ref,
                 lg_ref, lb_ref, rw_ref, rb_ref,
                 x1_ref, te_ref, gt_ref, cachep_ref, caches_ref, carry_ref, seq_ref, pooled_ref):
    i = pl.program_id(0)
    n_prompt_tiles = TP // TM
    tiles_per_seq = SEQ // TM
    is_prompt = i < n_prompt_tiles
    x = x_ref[...]
    u = jnp.dot(x.astype(BF16), win_ref[...], preferred_element_type=F32)
    u_hi, u_lo = _split_bf16(u)

    sums = []
    for gi in range(len(POOL_WINDOWS)):
        cols = _group_cols(gi)
        band = jnp.where(is_prompt, bandp_ref[gi], bands_ref[gi])
        sums.append(jnp.dot(band, u_hi[:, cols], preferred_element_type=F32)
                    + jnp.dot(band, u_lo[:, cols], preferred_element_type=F32))

    def history_sums(hist, gi):
        cols = _group_cols(gi)
        h_hi, h_lo = _split_bf16(hist[:, cols])
        return (jnp.dot(hband_ref[gi], h_hi, preferred_element_type=F32)
                + jnp.dot(hband_ref[gi], h_lo, preferred_element_type=F32))

    @pl.when(is_prompt)
    def _():
        j = i % tiles_per_seq

        @pl.when(j == 0)
        def _():
            carry_ref[...] = jnp.zeros_like(carry_ref)

        pos = (j * TM + lax.broadcasted_iota(I32, (TM, 1), 0)).astype(F32)
        hist = carry_ref[...]
        for gi, w in enumerate(POOL_WINDOWS):
            cols = _group_cols(gi)
            cnt = jnp.minimum(pos + 1.0, float(w))
            pooled_ref[:, cols] = sums[gi] / cnt - u[:, cols]
            head = sums[gi][0:CARRY, :] + history_sums(hist, gi)
            pooled_ref[0:CARRY, cols] = head / cnt[0:CARRY, :] - u[0:CARRY, cols]

        @pl.when(j == tiles_per_seq - 1)
        def _():
            cachep_ref[0] = u[TM - CARRY:TM, :]

        carry_ref[...] = u[TM - CARRY:TM, :]

    @pl.when(jnp.logical_not(is_prompt))
    def _():
        for gi, w in enumerate(POOL_WINDOWS):
            cols = _group_cols(gi)
            pooled_ref[:, cols] = sums[gi] * (1.0 / w) - u[:, cols]
        seq_ref[0:1, :] = jnp.zeros((1, D), F32)
        for r in range(SEQS_PER_TILE):
            r0 = r * DEC_SEQ
            seq_ref[1:CARRY, :] = pre_ref[r]
            hist = seq_ref[...]
            for gi, w in enumerate(POOL_WINDOWS):
                cols = _group_cols(gi)
                head = sums[gi][r0:r0 + CARRY, :] + history_sums(hist, gi)
                pooled_ref[r0:r0 + CARRY, cols] = head * (1.0 / w) - u[r0:r0 + CARRY, cols]
            caches_ref[r] = u[r0 + DEC_SEQ - CARRY:r0 + DEC_SEQ, :]

    mixed = [jnp.dot(pooled_ref[:, _group_cols(gi)].astype(BF16), wg_ref[gi],
                     preferred_element_type=F32) for gi in range(len(POOL_WINDOWS))]
    mixed = jnp.concatenate(mixed, axis=-1) * sc_ref[...]
    h = jnp.dot(mixed.astype(BF16), wout_ref[...], preferred_element_type=F32)
    _residual_norm_route(x, h, lg_ref, lb_ref, rw_ref, rb_ref, x1_ref, te_ref, gt_ref)


def _pool_mixer(x, prefix, win, wgrp, scale, wout, ln_g, ln_b, router_w, router_b):
    n_prompt_tiles = TP // TM
    tiles_per_seq = SEQ // TM
    n_win = len(POOL_WINDOWS)
    band_p, band_s, hist = _band_tables()
    return pl.pallas_call(
        _pool_kernel,
        grid=(T // TM,),
        in_specs=[
            _row_spec(D),
            pl.BlockSpec((SEQS_PER_TILE, POOL_STATE, D),
                         lambda i: (jnp.maximum(i - n_prompt_tiles, 0), 0, 0)),
            _const_spec((n_win, TM, TM)), _const_spec((n_win, TM, TM)),
            _const_spec((n_win, CARRY, CARRY)),
            _const_spec((D, D)), _const_spec((n_win, POOL_GC, POOL_GC)),
            _const_spec((1, D)), _const_spec((D, D)),
            _const_spec((1, D)), _const_spec((1, D)),
            _const_spec((D, N_EXPERTS)), _const_spec((1, N_EXPERTS)),
        ],
        out_specs=_EPILOGUE_OUT_SPECS + [
            pl.BlockSpec((1, CARRY, D),
                         lambda i: (jnp.minimum(i, n_prompt_tiles - 1) // tiles_per_seq, 0, 0)),
            pl.BlockSpec((SEQS_PER_TILE, CARRY, D),
                         lambda i: (jnp.maximum(i - n_prompt_tiles, 0), 0, 0)),
        ],
        out_shape=_EPILOGUE_OUT_SHAPE + [
            jax.ShapeDtypeStruct((BATCH, CARRY, D), F32),
            jax.ShapeDtypeStruct((DEC_BATCH, CARRY, D), F32),
        ],
        scratch_shapes=[
            pltpu.VMEM((CARRY, D), F32),
            pltpu.VMEM((CARRY, D), F32),
            pltpu.VMEM((TM, D), F32),
        ],
        compiler_params=_cparams(("arbitrary",)),
        name="pool_norm_route",
    )(x, prefix, band_p, band_s, hist, win, wgrp, scale, wout, ln_g, ln_b, router_w, router_b)


def _lane_cumsum(v):
    lane = lax.broadcasted_iota(I32, v.shape, 1)
    shift = 1
    while shift < LANES:
        v = v + jnp.where(lane >= shift, pltpu.roll(v, shift, axis=1), 0.0)
        shift *= 2
    return v


def _positions_kernel(te_ref, lpos_ref, tile_ref, blk_ref, stat_ref, cnt_ref, base_ref):
    phase = pl.program_id(0)
    i = pl.program_id(1)
    te = te_ref[...]
    lane = lax.broadcasted_iota(I32, (TM, LANES), 1)
    hits = [lane == te[:, j:j + 1] for j in range(TOP_K)]
    onehot = sum(h.astype(F32) for h in hits)
    tile_cnt = jnp.sum(onehot, axis=0, keepdims=True)

    @pl.when((phase == 0) & (i == 0))
    def _():
        cnt_ref[...] = jnp.zeros_like(cnt_ref)

    @pl.when(phase == 0)
    def _():
        cnt_ref[...] = cnt_ref[...] + tile_cnt

    @pl.when((phase == 1) & (i == 0))
    def _():
        cnt = cnt_ref[...].astype(I32)
        padded = ((cnt + (BM - 1)) & ~(BM - 1)).astype(F32)
        pad_end = _lane_cumsum(padded)
        base_ref[...] = pad_end - padded
        stat_ref[...] = jnp.concatenate([cnt_ref[...], pad_end - padded, pad_end], axis=0)
        lane8 = lax.broadcasted_iota(I32, (NB_PAD, LANES), 1)
        start = (lax.broadcasted_iota(I32, (NB_PAD, LANES), 0) * BM).astype(F32)
        done = jnp.where((lane8 < N_EXPERTS) & (pad_end[0:1, :] <= start), 1.0, 0.0)
        blk = jnp.minimum(jnp.sum(done, axis=-1, keepdims=True), N_EXPERTS - 1.0)
        blk_ref[...] = blk.astype(I32)

    @pl.when(phase == 1)
    def _():
        r = lax.broadcasted_iota(I32, (TM, TM), 0)
        c = lax.broadcasted_iota(I32, (TM, TM), 1)
        tri = jnp.where(r > c, 1.0, 0.0).astype(BF16)
        before = jnp.dot(tri, onehot.astype(BF16), preferred_element_type=F32)
        cnt8 = jnp.broadcast_to(tile_cnt, (SUBLANES, LANES))
        local_start = _lane_cumsum(cnt8) - cnt8
        slot = local_start[0:1, :] + before
        lpos = jnp.zeros((TM, LANES), F32)
        for j in range(TOP_K):
            pj = jnp.sum(jnp.where(hits[j], slot, 0.0), axis=-1, keepdims=True)
            lpos = jnp.where(lane == j, pj, lpos)
        lpos_ref[...] = lpos.astype(I32)
        row = lax.broadcasted_iota(I32, (SUBLANES, LANES), 0)
        tile_ref[0] = jnp.where(row == 0, cnt8, jnp.where(row == 1, base_ref[...], local_start))
        base_ref[...] = base_ref[...] + tile_cnt


def _positions(top_e):
    return pl.pallas_call(
        _positions_kernel,
        grid=(2, T // TM),
        in_specs=[pl.BlockSpec((TM, TOP_K), lambda p, i: (i, 0))],
        out_specs=[
            pl.BlockSpec((TM, LANES), lambda p, i: (i * p, 0)),
            pl.BlockSpec((1, SUBLANES, LANES), lambda p, i: (i * p, 0, 0)),
            pl.BlockSpec((NB_PAD, 1), lambda p, i: (0, 0)),
            pl.BlockSpec((24, LANES), lambda p, i: (0, 0)),
        ],
        out_shape=[
            jax.ShapeDtypeStruct((T, LANES), I32),
            jax.ShapeDtypeStruct((T // TM, SUBLANES, LANES), F32),
            jax.ShapeDtypeStruct((NB_PAD, 1), I32),
            jax.ShapeDtypeStruct((24, LANES), F32),
        ],
        scratch_shapes=[pltpu.VMEM((8, LANES), F32), pltpu.VMEM((8, LANES), F32)],
        compiler_params=_cparams(("arbitrary", "arbitrary")),
        name="moe_positions",
    )(top_e)


N_TILES = T // TM
TILE_ROWS = TM * TOP_K
RUN_SIZES = tuple(TM >> b for b in range(TM.bit_length()))
ZROWS = 256


def _tok_rows(ref, tok, n_tok):
    start = tok * SUBLANES
    if not isinstance(start, int):
        start = pl.multiple_of(start, SUBLANES)
    return ref.at[pl.ds(start, n_tok * SUBLANES)]


def _copy_runs(tile, cnt_ref, goff_ref, lst_ref, make_copy):
    def run(e, carry):
        n = cnt_ref[tile * N_EXPERTS + e]
        g = goff_ref[tile * N_EXPERTS + e]
        l = lst_ref[tile * N_EXPERTS + e]
        for piece, size in enumerate(RUN_SIZES):
            take = (n & size) != 0

            @pl.when(take)
            def _(g=g, l=l, size=size, piece=piece):
                make_copy(l, g, size).start(priority=piece % 2)

            step = jnp.where(take, size, 0)
            g = g + step
            l = l + step
        return carry

    lax.fori_loop(0, N_EXPERTS, run, 0)


def _dispatch_kernel(cnt_ref, goff_ref, lst_ref, ecnt_ref, estart_ref, x_ref, lpos_ref, xs_hbm,
                     buf0, buf1, zero_ref, sem_zero, sem_run):
    i = pl.program_id(0)
    bufs = (buf0, buf1)

    @pl.when(i == 0)
    def _():
        zero_ref[...] = jnp.zeros_like(zero_ref)

        def zero_copy(tok, n_tok):
            return pltpu.make_async_copy(zero_ref.at[pl.ds(0, n_tok * SUBLANES)],
                                         _tok_rows(xs_hbm, tok, n_tok), sem_zero)

        def fill_expert(e, carry):
            tok = estart_ref[e] + ecnt_ref[e]
            n_pad = (-ecnt_ref[e]) & (BM - 1)
            size = ZROWS
            while size >= 1:
                take = (n_pad & size) != 0

                @pl.when(take)
                def _(tok=tok, size=size):
                    cp = zero_copy(tok, size)
                    cp.start()
                    cp.wait()

                tok = tok + jnp.where(take, size, 0)
                size //= 2
            return carry

        lax.fori_loop(0, N_EXPERTS, fill_expert, 0)

        def fill_unused(c, carry):
            cp = zero_copy(c * ZROWS, ZROWS)
            cp.start()
            cp.wait()
            return carry

        last = N_EXPERTS - 1
        used_rows = estart_ref[last] + ((ecnt_ref[last] + (BM - 1)) & ~(BM - 1))
        lax.fori_loop(used_rows // ZROWS, R_ROWS // ZROWS, fill_unused, 0)

    xb = x_ref[...].astype(BF16)
    pos_t = lpos_ref[...].astype(F32).T

    def wait_runs(slot):
        pltpu.make_async_copy(bufs[slot], _tok_rows(xs_hbm, 0, TILE_ROWS), sem_run.at[slot]).wait()

    def sort_and_send(slot):
        buf = bufs[slot]

        @pl.when(i >= 2)
        def _():
            wait_runs(slot)

        for c in range(TILE_ROWS // TM):
            row = (c * TM + lax.broadcasted_iota(I32, (TM, TM), 0)).astype(F32)
            perm = jnp.zeros((TM, TM), F32)
            for k in range(TOP_K):
                perm = perm + jnp.where(row == pos_t[k:k + 1, :], 1.0, 0.0)
            rows = jnp.dot(perm.astype(BF16), xb, preferred_element_type=F32)
            _store_row_tiles(buf, rows, c * TM)

        _copy_runs(i, cnt_ref, goff_ref, lst_ref,
                   lambda l, g, size: pltpu.make_async_copy(_tok_rows(buf, l, size),
                                                            _tok_rows(xs_hbm, g, size),
                                                            sem_run.at[slot]))

        @pl.when(i == N_TILES - 1)
        def _():
            wait_runs(1 - slot)
            wait_runs(slot)

    for slot in range(2):
        @pl.when((i & 1) == slot)
        def _(slot=slot):
            sort_and_send(slot)


def _dispatch(x1, lpos, tile_cnt, tile_goff, tile_lst, counts, starts):
    return pl.pallas_call(
        _dispatch_kernel,
        grid_spec=pltpu.PrefetchScalarGridSpec(
            num_scalar_prefetch=5,
            grid=(N_TILES,),
            in_specs=[
                pl.BlockSpec((TM, D), lambda i, *_: (i, 0)),
                pl.BlockSpec((TM, LANES), lambda i, *_: (i, 0)),
            ],
            out_specs=pl.BlockSpec(memory_space=pl.ANY),
            scratch_shapes=[
                pltpu.VMEM((TILE_ROWS * SUBLANES, LANES), F32),
                pltpu.VMEM((TILE_ROWS * SUBLANES, LANES), F32),
                pltpu.VMEM((ZROWS * SUBLANES, LANES), F32),
                pltpu.SemaphoreType.DMA(()),
                pltpu.SemaphoreType.DMA((2,)),
            ],
        ),
        out_shape=jax.ShapeDtypeStruct((R_ROWS * SUBLANES, LANES), F32),
        compiler_params=_cparams(("arbitrary",)),
        name="moe_dispatch",
    )(tile_cnt, tile_goff, tile_lst, counts, starts, x1, lpos)


W_CHUNK = 64


def _split_even_odd(x):
    rows, cols = x.shape
    lane = lax.broadcasted_iota(I32, (rows, LANES), 1)
    idx_even = (2 * lane) % LANES
    idx_odd = (2 * lane + 1) % LANES
    low = lane < LANES // 2
    even, odd = [], []
    for c in range(cols // (2 * LANES)):
        a = x[:, (2 * c) * LANES:(2 * c + 1) * LANES]
        b = x[:, (2 * c + 1) * LANES:(2 * c + 2) * LANES]
        even.append(jnp.where(low, jnp.take_along_axis(a, idx_even, axis=1),
                              jnp.take_along_axis(b, idx_even, axis=1)))
        odd.append(jnp.where(low, jnp.take_along_axis(a, idx_odd, axis=1),
                             jnp.take_along_axis(b, idx_odd, axis=1)))
    return jnp.concatenate(even, axis=1), jnp.concatenate(odd, axis=1)


def _expert_kernel(blk_ref, used_ref, xs_ref, wu_ref, wd_ref, bu_ref, bd_ref, ys_ref,
                   wg_s, wl_s, wd_s, bg_s, bl_s):
    i = pl.program_id(0)
    active = i < used_ref[0]
    new_expert = (i == 0) | (blk_ref[i] != blk_ref[jnp.maximum(i - 1, 0)])

    @pl.when(active & new_expert)
    def _():
        def convert(r, carry):
            rows = pl.ds(pl.multiple_of(r * W_CHUNK, W_CHUNK), W_CHUNK)
            even, odd = _split_even_odd(wu_ref[0, 0, rows, :])
            wg_s[rows, :] = even.astype(BF16)
            wl_s[rows, :] = odd.astype(BF16)
            wd_s[rows, :] = wd_ref[0, 0, rows, :].astype(BF16)
            return carry

        lax.fori_loop(0, D // W_CHUNK, convert, 0)
        even, odd = _split_even_odd(jnp.broadcast_to(bu_ref[0, 0], (SUBLANES, 2 * D)))
        bg_s[...] = even
        bl_s[...] = odd

    @pl.when(active)
    def _():
        xb = _load_row_tiles(xs_ref, 0, BM).astype(BF16)
        glu = jnp.dot(xb, wg_s[...], preferred_element_type=F32) + bg_s[0:1, :]
        lin = jnp.dot(xb, wl_s[...], preferred_element_type=F32) + bl_s[0:1, :]
        glu = jnp.minimum(glu, SWIGLU_LIMIT)
        lin = jnp.clip(lin, -SWIGLU_LIMIT, SWIGLU_LIMIT)
        a = glu * jax.nn.sigmoid(SWIGLU_ALPHA * glu) * (lin + 1.0)
        y = jnp.dot(a.astype(BF16), wd_s[...], preferred_element_type=F32) + bd_ref[0, 0]
        _store_row_tiles(ys_ref, y)

    @pl.when(jnp.logical_not(active))
    def _():
        ys_ref[...] = jnp.zeros_like(ys_ref)


def _experts(layer, xs, blk_e, n_used, w_up, w_down, b_up, b_down):
    def row_map(i, blk, used):
        return (jnp.minimum(i, used[0] - 1), 0)

    def w_map(i, blk, used):
        return (layer, blk[jnp.minimum(i, used[0] - 1)], 0, 0)

    return pl.pallas_call(
        _expert_kernel,
        grid_spec=pltpu.PrefetchScalarGridSpec(
            num_scalar_prefetch=2,
            grid=(NB,),
            in_specs=[
                _tile_rows_spec(BM, row_map),
                pl.BlockSpec((1, 1, D, 2 * D), w_map),
                pl.BlockSpec((1, 1, D, D), w_map),
                pl.BlockSpec((1, 1, 1, 2 * D), w_map),
                pl.BlockSpec((1, 1, 1, D), w_map),
            ],
            out_specs=_tile_rows_spec(BM, lambda i, blk, used: (i, 0)),
            scratch_shapes=[
                pltpu.VMEM((D, D), BF16), pltpu.VMEM((D, D), BF16), pltpu.VMEM((D, D), BF16),
                pltpu.VMEM((SUBLANES, D), F32), pltpu.VMEM((SUBLANES, D), F32),
            ],
        ),
        out_shape=jax.ShapeDtypeStruct((R_ROWS * SUBLANES, LANES), F32),
        compiler_params=_cparams(("arbitrary",)),
        name="moe_experts",
    )(blk_e, n_used, xs, w_up, w_down, b_up, b_down)


N_TILES_PROMPT = TP // TM


def _combine_kernel(split_out, cnt_ref, goff_ref, lst_ref, x_ref, lpos_ref, gt_ref, lg_ref, lb_ref,
                    ys_hbm, *refs):
    if split_out:
        op_ref, os_ref, buf0, buf1, sem_run = refs
    else:
        o_ref, buf0, buf1, sem_run = refs
    bufs = (buf0, buf1)
    s = pl.program_id(0)

    def start_runs(slot):
        _copy_runs(s, cnt_ref, goff_ref, lst_ref,
                   lambda l, g, size: pltpu.make_async_copy(_tok_rows(ys_hbm, g, size),
                                                            _tok_rows(bufs[slot], l, size),
                                                            sem_run.at[slot]))

    def finish(slot):
        buf = bufs[slot]
        pltpu.make_async_copy(_tok_rows(ys_hbm, 0, TILE_ROWS), buf, sem_run.at[slot]).wait()
        gt = gt_ref[...]
        pos = lpos_ref[...]
        m = jnp.zeros((TM, D), F32)
        for c in range(TILE_ROWS // TM):
            col = c * TM + lax.broadcasted_iota(I32, (TM, TM), 1)
            weight = jnp.zeros((TM, TM), F32)
            for k in range(TOP_K):
                weight = weight + jnp.where(col == pos[:, k:k + 1], gt[:, k:k + 1], 0.0)
            rows = _load_row_tiles(buf, c * TM, TM).astype(BF16)
            m = m + jnp.dot(weight.astype(BF16), rows, preferred_element_type=F32)
        y = _layer_norm(ALPHA * x_ref[...] + m, lg_ref[...], lb_ref[...])
        if split_out:
            @pl.when(s - 1 < N_TILES_PROMPT)
            def _():
                op_ref[...] = y

            @pl.when(s - 1 >= N_TILES_PROMPT)
            def _():
                os_ref[...] = y
        else:
            o_ref[...] = y

    for slot in range(2):
        @pl.when((s < N_TILES) & ((s & 1) == slot))
        def _(slot=slot):
            start_runs(slot)

    for slot in range(2):
        @pl.when((s > 0) & (((s - 1) & 1) == slot))
        def _(slot=slot):
            finish(slot)


def _combine(x1, lpos, gates, ln_g, ln_b, ys, tile_cnt, tile_goff, tile_lst, split_out):
    def tile_map(s, *_):
        return (jnp.maximum(s - 1, 0), 0)

    if split_out:
        out_specs = [
            pl.BlockSpec((TM, D), lambda s, *_: (jnp.clip(s - 1, 0, N_TILES_PROMPT - 1), 0)),
            pl.BlockSpec((TM, D), lambda s, *_: (jnp.maximum(s - 1 - N_TILES_PROMPT, 0), 0)),
        ]
        out_shape = [jax.ShapeDtypeStruct((TP, D), F32), jax.ShapeDtypeStruct((TS, D), F32)]
    else:
        out_specs = pl.BlockSpec((TM, D), tile_map)
        out_shape = jax.ShapeDtypeStruct((T, D), F32)
    return pl.pallas_call(
        functools.partial(_combine_kernel, split_out),
        grid_spec=pltpu.PrefetchScalarGridSpec(
            num_scalar_prefetch=3,
            grid=(N_TILES + 1,),
            in_specs=[
                pl.BlockSpec((TM, D), tile_map),
                pl.BlockSpec((TM, LANES), tile_map),
                pl.BlockSpec((TM, TOP_K), tile_map),
                pl.BlockSpec((1, D), lambda s, *_: (0, 0)),
                pl.BlockSpec((1, D), lambda s, *_: (0, 0)),
                pl.BlockSpec(memory_space=pl.ANY),
            ],
            out_specs=out_specs,
            scratch_shapes=[
                pltpu.VMEM((TILE_ROWS * SUBLANES, LANES), F32),
                pltpu.VMEM((TILE_ROWS * SUBLANES, LANES), F32),
                pltpu.SemaphoreType.DMA((2,)),
            ],
        ),
        out_shape=out_shape,
        compiler_params=_cparams(("arbitrary",)),
        name="moe_combine_norm",
    )(tile_cnt, tile_goff, tile_lst, x1, lpos, gates, ln_g, ln_b, ys)


def _moe_layer(layer, x1, top_e, gates, w_up, b_up, w_down, b_down, ln_g, ln_b, split_out):
    lpos, tile_tab, blk_e, stat = _positions(top_e)

    def per_tile(row):
        return tile_tab[:, row, :N_EXPERTS].astype(I32).reshape(N_TILES * N_EXPERTS)

    tile_cnt, tile_goff, tile_lst = per_tile(0), per_tile(1), per_tile(2)
    counts = stat[0, :N_EXPERTS].astype(I32)
    starts = stat[8, :N_EXPERTS].astype(I32)
    n_used = (stat[16, N_EXPERTS - 1:N_EXPERTS].astype(I32)) // BM
    xs = _dispatch(x1, lpos, tile_cnt, tile_goff, tile_lst, counts, starts)
    ys = _experts(layer, xs, blk_e.reshape(NB_PAD), n_used, w_up, w_down, b_up[:, :, None, :],
                  b_down[:, :, None, :])
    return _combine(x1, lpos, gates, ln_g, ln_b, ys, tile_cnt, tile_goff, tile_lst, split_out)


def _rope_tables():
    half = DK // 2
    inv = 1.0 / (ROPE_BASE ** jnp.linspace(0.0, 1.0, half, dtype=F32))
    pos = jnp.concatenate([jnp.arange(SEQ), PAST_LEN + (jnp.arange(TM_PROJ) % DEC_SEQ)]).astype(F32)
    ang = pos[:, None] * inv[None, :]
    return jnp.cos(ang), jnp.sin(ang)


def kernel(x_prompt, x_sample, state_ret, cache_pool, ret_w_in, ret_w_out, pool_w_in, pool_w_grp,
           pool_scale, pool_w_out, ln1_g, ln1_b, ln2_g, ln2_b, router_w, router_b, w_up, b_up,
           w_down, b_down):
    xp = x_prompt.reshape(TP, D)
    xs = x_sample.reshape(TS, D)

    def vec(a):
        return a.reshape(1, -1)

    cos_t, sin_t = _rope_tables()
    p_all = _ret_project(xp, xs, ret_w_in[0].astype(BF16), cos_t, sin_t)
    o_prompt, s_prompt = _ret_core(p_all, BATCH, SUPER, SEQ // SUPER, 0)
    o_sample, s_sample = _ret_core(p_all, DEC_BATCH, DEC_SEQ, 1, TP // DEC_SEQ,
                                   s0=state_ret[0].reshape(DEC_BATCH * HEADS, DK, DV))
    x1, top_e, gates = _ret_out(o_prompt, o_sample, p_all, xp, xs, ret_w_out[0].astype(BF16),
                                vec(ln1_g[0]), vec(ln1_b[0]), router_w[0], vec(router_b[0]))
    x = _moe_layer(0, x1, top_e, gates, w_up, b_up, w_down, b_down, vec(ln2_g[0]), vec(ln2_b[0]),
                   split_out=False)

    x1, top_e, gates, cache_p, cache_s = _pool_mixer(
        x, cache_pool[0], pool_w_in[0].astype(BF16), pool_w_grp[0].astype(BF16),
        vec(pool_scale[0]), pool_w_out[0].astype(BF16), vec(ln1_g[1]), vec(ln1_b[1]), router_w[1],
        vec(router_b[1]))
    yp, ys = _moe_layer(1, x1, top_e, gates, w_up, b_up, w_down, b_down, vec(ln2_g[1]),
                        vec(ln2_b[1]), split_out=True)

    y_prompt = yp.reshape(BATCH, SEQ, D)
    y_sample = ys.reshape(DEC_BATCH, DEC_SEQ, D)
    state_ret_prompt = s_prompt.reshape(1, BATCH, HEADS, DK, DV)
    state_ret_sample = s_sample.reshape(1, DEC_BATCH, HEADS, DK, DV)
    cache_pool_prompt = cache_p[None, :, 1:, :]
    cache_pool_sample = cache_s[None, :, 1:, :]
    return (y_prompt, y_sample, state_ret_prompt, state_ret_sample, cache_pool_prompt,
            cache_pool_sample)
```
